```python
import math
import jax, jax.numpy as jnp
from jax import lax
import numpy as np

D_MODEL = 2048
BATCH = 16
SEQ = 256
DEPTH = 2
DEC_BATCH = 4
DEC_SEQ = 1024
PAST_LEN = 256

GRID_W = 64
D_MIX = D_MODEL
D_RWKV = D_MIX // 2
D_HYENA = D_MIX - D_RWKV
HEAD_DIM = 64
N_HEADS = D_RWKV // HEAD_DIM
LORA_W = 64
LORA_A = 64
LORA_G = 160
D_FF = 4 * D_MODEL
FILT_BANDS = 16
FILT_EMB = 1 + 2 * FILT_BANDS
FILT_HIDDEN = 64
HYENA_ORDER = 2
N_DIR = 2
N_CONV = 3 * D_RWKV + 3 * D_HYENA
D_IN = N_CONV + LORA_W + LORA_A + LORA_G
ALPHA = (2 * DEPTH) ** 0.25
BETA = (8 * DEPTH) ** -0.25
LN_EPS = 1e-5
GN_EPS = 64e-5
FILT_TARGET = 1e-2
FAST_DECAY_PCT = 0.3
SLOW_DECAY_PCT = 1.5
DECAY_SCALE = math.exp(-0.5)

kernel_name = "hybrid_rwkv7_hyena_diffusion_step"


def layer_norm(x, g=None, b=None, eps=LN_EPS):
    xf = x.astype(jnp.float32)
    mu = jnp.mean(xf, axis=-1, keepdims=True)
    var = jnp.mean(jnp.square(xf - mu), axis=-1, keepdims=True)
    y = (xf - mu) * lax.rsqrt(var + eps)
    if g is not None:
        y = y * g.astype(jnp.float32) + b.astype(jnp.float32)
    return y.astype(x.dtype)


def short_conv(u, w, on_grid):
    bsz, seq, ch = u.shape
    if on_grid:
        rows = seq // GRID_W
        y = lax.conv_general_dilated(
            u.reshape(bsz, rows, GRID_W, ch), w[:, :, None, :].astype(u.dtype),
            window_strides=(1, 1), padding='SAME',
            dimension_numbers=('NHWC', 'HWIO', 'NHWC'), feature_group_count=ch)
        return y.reshape(bsz, seq, ch)
    wr = w[1].astype(u.dtype)
    up = jnp.pad(u, ((0, 0), (1, 1), (0, 0)))
    return up[:, :-2] * wr[0] + up[:, 1:-1] * wr[1] + up[:, 2:] * wr[2]


def wkv_scan(s0, r, w, k, v, kk, iclr, reverse):
    b = kk * iclr

    def step(S, inp):
        r_t, w_t, k_t, v_t, kk_t, b_t = inp
        sa = jnp.einsum('bhvk,bhk->bhv', S, kk_t)
        S = S * w_t[:, :, None, :] - sa[..., None] * b_t[:, :, None, :] + v_t[..., None] * k_t[:, :, None, :]
        o_t = jnp.einsum('bhvk,bhk->bhv', S, r_t)
        return S, o_t

    xs = tuple(jnp.moveaxis(t, 1, 0) for t in (r, w, k, v, kk, b))
    S, o = lax.scan(step, s0, xs, reverse=reverse)
    return S, jnp.moveaxis(o, 0, 1)


def rwkv_mixer(r, k, v, xw, xa, xg, p, s0):
    bsz, seq, _ = r.shape

    def heads(t):
        return t.reshape(bsz, seq, N_HEADS, HEAD_DIM)

    kk = heads(k * p['k_k'].astype(jnp.float32))
    kk = kk * lax.rsqrt(jnp.sum(kk * kk, axis=-1, keepdims=True) + 1e-12)
    g = jax.nn.sigmoid(xg) @ p['g_up'].astype(jnp.float32)
    rh, vh = heads(r), heads(v)
    r_k = p['r_k'].astype(jnp.float32)
    outs, states, bonus = [], [], []
    for d in range(N_DIR):
        logw = -DECAY_SCALE * jax.nn.sigmoid(p['w0'][d].astype(jnp.float32) + jnp.tanh(xw) @ p['w_up'][d].astype(jnp.float32))
        iclr = jax.nn.sigmoid(p['a0'][d].astype(jnp.float32) + xa @ p['a_up'][d].astype(jnp.float32))
        kd = heads(k * (1.0 + (iclr - 1.0) * p['k_a'].astype(jnp.float32)))
        S, o = wkv_scan(s0[:, d].astype(jnp.float32), rh, heads(jnp.exp(logw)), kd, vh, kk, heads(iclr), reverse=(d == 1))
        outs.append(o)
        states.append(S)
        bonus.append(jnp.sum(rh * kd * r_k, axis=-1, keepdims=True) * vh)
    o = outs[0] + outs[1]
    mu = jnp.mean(o, axis=-1, keepdims=True)
    var = jnp.mean(jnp.square(o - mu), axis=-1, keepdims=True)
    o = ((o - mu) * lax.rsqrt(var + GN_EPS)).reshape(bsz, seq, D_RWKV)
    o = o * p['ln_g'].astype(jnp.float32) + p['ln_b'].astype(jnp.float32)
    o = (o + (bonus[0] + bonus[1]).reshape(bsz, seq, D_RWKV)) * g
    return o, jnp.stack(states, axis=1)


def hyena_filters(seq, p):
    t = jnp.linspace(0.0, 1.0, seq, dtype=jnp.float32)[:, None]
    w = (2.0 * math.pi / seq) * jnp.arange(seq, dtype=jnp.float32)[:, None]
    f = jnp.linspace(1e-4, FILT_BANDS - 1, FILT_BANDS, dtype=jnp.float32)[None, :]
    z = jnp.concatenate([t, jnp.cos(f * w), -jnp.sin(f * w)], axis=-1)
    freq = p['filt_freq'].astype(jnp.float32)
    h = jnp.sin(freq[0] * (z @ p['filt_w1'].astype(jnp.float32) + p['filt_b1'].astype(jnp.float32)))
    h = jnp.sin(freq[1] * (h @ p['filt_w2'].astype(jnp.float32) + p['filt_b2'].astype(jnp.float32)))
    h = (h @ p['filt_w3'].astype(jnp.float32)).reshape(seq, HYENA_ORDER, N_DIR, D_HYENA)
    max_decay = math.log(FILT_TARGET) / FAST_DECAY_PCT
    min_decay = math.log(FILT_TARGET) / SLOW_DECAY_PCT
    deltas = jnp.abs(jnp.linspace(min_decay, max_decay, D_HYENA, dtype=jnp.float32))
    h = h * jnp.exp(-t * deltas)[:, None, None, :]
    return h / jnp.sum(jnp.abs(h), axis=(0, 2), keepdims=True)


def fft_long_conv(u, h_f, h_b, bias):
    seq, ch = h_f.shape
    k2 = jnp.concatenate([h_f, jnp.zeros((1, ch), jnp.float32), h_b[1:][::-1]], axis=0)
    y = jnp.fft.irfft(jnp.fft.rfft(u, n=2 * seq, axis=1) * jnp.fft.rfft(k2, axis=0)[None], n=2 * seq, axis=1)[:, :seq]
    return y + u * bias


def hyena_mixer(v, x1, x2, p):
    h = hyena_filters(v.shape[1], p)
    bias = p['hyena_bias'].astype(jnp.float32)
    z = x1 * fft_long_conv(v, h[:, 0, 0], h[:, 0, 1], bias[0])
    return x2 * fft_long_conv(z, h[:, 1, 0], h[:, 1, 1], bias[1])


def trunk_layer(x, cond, p, s0, on_grid):
    mod = (jax.nn.silu(cond) @ p['w_ada'] + p['b_ada'])[:, None, :]
    sh1, sc1, g1, sh2, sc2, g2 = jnp.split(mod, 6, axis=-1)
    h = layer_norm(x) * (1.0 + sc1) + sh1
    proj = h @ p['w_in']
    cv = short_conv(proj[..., :N_CONV], p['conv_w'], on_grid).astype(jnp.float32)
    r = cv[..., 0:D_RWKV]
    k = cv[..., D_RWKV:2 * D_RWKV]
    v = cv[..., 2 * D_RWKV:3 * D_RWKV]
    o3 = 3 * D_RWKV
    hv = cv[..., o3:o3 + D_HYENA]
    hx1 = cv[..., o3 + D_HYENA:o3 + 2 * D_HYENA]
    hx2 = cv[..., o3 + 2 * D_HYENA:o3 + 3 * D_HYENA]
    lo = proj[..., N_CONV:].astype(jnp.float32)
    xw = lo[..., :LORA_W]
    xa = lo[..., LORA_W:LORA_W + LORA_A]
    xg = lo[..., LORA_W + LORA_A:]
    a_out, S = rwkv_mixer(r, k, v, xw, xa, xg, p, s0)
    b_out = hyena_mixer(hv, hx1, hx2, p)
    mix = jnp.concatenate([a_out, b_out], axis=-1).astype(x.dtype) @ p['w_out']
    x = layer_norm(ALPHA * x + g1 * mix, p['ln1_g'], p['ln1_b'])
    h = layer_norm(x) * (1.0 + sc2) + sh2
    ff = jnp.square(jax.nn.relu(h @ p['mlp_w1'])) @ p['mlp_w2']
    x = layer_norm(ALPHA * x + g2 * ff, p['ln2_g'], p['ln2_b'])
    return x, S


def setup_inputs(seed: int = 0) -> dict:
    key = jax.random.key(seed)
    ks = jax.random.split(key, 40)

    def nrm(i, shape, scale):
        return scale * jax.random.normal(ks[i], shape, jnp.float32)

    col_scale = jnp.ones((D_IN,), jnp.float32)
    col_scale = col_scale.at[2 * D_RWKV:3 * D_RWKV].set(BETA).at[3 * D_RWKV:3 * D_RWKV + D_HYENA].set(BETA)
    w_in = nrm(7, (DEPTH, D_MODEL, D_IN), D_MODEL ** -0.5) * col_scale
    conv_w = nrm(8, (DEPTH, 3, 3, N_CONV), 0.2).at[:, 1, 1, :].add(1.0)
    return {
        'x_prompt': nrm(0, (BATCH, SEQ, D_MODEL), 1.0),
        'x_sample': nrm(1, (DEC_BATCH, DEC_SEQ, D_MODEL), 1.0),
        'c': nrm(2, (DEC_BATCH, D_MODEL), 1.0),
        'state_rwkv': nrm(3, (DEC_BATCH, DEPTH, N_DIR, N_HEADS, HEAD_DIM, HEAD_DIM), 0.3),
        'c_ctx': nrm(4, (D_MODEL,), 1.0),
        'w_ada': nrm(5, (DEPTH, D_MODEL, 6 * D_MODEL), 0.5 * D_MODEL ** -0.5),
        'b_ada': nrm(6, (DEPTH, 6 * D_MODEL), 0.02),
        'w_in': w_in,
        'conv_w': conv_w,
        'lora_w_up': nrm(9, (DEPTH, N_DIR, LORA_W, D_RWKV), 0.1),
        'lora_w0': nrm(10, (DEPTH, N_DIR, D_RWKV), 1.0),
        'lora_a_up': nrm(11, (DEPTH, N_DIR, LORA_A, D_RWKV), 0.5 * LORA_A ** -0.5),
        'lora_a0': nrm(12, (DEPTH, N_DIR, D_RWKV), 0.5),
        'lora_g_up': nrm(13, (DEPTH, LORA_G, D_RWKV), LORA_G ** -0.5),
        'rwkv_k_k': 0.85 + nrm(14, (DEPTH, D_RWKV), 0.1),
        'rwkv_k_a': 1.0 + nrm(15, (DEPTH, D_RWKV), 0.1),
        'rwkv_r_k': nrm(16, (DEPTH, N_HEADS, HEAD_DIM), 0.1),
        'rwkv_ln_g': 1.0 + nrm(17, (DEPTH, D_RWKV), 0.1),
        'rwkv_ln_b': nrm(18, (DEPTH, D_RWKV), 0.02),
        'filt_w1': nrm(19, (DEPTH, FILT_EMB, FILT_HIDDEN), FILT_EMB ** -0.5),
        'filt_b1': nrm(20, (DEPTH, FILT_HIDDEN), 0.1),
        'filt_w2': nrm(21, (DEPTH, FILT_HIDDEN, FILT_HIDDEN), FILT_HIDDEN ** -0.5),
        'filt_b2': nrm(22, (DEPTH, FILT_HIDDEN), 0.1),
        'filt_w3': nrm(23, (DEPTH, FILT_HIDDEN, HYENA_ORDER * N_DIR * D_HYENA), FILT_HIDDEN ** -0.5),
        'filt_freq': 1.0 + nrm(24, (DEPTH, 2, FILT_HIDDEN), 0.1),
        'hyena_bias': nrm(25, (DEPTH, HYENA_ORDER, D_HYENA), 0.5),
        'w_out': nrm(26, (DEPTH, D_MIX, D_MODEL), BETA * D_MIX ** -0.5),
        'ln1_g': 1.0 + nrm(27, (DEPTH, D_MODEL), 0.1),
        'ln1_b': nrm(28, (DEPTH, D_MODEL), 0.02),
        'ln2_g': 1.0 + nrm(29, (DEPTH, D_MODEL), 0.1),
        'ln2_b': nrm(30, (DEPTH, D_MODEL), 0.02),
        'mlp_w1': nrm(31, (DEPTH, D_MODEL, D_FF), D_MODEL ** -0.5),
        'mlp_w2': nrm(32, (DEPTH, D_FF, D_MODEL), BETA * D_FF ** -0.5),
    }


def reference(x_prompt, x_sample, c, state_rwkv, c_ctx, w_ada, b_ada, w_in, conv_w,
              lora_w_up, lora_w0, lora_a_up, lora_a0, lora_g_up, rwkv_k_k, rwkv_k_a, rwkv_r_k,
              rwkv_ln_g, rwkv_ln_b, filt_w1, filt_b1, filt_w2, filt_b2, filt_w3, filt_freq,
              hyena_bias, w_out, ln1_g, ln1_b, ln2_g, ln2_b, mlp_w1, mlp_w2):
    xp, xs = x_prompt, x_sample
    n_ctx_req = x_prompt.shape[0]
    ctx_states = []
    for i in range(DEPTH):
        p = {
            'w_ada': w_ada[i], 'b_ada': b_ada[i], 'w_in': w_in[i], 'conv_w': conv_w[i],
            'w_up': lora_w_up[i], 'w0': lora_w0[i], 'a_up': lora_a_up[i], 'a0': lora_a0[i],
            'g_up': lora_g_up[i], 'k_k': rwkv_k_k[i], 'k_a': rwkv_k_a[i], 'r_k': rwkv_r_k[i],
            'ln_g': rwkv_ln_g[i], 'ln_b': rwkv_ln_b[i],
            'filt_w1': filt_w1[i], 'filt_b1': filt_b1[i], 'filt_w2': filt_w2[i], 'filt_b2': filt_b2[i],
            'filt_w3': filt_w3[i], 'filt_freq': filt_freq[i], 'hyena_bias': hyena_bias[i],
            'w_out': w_out[i], 'ln1_g': ln1_g[i], 'ln1_b': ln1_b[i], 'ln2_g': ln2_g[i], 'ln2_b': ln2_b[i],
            'mlp_w1': mlp_w1[i], 'mlp_w2': mlp_w2[i],
        }
        s_zero = jnp.zeros((n_ctx_req, N_DIR, N_HEADS, HEAD_DIM, HEAD_DIM), jnp.float32)
        xp, s_ctx = trunk_layer(xp, c_ctx[None, :], p, s_zero, on_grid=False)
        ctx_states.append(s_ctx)
        xs, _ = trunk_layer(xs, c, p, state_rwkv[:, i], on_grid=True)
    new_state_rwkv = jnp.stack(ctx_states, axis=1).astype(x_prompt.dtype)
    return (xp, xs, new_state_rwkv)
```

```python
import functools
import math

import jax
import jax.numpy as jnp
import numpy as np
from jax import lax
from jax.experimental import pallas as pl
from jax.experimental.pallas import tpu as pltpu

F32 = jnp.float32
BF16 = jnp.bfloat16

D_MODEL = 2048
D_RWKV = 1024
D_HYENA = 1024
HEAD_DIM = 64
N_HEADS = D_RWKV // HEAD_DIM
N_PAIRS = N_HEADS // 2
PAIR_W = 2 * HEAD_DIM
LORA_W = 64
LORA_A = 64
LORA_G = 160
LORA_G_PAD = 256
N_CONV = 3 * D_RWKV + 3 * D_HYENA
D_IN = N_CONV + LORA_W + LORA_A + LORA_G
D_FF = 4 * D_MODEL
GRID_W = 64
CTX_SEQ = 256
LAT_SEQ = 1024
FILT_BANDS = 16
FILT_EMB = 1 + 2 * FILT_BANDS
FILT_EMB_PAD = 128
FILT_HIDDEN = 64
DEPTH = 2
ALPHA = (2 * DEPTH) ** 0.25
LN_EPS = 1e-5
GN_EPS = 64e-5
FILT_TARGET = 1e-2
FAST_DECAY_PCT = 0.3
SLOW_DECAY_PCT = 1.5
DECAY_SCALE = math.exp(-0.5)

TM = 1024
TN_IN = 512
NJ_CONV = N_CONV // TN_IN
D_IN_PAD = (NJ_CONV + 1) * TN_IN
LO_BLOCK = N_CONV // TN_IN
CONV_PAD = 72
CHUNK = 64
N_COND = 8
TN_MOD = 1024
TM_OUT = 512
TF_MLP = 512
VMEM_LIMIT = 56 * 1024 * 1024


def _cparams(sem):
    return pltpu.CompilerParams(dimension_semantics=sem, vmem_limit_bytes=VMEM_LIMIT)


def _dot(a, b):
    return jnp.dot(a.astype(BF16), b.astype(BF16), preferred_element_type=F32)


def _dot_nt(a, b):
    return lax.dot_general(a.astype(BF16), b.astype(BF16), (((1,), (1,)), ((), ())),
                           preferred_element_type=F32)


def _dot_tn(a, b):
    return lax.dot_general(a.astype(BF16), b.astype(BF16), (((0,), (0,)), ((), ())),
                           preferred_element_type=F32)


def _split(x):
    hi = x.astype(BF16)
    lo = (x - hi.astype(F32)).astype(BF16)
    return hi, lo


def _dot3(a, b):
    ah, al = _split(a)
    bh, bl = _split(b)
    return _dot(ah, bh) + (_dot(ah, bl) + _dot(al, bh))


def _dot2(a, b_exact):
    ah, al = _split(a)
    return _dot(ah, b_exact) + _dot(al, b_exact)


def _dot2_tn(a, b_exact):
    ah, al = _split(a)
    return _dot_tn(ah, b_exact) + _dot_tn(al, b_exact)


def _sigmoid(x):
    return 1.0 / (1.0 + jnp.exp(-x))


def _layer_norm_rows(x):
    mu = jnp.mean(x, axis=-1, keepdims=True)
    xc = x - mu
    var = jnp.mean(xc * xc, axis=-1, keepdims=True)
    return xc * lax.rsqrt(var + LN_EPS)


def _mod_kernel(c_ref, w_ref, b_ref, o_ref):
    c = c_ref[...]
    s = c * _sigmoid(c)
    o_ref[0] = _dot(s, w_ref[0]) + b_ref[0]


def _mod_call(cond, w_ada, b_ada):
    depth = w_ada.shape[0]
    n_out = w_ada.shape[2]
    return pl.pallas_call(
        _mod_kernel,
        out_shape=jax.ShapeDtypeStruct((depth, N_COND, n_out), F32),
        grid=(depth, n_out // TN_MOD),
        in_specs=[
            pl.BlockSpec((N_COND, D_MODEL), lambda l, j: (0, 0)),
            pl.BlockSpec((1, D_MODEL, TN_MOD), lambda l, j: (l, 0, j)),
            pl.BlockSpec((1, 1, TN_MOD), lambda l, j: (l, 0, j)),
        ],
        out_specs=pl.BlockSpec((1, N_COND, TN_MOD), lambda l, j: (l, 0, j)),
        compiler_params=_cparams(("parallel", "parallel")),
        name="mod",
    )(cond, w_ada, b_ada.reshape(depth, 1, n_out))


def _inproj_kernel(x_ref, sh_ref, sc_ref, w_ref, cw_ref, o_ref, h_ref, acc_ref, *, n_ctx_units):
    i = pl.program_id(0)
    j = pl.program_id(1)
    tn = o_ref.shape[1]

    @pl.when(j == 0)
    def _():
        sc = 1.0 + sc_ref[0, 0]
        sh = sh_ref[0, 0]

        def body(rb, carry):
            rows = pl.ds(pl.multiple_of(rb * 128, 128), 128)
            h = _layer_norm_rows(x_ref[rows, :]) * sc + sh
            h_ref[rows, :] = h.astype(BF16)
            return carry

        lax.fori_loop(0, TM // 128, body, 0)
        acc_ref[0:CONV_PAD, :] = jnp.zeros((CONV_PAD, tn), F32)
        acc_ref[CONV_PAD + TM:, :] = jnp.zeros((CONV_PAD, tn), F32)

    acc = jnp.dot(h_ref[...], w_ref[...], preferred_element_type=F32)

    @pl.when(j == NJ_CONV)
    def _():
        o_ref[...] = acc

    @pl.when(j < NJ_CONV)
    def _():
        acc_ref[CONV_PAD:CONV_PAD + TM, :] = acc

    row = lax.broadcasted_iota(jnp.int32, (GRID_W, tn), 0)
    first = row == 0
    last = row == GRID_W - 1

    def tap(a, b):
        return cw_ref[3 * a + b:3 * a + b + 1, :]

    def window(start, mask_first, mask_last):
        uc = acc_ref[start:start + GRID_W, :]
        ul = acc_ref[start - 1:start - 1 + GRID_W, :]
        ur = acc_ref[start + 1:start + 1 + GRID_W, :]
        if mask_first:
            ul = jnp.where(first, 0.0, ul)
        if mask_last:
            ur = jnp.where(last, 0.0, ur)
        return ul, uc, ur

    @pl.when(jnp.logical_and(j < NJ_CONV, i < n_ctx_units))
    def _():
        per_seq = CTX_SEQ // GRID_W
        for rb in range(TM // GRID_W):
            ul, uc, ur = window(CONV_PAD + rb * GRID_W, rb % per_seq == 0, rb % per_seq == per_seq - 1)
            o_ref[rb * GRID_W:(rb + 1) * GRID_W, :] = ul * tap(1, 0) + uc * tap(1, 1) + ur * tap(1, 2)

    @pl.when(jnp.logical_and(j < NJ_CONV, i >= n_ctx_units))
    def _():
        for gi in range(TM // GRID_W):
            y = None
            for a in range(3):
                ul, uc, ur = window(CONV_PAD + (gi + a - 1) * GRID_W, True, True)
                t = ul * tap(a, 0) + uc * tap(a, 1) + ur * tap(a, 2)
                y = t if y is None else y + t
            o_ref[gi * GRID_W:(gi + 1) * GRID_W, :] = y


def _unit_group(i, n_ctx_units):
    return jnp.maximum(i - n_ctx_units + 1, 0)


def _inproj_call(x, mod, w_in, conv_w, n_ctx_units):
    n_units = x.shape[0] // TM
    kern = functools.partial(_inproj_kernel, n_ctx_units=n_ctx_units)
    grp = lambda i: _unit_group(i, n_ctx_units)
    return pl.pallas_call(
        kern,
        out_shape=jax.ShapeDtypeStruct((x.shape[0], D_IN_PAD), F32),
        grid=(n_units, NJ_CONV + 1),
        in_specs=[
            pl.BlockSpec((TM, D_MODEL), lambda i, j: (i, 0)),
            pl.BlockSpec((1, 1, 1, D_MODEL), lambda i, j: (grp(i), 0, 0, 0)),
            pl.BlockSpec((1, 1, 1, D_MODEL), lambda i, j: (grp(i), 1, 0, 0)),
            pl.BlockSpec((D_MODEL, TN_IN), lambda i, j: (0, j)),
            pl.BlockSpec((9, TN_IN), lambda i, j: (0, j)),
        ],
        out_specs=pl.BlockSpec((TM, TN_IN), lambda i, j: (i, j)),
        scratch_shapes=[
            pltpu.VMEM((TM, D_MODEL), BF16),
            pltpu.VMEM((TM + 2 * CONV_PAD, TN_IN), F32),
        ],
        compiler_params=_cparams(("parallel", "arbitrary")),
        name="inproj",
    )(x, mod, mod, w_in, conv_w)


def _stack_heads(z, head0):
    return jnp.concatenate([jnp.where(head0, z, 0.0), jnp.where(head0, 0.0, z)], axis=0)


def _unstack_heads(z):
    c = z.shape[0] // 2
    return z[:c] + z[c:]


def _wkv_kernel(*refs, has_init, has_state_out):
    (r_ref, k_ref, v_ref, lo_ref, wup_ref, w0_ref, aup_ref, a0_ref, kk_ref, ka_ref) = refs[:10]
    pos = 10
    s0_ref = None
    if has_init:
        s0_ref = refs[pos]
        pos += 1
    o_ref = refs[pos]
    pos += 1
    sout_ref = None
    if has_state_out:
        sout_ref = refs[pos]
        pos += 1
    s_ref = refs[pos]

    d = pl.program_id(0)
    c = pl.program_id(2)
    n_chunks = pl.num_programs(2)
    C = CHUNK

    @pl.when(c == 0)
    def _():
        if has_init:
            s_ref[...] = s0_ref[0, 0]
        else:
            s_ref[...] = jnp.zeros(s_ref.shape, F32)

    r = r_ref[...]
    k = k_ref[...]
    v = v_ref[...]
    lo = lo_ref[...]
    xw = lo[:, 0:LORA_W]
    xa = lo[:, LORA_W:LORA_W + LORA_A]
    logw = -DECAY_SCALE * _sigmoid(w0_ref[0] + _dot3(jnp.tanh(xw), wup_ref[0]))
    iclr = _sigmoid(a0_ref[0] + _dot3(xa, aup_ref[0]))
    kd = k * (1.0 + (iclr - 1.0) * ka_ref[...])
    kkr = k * kk_ref[...]

    sgn = 1 - 2 * d
    ti = lax.broadcasted_iota(jnp.int32, (C, C), 0)
    tj = lax.broadcasted_iota(jnp.int32, (C, C), 1)
    cum_mask = jnp.where((ti - tj) * sgn >= 0, 1.0, 0.0).astype(BF16)
    si = lax.broadcasted_iota(jnp.int32, (2 * C, 2 * C), 0)
    sj = lax.broadcasted_iota(jnp.int32, (2 * C, 2 * C), 1)
    same_head = (si >= C) == (sj >= C)
    dlt = (jnp.bitwise_and(si, C - 1) - jnp.bitwise_and(sj, C - 1)) * sgn
    strict = jnp.logical_and(same_head, dlt > 0)
    incl = jnp.logical_and(same_head, dlt >= 0)
    eye = si == sj
    lane = lax.broadcasted_iota(jnp.int32, (C, PAIR_W), 1)
    head0 = lane < HEAD_DIM
    gi = lax.broadcasted_iota(jnp.int32, (PAIR_W, PAIR_W), 0)
    gj = lax.broadcasted_iota(jnp.int32, (PAIR_W, PAIR_W), 1)
    blockdiag = (gi >= HEAD_DIM) == (gj >= HEAD_DIM)
    head_sum = jnp.where(blockdiag, 1.0, 0.0).astype(BF16)
    ones_t = jnp.ones((C, PAIR_W), BF16)
    last_row = jnp.where(d == 0, C - 1, 0)
    row_c = lax.broadcasted_iota(jnp.int32, (C, PAIR_W), 0)

    for p in range(N_PAIRS):
        cols = slice(p * PAIR_W, (p + 1) * PAIR_W)
        rp, vp, kdp, kkrp = r[:, cols], v[:, cols], kd[:, cols], kkr[:, cols]
        lwp, icp = logw[:, cols], iclr[:, cols]
        kkp = kkrp * lax.rsqrt(_dot2(kkrp * kkrp, head_sum) + 1e-12)
        bp = kkp * icp

        lh, ll = _split(lwp)
        cum = _dot(cum_mask, lh) + _dot(cum_mask, ll)
        cum_prev = cum - lwp
        mid = cum[C // 2:C // 2 + 1, :]
        total = jnp.sum(jnp.where(row_c == last_row, cum, 0.0), axis=0, keepdims=True)
        e_in = jnp.exp(cum - mid)
        e_in_prev = jnp.exp(cum_prev - mid)
        e_out = jnp.exp(mid - cum)
        e_tail = jnp.exp(total - cum)
        a1 = kkp * e_in_prev
        r1 = rp * e_in
        b1 = bp * e_out
        k1 = kdp * e_out
        a0 = kkp * jnp.exp(cum_prev)
        r0 = rp * jnp.exp(cum)
        b2 = bp * e_tail
        k2 = kdp * e_tail
        dec_rows = jnp.exp(_dot_tn(lh, ones_t) + _dot_tn(ll, ones_t))

        a1s = _stack_heads(a1, head0)
        r1s = _stack_heads(r1, head0)
        a0s = _stack_heads(a0, head0)
        vs = _stack_heads(vp, head0)
        b1d = jnp.concatenate([b1, b1], axis=0)
        k1d = jnp.concatenate([k1, k1], axis=0)
        l_ab = jnp.where(strict, _dot_nt(a1s, b1d), 0.0)
        l_ak = jnp.where(strict, _dot_nt(a1s, k1d), 0.0)
        m_rb = jnp.where(incl, _dot_nt(r1s, b1d), 0.0)
        m_rk = jnp.where(incl, _dot_nt(r1s, k1d), 0.0)

        x = jnp.where(eye, 1.0, 0.0) - l_ab
        pw = l_ab
        step = 1
        while 2 * step < C:
            pw = _dot(pw, pw)
            x = x + _dot(x, pw)
            step *= 2
        a_hat = _dot(x, a0s)
        y_hat = _dot(x, _dot(l_ak, vs))

        s_t = s_ref[p]
        ys = _dot(a_hat, s_t) + y_hat
        o_p = _dot(r0, s_t) + _unstack_heads(_dot(m_rk, vs) - _dot(m_rb, ys))
        yp = _unstack_heads(ys)
        upd = _dot_tn(jnp.concatenate([b2, k2], axis=0), jnp.concatenate([-yp, vp], axis=0))
        s_new = dec_rows * s_t + jnp.where(blockdiag, upd, 0.0)
        s_ref[p] = s_new
        o_ref[0, :, cols] = o_p
        if has_state_out:
            @pl.when(c == n_chunks - 1)
            def _():
                sout_ref[0, 0, p] = s_new


def _wkv_call(proj, lparams, s0, row0, n_seq, seq_len, has_state_out):
    n_chunks = seq_len // CHUNK
    blk0 = row0 // CHUNK
    has_init = s0 is not None

    def rows(d, s, c):
        return blk0 + s * n_chunks + c + d * (n_chunks - 1 - 2 * c)

    def col(cb):
        return lambda d, s, c: (rows(d, s, c), cb)

    vec = pl.BlockSpec((1, 1, D_RWKV), lambda d, s, c: (d, 0, 0))
    shared = pl.BlockSpec((1, D_RWKV), lambda d, s, c: (0, 0))
    in_specs = [
        pl.BlockSpec((CHUNK, D_RWKV), col(0)),
        pl.BlockSpec((CHUNK, D_RWKV), col(1)),
        pl.BlockSpec((CHUNK, D_RWKV), col(2)),
        pl.BlockSpec((CHUNK, TN_IN), col(LO_BLOCK)),
        pl.BlockSpec((1, LORA_W, D_RWKV), lambda d, s, c: (d, 0, 0)), vec,
        pl.BlockSpec((1, LORA_A, D_RWKV), lambda d, s, c: (d, 0, 0)), vec,
        shared, shared,
    ]
    args = [proj, proj, proj, proj, lparams["w_up"], lparams["w0"], lparams["a_up"], lparams["a0"],
            lparams["k_k"], lparams["k_a"]]
    if has_init:
        in_specs.append(pl.BlockSpec((1, 1, N_PAIRS, PAIR_W, PAIR_W), lambda d, s, c: (s, d, 0, 0, 0)))
        args.append(s0)
    n_rows = n_seq * seq_len
    out_shape = [jax.ShapeDtypeStruct((2, n_rows, D_RWKV), F32)]
    out_specs = [pl.BlockSpec((1, CHUNK, D_RWKV),
                              lambda d, s, c: (d, s * n_chunks + c + d * (n_chunks - 1 - 2 * c), 0))]
    if has_state_out:
        out_shape.append(jax.ShapeDtypeStruct((n_seq, 2, N_PAIRS, PAIR_W, PAIR_W), F32))
        out_specs.append(pl.BlockSpec((1, 1, N_PAIRS, PAIR_W, PAIR_W), lambda d, s, c: (s, d, 0, 0, 0)))
    kern = functools.partial(_wkv_kernel, has_init=has_init, has_state_out=has_state_out)
    return pl.pallas_call(
        kern,
        out_shape=out_shape,
        grid=(2, n_seq, n_chunks),
        in_specs=in_specs,
        out_specs=out_specs,
        scratch_shapes=[pltpu.VMEM((N_PAIRS, PAIR_W, PAIR_W), F32)],
        compiler_params=_cparams(("parallel", "parallel", "arbitrary")),
        name="wkv",
    )(*args)


TM_POST = 256


def _rwkvpost_kernel(o_ref, r_ref, k_ref, v_ref, lo_ref, aup_ref, a0_ref, ka_ref, rk_ref, lng_ref, lnb_ref,
                     gup_ref, out_ref):
    lo = lo_ref[...]
    xa = lo[:, LORA_W:LORA_W + LORA_A]
    sg = _sigmoid(lo[:, LORA_W + LORA_A:LORA_W + LORA_A + LORA_G_PAD])
    gi = lax.broadcasted_iota(jnp.int32, (PAIR_W, PAIR_W), 0)
    gj = lax.broadcasted_iota(jnp.int32, (PAIR_W, PAIR_W), 1)
    head_sum = jnp.where((gi >= HEAD_DIM) == (gj >= HEAD_DIM), 1.0, 0.0).astype(BF16)
    for p in range(N_PAIRS):
        cols = slice(p * PAIR_W, (p + 1) * PAIR_W)
        o = o_ref[0, :, cols] + o_ref[1, :, cols]
        mu = _dot2(o, head_sum) * (1.0 / HEAD_DIM)
        oc = o - mu
        var = _dot2(oc * oc, head_sum) * (1.0 / HEAD_DIM)
        y = oc * lax.rsqrt(var + GN_EPS) * lng_ref[:, cols] + lnb_ref[:, cols]
        k = k_ref[:, cols]
        ic0 = _sigmoid(a0_ref[0, :, cols] + _dot3(xa, aup_ref[0, :, cols]))
        ic1 = _sigmoid(a0_ref[1, :, cols] + _dot3(xa, aup_ref[1, :, cols]))
        ka = ka_ref[:, cols]
        kd_sum = k * (2.0 + (ic0 + ic1 - 2.0) * ka)
        bonus = _dot2(r_ref[:, cols] * kd_sum * rk_ref[:, cols], head_sum) * v_ref[:, cols]
        g = _dot(sg, gup_ref[:, cols])
        out_ref[:, cols] = ((y + bonus) * g).astype(BF16)


def _rwkvpost_call(o, proj, lparams, row0):
    n_rows = o.shape[1]
    blk0 = row0 // TM_POST

    def col(cb):
        return lambda i: (blk0 + i, cb)

    shared = pl.BlockSpec((1, D_RWKV), lambda i: (0, 0))
    both = pl.BlockSpec((2, 1, D_RWKV), lambda i: (0, 0, 0))
    return pl.pallas_call(
        _rwkvpost_kernel,
        out_shape=jax.ShapeDtypeStruct((n_rows, D_RWKV), BF16),
        grid=(n_rows // TM_POST,),
        in_specs=[
            pl.BlockSpec((2, TM_POST, D_RWKV), lambda i: (0, i, 0)),
            pl.BlockSpec((TM_POST, D_RWKV), col(0)),
            pl.BlockSpec((TM_POST, D_RWKV), col(1)),
            pl.BlockSpec((TM_POST, D_RWKV), col(2)),
            pl.BlockSpec((TM_POST, TN_IN), col(LO_BLOCK)),
            pl.BlockSpec((2, LORA_A, D_RWKV), lambda i: (0, 0, 0)), both,
            shared, shared, shared, shared,
            pl.BlockSpec((LORA_G_PAD, D_RWKV), lambda i: (0, 0)),
        ],
        out_specs=pl.BlockSpec((TM_POST, D_RWKV), lambda i: (i, 0)),
        compiler_params=_cparams(("parallel",)),
        name="rwkvpost",
    )(o, proj, proj, proj, proj, lparams["a_up"], lparams["a0"], lparams["k_a"], lparams["r_k"],
      lparams["ln_g"], lparams["ln_b"], lparams["g_up"])


def _dft_mats(L):
    idx = (np.arange(L)[:, None] * np.arange(L)[None, :]) % (2 * L)
    ang = np.pi * idx.astype(np.float64) / L
    alt = np.cos(np.pi * np.arange(L))
    fc = np.cos(ang)
    fs = -np.sin(ang)
    fs[0, :] = alt
    fwd = np.concatenate([fc, fs], axis=0)
    ic = 2.0 * np.cos(ang.T)
    ic[:, 0] = 1.0
    isn = -2.0 * np.sin(ang.T)
    isn[:, 0] = alt
    inv = np.concatenate([ic, isn], axis=1) / (2 * L)
    return fwd.astype(np.float32), inv.astype(np.float32)


def _filter_features(L):
    t = np.linspace(0.0, 1.0, L, dtype=np.float32)[:, None]
    w = (2.0 * math.pi / L) * np.arange(L, dtype=np.float32)[:, None]
    f = np.linspace(1e-4, FILT_BANDS - 1, FILT_BANDS, dtype=np.float32)[None, :]
    z = np.concatenate([t, np.cos(f * w), -np.sin(f * w)], axis=-1)
    zp = np.zeros((L, FILT_EMB_PAD), np.float32)
    zp[:, :FILT_EMB] = z
    return zp, t


def _filter_deltas():
    max_decay = math.log(FILT_TARGET) / FAST_DECAY_PCT
    min_decay = math.log(FILT_TARGET) / SLOW_DECAY_PCT
    return np.abs(np.linspace(min_decay, max_decay, D_HYENA, dtype=np.float32))[None, :]


TC_FILT = 256


def _hyfilt_kernel(z_ref, t_ref, dl_ref, fwd_ref, w1_ref, b1_ref, w2_ref, b2_ref,
                   w3f0_ref, w3b0_ref, w3f1_ref, w3b1_ref, fr_ref, o_ref):
    L = z_ref.shape[0]
    tc = o_ref.shape[3]
    h = jnp.sin(fr_ref[0:1, :] * (_dot3(z_ref[...], w1_ref[...]) + b1_ref[...]))
    h = jnp.sin(fr_ref[1:2, :] * (_dot3(h, w2_ref[...]) + b2_ref[...]))
    decay = jnp.exp(-t_ref[...] * dl_ref[...])
    first = lax.broadcasted_iota(jnp.int32, (L, tc), 0) == 0
    fwd = fwd_ref[...]
    for order, (wf_ref, wb_ref) in enumerate(((w3f0_ref, w3b0_ref), (w3f1_ref, w3b1_ref))):
        hf = _dot3(h, wf_ref[...]) * decay
        hb = _dot3(h, wb_ref[...]) * decay
        norm = jnp.sum(jnp.abs(hf) + jnp.abs(hb), axis=0, keepdims=True)
        hf = hf / norm
        hb = jnp.where(first, 0.0, hb / norm)
        ks = _dot3(fwd, hf + hb)
        kd = _dot3(fwd, hf - hb)
        kr = ks[:L]
        o_ref[order, 0] = kr
        o_ref[order, 1] = jnp.where(first, 0.0, kd[L:])
        o_ref[order, 2] = jnp.where(first, ks[L:L + 1], kr)


def _hyfilt_call(L, fparams):
    z, t = _filter_features(L)
    fwd, _ = _dft_mats(L)
    nct = D_HYENA // TC_FILT
    full = lambda shape: pl.BlockSpec(shape, lambda j: tuple(0 for _ in shape))
    w3_spec = lambda g: pl.BlockSpec((FILT_HIDDEN, TC_FILT), lambda j: (0, g * nct + j))
    w3 = fparams["w3"]
    return pl.pallas_call(
        _hyfilt_kernel,
        out_shape=jax.ShapeDtypeStruct((2, 3, L, D_HYENA), F32),
        grid=(nct,),
        in_specs=[
            full((L, FILT_EMB_PAD)), full((L, 1)), pl.BlockSpec((1, TC_FILT), lambda j: (0, j)), full((2 * L, L)),
            full((FILT_EMB_PAD, FILT_HIDDEN)), full((1, FILT_HIDDEN)),
            full((FILT_HIDDEN, FILT_HIDDEN)), full((1, FILT_HIDDEN)),
            w3_spec(0), w3_spec(1), w3_spec(2), w3_spec(3),
            full((2, FILT_HIDDEN)),
        ],
        out_specs=pl.BlockSpec((2, 3, L, TC_FILT), lambda j: (0, 0, 0, j)),
        compiler_params=_cparams(("parallel",)),
        name="hyfilt",
    )(jnp.asarray(z), jnp.asarray(t), jnp.asarray(_filter_deltas()), jnp.asarray(fwd),
      fparams["w1"], fparams["b1"], fparams["w2"], fparams["b2"], w3, w3, w3, w3, fparams["freq"])


def _hyena_kernel(u_ref, x1_ref, x2_ref, fwd_ref, inv_ref, spec_ref, bias_ref, out_ref):
    L = u_ref.shape[0]
    fwd = fwd_ref[...]
    inv = inv_ref[...]

    def long_conv(u, order):
        spec = _dot(fwd, u)
        ur, ui = spec[:L], spec[L:]
        kr, ki, kr2 = spec_ref[order, 0], spec_ref[order, 1], spec_ref[order, 2]
        yr = ur * kr - ui * ki
        yi = ur * ki + ui * kr2
        y = _dot(inv, jnp.concatenate([yr, yi], axis=0))
        return y + u * bias_ref[order:order + 1, :]

    z = x1_ref[...] * long_conv(u_ref[...], 0)
    out_ref[...] = (x2_ref[...] * long_conv(z, 1)).astype(BF16)


def _hyena_call(proj, spec, bias, row0, n_seq, L, tc):
    fwd, inv = _dft_mats(L)
    nct = D_HYENA // tc
    blk0 = row0 // L
    cb0 = 3 * D_RWKV // tc

    def col(which):
        return lambda s, j: (blk0 + s, cb0 + which * nct + j)

    return pl.pallas_call(
        _hyena_kernel,
        out_shape=jax.ShapeDtypeStruct((n_seq * L, D_HYENA), BF16),
        grid=(n_seq, nct),
        in_specs=[
            pl.BlockSpec((L, tc), col(0)),
            pl.BlockSpec((L, tc), col(1)),
            pl.BlockSpec((L, tc), col(2)),
            pl.BlockSpec((2 * L, L), lambda s, j: (0, 0)),
            pl.BlockSpec((L, 2 * L), lambda s, j: (0, 0)),
            pl.BlockSpec((2, 3, L, tc), lambda s, j: (0, 0, 0, j)),
            pl.BlockSpec((2, tc), lambda s, j: (0, j)),
        ],
        out_specs=pl.BlockSpec((L, tc), lambda s, j: (s, j)),
        compiler_params=_cparams(("parallel", "parallel")),
        name="hyena",
    )(proj, proj, proj, jnp.asarray(fwd).astype(BF16), jnp.asarray(inv).astype(BF16), spec, bias)


def _outproj_kernel(a_ref, b_ref, x_ref, g_ref, w_ref, lg_ref, lb_ref, o_ref):
    mix = (jnp.dot(a_ref[...], w_ref[0:D_RWKV, :], preferred_element_type=F32)
           + jnp.dot(b_ref[...], w_ref[D_RWKV:, :], preferred_element_type=F32))
    y = ALPHA * x_ref[...] + g_ref[0, 0] * mix
    o_ref[...] = _layer_norm_rows(y) * lg_ref[...] + lb_ref[...]


def _tile_group(i, tm, n_ctx_units):
    return _unit_group(i // (TM // tm), n_ctx_units)


def _outproj_call(a_out, b_out, x, mod, w_out, ln_g, ln_b, n_ctx_units):
    grp = lambda i: _tile_group(i, TM_OUT, n_ctx_units)
    return pl.pallas_call(
        _outproj_kernel,
        out_shape=jax.ShapeDtypeStruct(x.shape, F32),
        grid=(x.shape[0] // TM_OUT,),
        in_specs=[
            pl.BlockSpec((TM_OUT, D_RWKV), lambda i: (i, 0)),
            pl.BlockSpec((TM_OUT, D_HYENA), lambda i: (i, 0)),
            pl.BlockSpec((TM_OUT, D_MODEL), lambda i: (i, 0)),
            pl.BlockSpec((1, 1, 1, D_MODEL), lambda i: (grp(i), 2, 0, 0)),
            pl.BlockSpec((D_MODEL, D_MODEL), lambda i: (0, 0)),
            pl.BlockSpec((1, D_MODEL), lambda i: (0, 0)),
            pl.BlockSpec((1, D_MODEL), lambda i: (0, 0)),
        ],
        out_specs=pl.BlockSpec((TM_OUT, D_MODEL), lambda i: (i, 0)),
        compiler_params=_cparams(("parallel",)),
        name="outproj",
    )(a_out, b_out, x, mod, w_out, ln_g, ln_b)


def _mlp_kernel(x_ref, sh_ref, sc_ref, g_ref, w1_ref, w2_ref, lg_ref, lb_ref, o_ref, h_ref, acc_ref):
    f = pl.program_id(1)

    @pl.when(f == 0)
    def _():
        h = _layer_norm_rows(x_ref[...]) * (1.0 + sc_ref[0, 0]) + sh_ref[0, 0]
        h_ref[...] = h.astype(BF16)
        acc_ref[...] = jnp.zeros(acc_ref.shape, F32)

    hid = jnp.maximum(jnp.dot(h_ref[...], w1_ref[...], preferred_element_type=F32), 0.0)
    hid = (hid * hid).astype(BF16)
    acc_ref[...] += jnp.dot(hid, w2_ref[...], preferred_element_type=F32)

    @pl.when(f == pl.num_programs(1) - 1)
    def _():
        y = ALPHA * x_ref[...] + g_ref[0, 0] * acc_ref[...]
        o_ref[...] = _layer_norm_rows(y) * lg_ref[...] + lb_ref[...]


def _mlp_call(x, mod, w1, w2, ln_g, ln_b, n_ctx_units):
    grp = lambda i: _tile_group(i, TM_OUT, n_ctx_units)
    return pl.pallas_call(
        _mlp_kernel,
        out_shape=jax.ShapeDtypeStruct(x.shape, F32),
        grid=(x.shape[0] // TM_OUT, D_FF // TF_MLP),
        in_specs=[
            pl.BlockSpec((TM_OUT, D_MODEL), lambda i, f: (i, 0)),
            pl.BlockSpec((1, 1, 1, D_MODEL), lambda i, f: (grp(i), 3, 0, 0)),
            pl.BlockSpec((1, 1, 1, D_MODEL), lambda i, f: (grp(i), 4, 0, 0)),
            pl.BlockSpec((1, 1, 1, D_MODEL), lambda i, f: (grp(i), 5, 0, 0)),
            pl.BlockSpec((D_MODEL, TF_MLP), lambda i, f: (0, f)),
            pl.BlockSpec((TF_MLP, D_MODEL), lambda i, f: (f, 0)),
            pl.BlockSpec((1, D_MODEL), lambda i, f: (0, 0)),
            pl.BlockSpec((1, D_MODEL), lambda i, f: (0, 0)),
        ],
        out_specs=pl.BlockSpec((TM_OUT, D_MODEL), lambda i, f: (i, 0)),
        scratch_shapes=[pltpu.VMEM((TM_OUT, D_MODEL), BF16), pltpu.VMEM((TM_OUT, D_MODEL), F32)],
        compiler_params=_cparams(("parallel", "arbitrary")),
        name="mlp",
    )(x, mod, mod, mod, w1, w2, ln_g, ln_b)


def _state_to_pairs(s):
    lead = s.shape[:-3]
    st = jnp.swapaxes(s, -1, -2).reshape(lead + (N_PAIRS, 2, HEAD_DIM, HEAD_DIM))
    eye = jnp.eye(2, dtype=s.dtype)
    bd = st[..., :, :, :, None, :] * eye[:, None, :, None]
    return bd.reshape(lead + (N_PAIRS, PAIR_W, PAIR_W))


def _pairs_to_state(sp):
    lead = sp.shape[:-3]
    s5 = sp.reshape(lead + (N_PAIRS, 2, HEAD_DIM, 2, HEAD_DIM))
    diag = jnp.stack([s5[..., 0, :, 0, :], s5[..., 1, :, 1, :]], axis=-3)
    return jnp.swapaxes(diag, -1, -2).reshape(lead + (N_HEADS, HEAD_DIM, HEAD_DIM))


def kernel(x_prompt, x_sample, c, state_rwkv, c_ctx, w_ada, b_ada, w_in, conv_w, lora_w_up, lora_w0, lora_a_up, lora_a0, lora_g_up, rwkv_k_k, rwkv_k_a, rwkv_r_k, rwkv_ln_g, rwkv_ln_b, filt_w1, filt_b1, filt_w2, filt_b2, filt_w3, filt_freq, hyena_bias, w_out, ln1_g, ln1_b, ln2_g, ln2_b, mlp_w1, mlp_w2):
    n_ctx, ctx_seq, _ = x_prompt.shape
    n_lat, lat_seq, _ = x_sample.shape
    depth = w_ada.shape[0]
    assert ctx_seq == CTX_SEQ and lat_seq == LAT_SEQ and (n_ctx * ctx_seq) % TM == 0
    assert 1 + n_lat <= N_COND
    ctx_rows = n_ctx * ctx_seq
    lat_rows = n_lat * lat_seq
    n_ctx_units = ctx_rows // TM

    x = jnp.concatenate([x_prompt.reshape(ctx_rows, D_MODEL), x_sample.reshape(lat_rows, D_MODEL)], axis=0)
    cond = jnp.concatenate([c_ctx[None, :], c, jnp.zeros((N_COND - 1 - n_lat, D_MODEL), F32)], axis=0)
    mod_all = _mod_call(cond, w_ada, b_ada).reshape(depth, N_COND, 6, 1, D_MODEL)

    ctx_states = []
    for l in range(depth):
        mod = mod_all[l]
        w_in_l = jnp.pad(w_in[l].astype(BF16), ((0, 0), (0, D_IN_PAD - D_IN)))
        cw = jnp.pad(conv_w[l].reshape(9, N_CONV), ((0, 0), (0, D_IN_PAD - N_CONV)))
        lparams = {
            "w_up": lora_w_up[l], "w0": lora_w0[l][:, None, :],
            "a_up": lora_a_up[l], "a0": lora_a0[l][:, None, :],
            "k_k": rwkv_k_k[l][None, :], "k_a": rwkv_k_a[l][None, :],
            "r_k": rwkv_r_k[l].reshape(1, D_RWKV),
            "ln_g": rwkv_ln_g[l][None, :], "ln_b": rwkv_ln_b[l][None, :],
            "g_up": jnp.pad(lora_g_up[l], ((0, LORA_G_PAD - LORA_G), (0, 0))),
        }
        fparams = {
            "w1": jnp.pad(filt_w1[l], ((0, FILT_EMB_PAD - FILT_EMB), (0, 0))), "b1": filt_b1[l][None, :],
            "w2": filt_w2[l], "b2": filt_b2[l][None, :], "w3": filt_w3[l], "freq": filt_freq[l],
        }

        proj = _inproj_call(x, mod, w_in_l, cw, n_ctx_units)

        o_ctx, s_ctx = _wkv_call(proj, lparams, None, 0, n_ctx, ctx_seq, True)
        s0_lat = _state_to_pairs(state_rwkv[:, l].astype(F32))
        (o_lat,) = _wkv_call(proj, lparams, s0_lat, ctx_rows, n_lat, lat_seq, False)
        a_out = jnp.concatenate([_rwkvpost_call(o_ctx, proj, lparams, 0),
                                 _rwkvpost_call(o_lat, proj, lparams, ctx_rows)], axis=0)
        ctx_states.append(_pairs_to_state(s_ctx))

        spec_ctx = _hyfilt_call(ctx_seq, fparams)
        spec_lat = _hyfilt_call(lat_seq, fparams)
        b_out = jnp.concatenate([_hyena_call(proj, spec_ctx, hyena_bias[l], 0, n_ctx, ctx_seq, D_HYENA),
                                 _hyena_call(proj, spec_lat, hyena_bias[l], ctx_rows, n_lat, lat_seq, 256)], axis=0)

        x = _outproj_call(a_out, b_out, x, mod, w_out[l].astype(BF16), ln1_g[l][None, :], ln1_b[l][None, :],
                          n_ctx_units)
        x = _mlp_call(x, mod, mlp_w1[l].astype(BF16), mlp_w2[l].astype(BF16), ln2_g[l][None, :], ln2_b[l][None, :],
                      n_ctx_units)

    y_prompt = x[:ctx_rows].reshape(n_ctx, ctx_seq, D_MODEL)
    y_sample = x[ctx_rows:].reshape(n_lat, lat_seq, D_MODEL)
    new_state = jnp.stack(ctx_states, axis=1).astype(x_prompt.dtype)
    return (y_prompt, y_sample, new_state)
```

```python
import functools
import math

import jax
import jax.numpy as jnp
import numpy as np
from jax import lax
from jax.experimental import pallas as pl
from jax.experimental.pallas import tpu as pltpu

F32 = jnp.float32
BF16 = jnp.bfloat16

D_MODEL = 2048
D_RWKV = 1024
D_HYENA = 1024
HEAD_DIM = 64
N_HEADS = D_RWKV // HEAD_DIM
N_PAIRS = N_HEADS // 2
PAIR_W = 2 * HEAD_DIM
LORA_W = 64
LORA_A = 64
LORA_G = 160
LORA_G_PAD = 256
N_CONV = 3 * D_RWKV + 3 * D_HYENA
D_IN = N_CONV + LORA_W + LORA_A + LORA_G
D_FF = 4 * D_MODEL
GRID_W = 64
CTX_SEQ = 256
LAT_SEQ = 1024
FILT_BANDS = 16
FILT_EMB = 1 + 2 * FILT_BANDS
FILT_EMB_PAD = 128
FILT_HIDDEN = 64
DEPTH = 2
ALPHA = (2 * DEPTH) ** 0.25
LN_EPS = 1e-5
GN_EPS = 64e-5
FILT_TARGET = 1e-2
FAST_DECAY_PCT = 0.3
SLOW_DECAY_PCT = 1.5
DECAY_SCALE = math.exp(-0.5)

TM = 1024
TN_IN = 512
NJ_CONV = N_CONV // TN_IN
D_IN_PAD = (NJ_CONV + 1) * TN_IN
LO_BLOCK = N_CONV // TN_IN
CONV_PAD = 72
CHUNK = 64
WKV_GROUP = 4
WKV_ROWS = WKV_GROUP * CHUNK
N_COND = 8
TN_MOD = 1024
TM_OUT = 512
TF_MLP = 512
TM_POST = 256
TC_FILT = 256
VMEM_LIMIT = 56 * 1024 * 1024


def _cparams(sem):
    return pltpu.CompilerParams(dimension_semantics=sem, vmem_limit_bytes=VMEM_LIMIT)


def _dot(a, b):
    return jnp.dot(a.astype(BF16), b.astype(BF16), preferred_element_type=F32)


def _dot_nt(a, b):
    return lax.dot_general(a.astype(BF16), b.astype(BF16), (((1,), (1,)), ((), ())),
                           preferred_element_type=F32)


def _dot_tn(a, b):
    return lax.dot_general(a.astype(BF16), b.astype(BF16), (((0,), (0,)), ((), ())),
                           preferred_element_type=F32)


def _split(x):
    hi = x.astype(BF16)
    lo = (x - hi.astype(F32)).astype(BF16)
    return hi, lo


def _dot3(a, b):
    ah, al = _split(a)
    bh, bl = _split(b)
    return _dot(ah, bh) + (_dot(ah, bl) + _dot(al, bh))


def _dot2(a, b_exact):
    ah, al = _split(a)
    return _dot(ah, b_exact) + _dot(al, b_exact)


def _dot2_left(a_exact, b):
    bh, bl = _split(b)
    return _dot(a_exact, bh) + _dot(a_exact, bl)


def _sigmoid(x):
    return 1.0 / (1.0 + jnp.exp(-x))


def _layer_norm_rows(x):
    mu = jnp.mean(x, axis=-1, keepdims=True)
    xc = x - mu
    var = jnp.mean(xc * xc, axis=-1, keepdims=True)
    return xc * lax.rsqrt(var + LN_EPS)


def _unit_group(i, n_ctx_units):
    return jnp.maximum(i - n_ctx_units + 1, 0)


def _tile_group(i, tm, n_ctx_units):
    return _unit_group(i // (TM // tm), n_ctx_units)


def _mod_spec(l, which, grp):
    return pl.BlockSpec((None, 1, 1, 1, D_MODEL), lambda i, *_: (l, grp(i), which, 0, 0))


def _layer_vec_spec(l, width, n_grid):
    return pl.BlockSpec((None, 1, width), lambda *_: (l, 0, 0))


def _mod_kernel(c_ref, w_ref, b_ref, o_ref):
    c = c_ref[...]
    s = c * _sigmoid(c)
    o_ref[0] = _dot(s, w_ref[0]) + b_ref[0]


def _mod_call(cond, w_ada, b_ada):
    depth = w_ada.shape[0]
    n_out = w_ada.shape[2]
    return pl.pallas_call(
        _mod_kernel,
        out_shape=jax.ShapeDtypeStruct((depth, N_COND, n_out), F32),
        grid=(depth, n_out // TN_MOD),
        in_specs=[
            pl.BlockSpec((N_COND, D_MODEL), lambda l, j: (0, 0)),
            pl.BlockSpec((1, D_MODEL, TN_MOD), lambda l, j: (l, 0, j)),
            pl.BlockSpec((1, 1, TN_MOD), lambda l, j: (l, 0, j)),
        ],
        out_specs=pl.BlockSpec((1, N_COND, TN_MOD), lambda l, j: (l, 0, j)),
        compiler_params=_cparams(("parallel", "parallel")),
        name="mod",
    )(cond, w_ada, b_ada.reshape(depth, 1, n_out))


def _inproj_kernel(x_ref, sh_ref, sc_ref, w_ref, wlo_ref, cw_ref, o_ref, h_ref, acc_ref, *, n_ctx_units):
    i = pl.program_id(0)
    j = pl.program_id(1)
    tn = o_ref.shape[1]

    @pl.when(j == 0)
    def _():
        sc = 1.0 + sc_ref[0, 0]
        sh = sh_ref[0, 0]

        def body(rb, carry):
            rows = pl.ds(pl.multiple_of(rb * 128, 128), 128)
            h = _layer_norm_rows(x_ref[rows, :]) * sc + sh
            h_ref[rows, :] = h.astype(BF16)
            return carry

        lax.fori_loop(0, TM // 128, body, 0)
        acc_ref[0:CONV_PAD, :] = jnp.zeros((CONV_PAD, tn), F32)
        acc_ref[CONV_PAD + TM:, :] = jnp.zeros((CONV_PAD, tn), F32)

    @pl.when(j == NJ_CONV)
    def _():
        o_ref[...] = jnp.dot(h_ref[...], wlo_ref[...], preferred_element_type=F32)

    @pl.when(j < NJ_CONV)
    def _():
        acc_ref[CONV_PAD:CONV_PAD + TM, :] = jnp.dot(h_ref[...], w_ref[...], preferred_element_type=F32)

    row = lax.broadcasted_iota(jnp.int32, (GRID_W, tn), 0)
    first = row == 0
    last = row == GRID_W - 1

    def tap(a, b):
        return cw_ref[3 * a + b:3 * a + b + 1, :]

    def window(start, mask_first, mask_last):
        uc = acc_ref[start:start + GRID_W, :]
        ul = acc_ref[start - 1:start - 1 + GRID_W, :]
        ur = acc_ref[start + 1:start + 1 + GRID_W, :]
        if mask_first:
            ul = jnp.where(first, 0.0, ul)
        if mask_last:
            ur = jnp.where(last, 0.0, ur)
        return ul, uc, ur

    @pl.when(jnp.logical_and(j < NJ_CONV, i < n_ctx_units))
    def _():
        per_seq = CTX_SEQ // GRID_W
        for rb in range(TM // GRID_W):
            ul, uc, ur = window(CONV_PAD + rb * GRID_W, rb % per_seq == 0, rb % per_seq == per_seq - 1)
            o_ref[rb * GRID_W:(rb + 1) * GRID_W, :] = ul * tap(1, 0) + uc * tap(1, 1) + ur * tap(1, 2)

    @pl.when(jnp.logical_and(j < NJ_CONV, i >= n_ctx_units))
    def _():
        for gi in range(TM // GRID_W):
            y = None
            for a in range(3):
                ul, uc, ur = window(CONV_PAD + (gi + a - 1) * GRID_W, True, True)
                t = ul * tap(a, 0) + uc * tap(a, 1) + ur * tap(a, 2)
                y = t if y is None else y + t
            o_ref[gi * GRID_W:(gi + 1) * GRID_W, :] = y


def _inproj_call(x, mod_all, w_in, w_lo, conv_w, l, n_ctx_units):
    n_units = x.shape[0] // TM
    kern = functools.partial(_inproj_kernel, n_ctx_units=n_ctx_units)
    grp = lambda i: _unit_group(i, n_ctx_units)
    conv_col = lambda j: jnp.minimum(j, NJ_CONV - 1)
    return pl.pallas_call(
        kern,
        out_shape=jax.ShapeDtypeStruct((x.shape[0], D_IN_PAD), F32),
        grid=(n_units, NJ_CONV + 1),
        in_specs=[
            pl.BlockSpec((TM, D_MODEL), lambda i, j: (i, 0)),
            _mod_spec(l, 0, grp),
            _mod_spec(l, 1, grp),
            pl.BlockSpec((None, D_MODEL, TN_IN), lambda i, j: (l, 0, conv_col(j))),
            pl.BlockSpec((None, D_MODEL, TN_IN), lambda i, j: (l, 0, 0)),
            pl.BlockSpec((None, 9, TN_IN), lambda i, j: (l, 0, conv_col(j))),
        ],
        out_specs=pl.BlockSpec((TM, TN_IN), lambda i, j: (i, j)),
        scratch_shapes=[
            pltpu.VMEM((TM, D_MODEL), BF16),
            pltpu.VMEM((TM + 2 * CONV_PAD, TN_IN), F32),
        ],
        compiler_params=_cparams(("parallel", "arbitrary")),
        name="inproj",
    )(x, mod_all, mod_all, w_in, w_lo, conv_w)


def _stack_heads(z, head0):
    return jnp.concatenate([jnp.where(head0, z, 0.0), jnp.where(head0, 0.0, z)], axis=0)


def _unstack_heads(z):
    c = z.shape[0] // 2
    return z[:c] + z[c:]


def _wkv_kernel(*refs, has_init, has_state_out):
    (r_ref, k_ref, v_ref, lo_ref, wup_ref, w0_ref, aup_ref, a0_ref, kk_ref, ka_ref) = refs[:10]
    pos = 10
    s0_ref = None
    if has_init:
        s0_ref = refs[pos]
        pos += 1
    o_ref = refs[pos]
    pos += 1
    sout_ref = None
    if has_state_out:
        sout_ref = refs[pos]
        pos += 1
    (s_ref, lw_ref, cum_ref, kkn_ref, b_ref, kd_ref, msk_ref,
     ahat_ref, yhat_ref, mrb_ref, r0_ref, b2s_ref, kv_ref, mv_ref, dec_ref) = refs[pos:]

    d = pl.program_id(0)
    g = pl.program_id(2)
    C = CHUNK
    pairs = range(N_PAIRS)
    cols = [slice(p * PAIR_W, (p + 1) * PAIR_W) for p in pairs]

    @pl.when(g == 0)
    def _():
        if has_init:
            zero = jnp.zeros((HEAD_DIM, HEAD_DIM), F32)
            for p in pairs:
                top = jnp.concatenate([s0_ref[2 * p], zero], axis=1)
                bottom = jnp.concatenate([zero, s0_ref[2 * p + 1]], axis=1)
                s_ref[p] = jnp.concatenate([top, bottom], axis=0)
        else:
            s_ref[...] = jnp.zeros(s_ref.shape, F32)

    sgn = 1 - 2 * d
    si = lax.broadcasted_iota(jnp.int32, (2 * C, 2 * C), 0)
    sj = lax.broadcasted_iota(jnp.int32, (2 * C, 2 * C), 1)
    same_head = (si >= C) == (sj >= C)
    dlt = (jnp.bitwise_and(si, C - 1) - jnp.bitwise_and(sj, C - 1)) * sgn
    msk_ref[0] = jnp.where(jnp.logical_and(same_head, dlt < 0), 1.0, 0.0)
    msk_ref[1] = jnp.where(jnp.logical_and(same_head, dlt >= 0), 1.0, 0.0)
    msk_ref[2] = jnp.where(si == sj, 1.0, 0.0)
    blockdiag = (si >= HEAD_DIM) == (sj >= HEAD_DIM)
    msk_ref[3] = jnp.where(blockdiag, 1.0, 0.0)
    head_sum = jnp.where(blockdiag, 1.0, 0.0).astype(BF16)

    lo = lo_ref[...]
    xw = lo[:, 0:LORA_W]
    xa = lo[:, LORA_W:LORA_W + LORA_A]
    logw = -DECAY_SCALE * _sigmoid(w0_ref[...] + _dot3(jnp.tanh(xw), wup_ref[...]))
    lw_ref[...] = logw
    gi = lax.broadcasted_iota(jnp.int32, (WKV_ROWS, WKV_ROWS), 0)
    gj = lax.broadcasted_iota(jnp.int32, (WKV_ROWS, WKV_ROWS), 1)
    same_chunk = jnp.bitwise_and(gi, -C) == jnp.bitwise_and(gj, -C)
    cum_mask = jnp.where(jnp.logical_and(same_chunk, (gi - gj) * sgn >= 0), 1.0, 0.0).astype(BF16)
    cum_ref[...] = _dot2_left(cum_mask, logw)
    iclr = _sigmoid(a0_ref[...] + _dot3(xa, aup_ref[...]))
    k = k_ref[...]
    kd_ref[...] = k * (1.0 + (iclr - 1.0) * ka_ref[...])
    kkr = k * kk_ref[...]
    for p in pairs:
        slab = kkr[:, cols[p]]
        kkn = slab * lax.rsqrt(_dot(slab * slab, head_sum) + 1e-12)
        kkn_ref[:, cols[p]] = kkn
        b_ref[:, cols[p]] = kkn * iclr[:, cols[p]]

    def chunk_rows(i):
        cc = jnp.where(d == 0, i, WKV_GROUP - 1 - i)
        return pl.ds(pl.multiple_of(cc * C, C), C)

    def phase_a(i, carry):
        rows = chunk_rows(i)
        head0 = lax.broadcasted_iota(jnp.int32, (C, PAIR_W), 1) < HEAD_DIM
        strict_t = msk_ref[0]
        incl = msk_ref[1]
        eye = msk_ref[2]
        bd_f = msk_ref[3]

        lw = [lw_ref[rows, c] for c in cols]
        cum = [cum_ref[rows, c] for c in cols]
        kkn = [kkn_ref[rows, c] for c in cols]
        bb = [b_ref[rows, c] for c in cols]
        kd = [kd_ref[rows, c] for c in cols]
        rr = [r_ref[rows, c] for c in cols]
        vv = [v_ref[rows, c] for c in cols]
        cum_prev = [cm - x for cm, x in zip(cum, lw)]
        mid = [cm[C // 2:C // 2 + 1, :] for cm in cum]
        total = [jnp.where(d == 0, cm[C - 1:C, :], cm[0:1, :]) for cm in cum]
        e_in = [jnp.exp(cm - m) for cm, m in zip(cum, mid)]
        e_in_prev = [jnp.exp(cm - m) for cm, m in zip(cum_prev, mid)]
        e_out = [jnp.exp(m - cm) for cm, m in zip(cum, mid)]
        e_tail = [jnp.exp(t - cm) for cm, t in zip(cum, total)]
        a1s = [_stack_heads(x * e, head0) for x, e in zip(kkn, e_in_prev)]
        r1s = [_stack_heads(x * e, head0) for x, e in zip(rr, e_in)]
        b1 = [x * e for x, e in zip(bb, e_out)]
        k1 = [x * e for x, e in zip(kd, e_out)]
        a0s = [_stack_heads(x * jnp.exp(cm), head0) for x, cm in zip(kkn, cum_prev)]
        vs = [_stack_heads(x, head0) for x in vv]
        g_t = [_dot_nt(jnp.concatenate([b, k], axis=0), a) for a, b, k in zip(a1s, b1, k1)]
        l_ab_t = [jnp.concatenate([g_[:C]] * 2, axis=0) * strict_t for g_ in g_t]
        l_ak_t = [jnp.concatenate([g_[C:]] * 2, axis=0) * strict_t for g_ in g_t]
        g_r = [_dot_nt(a, jnp.concatenate([b, b, k, k], axis=0)) for a, b, k in zip(r1s, b1, k1)]
        m_rb = [g_[:, :PAIR_W] * incl for g_ in g_r]
        m_rk = [g_[:, PAIR_W:] * incl for g_ in g_r]
        lakv_t = [_dot_tn(a, b) for a, b in zip(vs, l_ak_t)]
        mv = [_unstack_heads(_dot(a, b)) for a, b in zip(m_rk, vs)]
        kv = [_dot_tn(a, b * e) * bd_f for a, b, e in zip(vv, kd, e_tail)]
        x = [eye - l_ for l_ in l_ab_t]
        pw = l_ab_t
        step = 1
        while 2 * step < C:
            pw = [_dot(q, q) for q in pw]
            x = [xx + _dot(xx, q) for xx, q in zip(x, pw)]
            step *= 2
        a_hat_t = [_dot_tn(a, xx) for xx, a in zip(x, a0s)]
        y_hat_t = [_dot(a, xx) for xx, a in zip(x, lakv_t)]
        for p in pairs:
            ahat_ref[i, p] = a_hat_t[p].astype(BF16)
            yhat_ref[i, p] = y_hat_t[p]
            mrb_ref[i, p] = m_rb[p].astype(BF16)
            r0_ref[i, p] = (rr[p] * jnp.exp(cum[p])).astype(BF16)
            b2s_ref[i, p] = _stack_heads(bb[p] * e_tail[p], head0).astype(BF16)
            kv_ref[i, p] = kv[p]
            mv_ref[i, p] = mv[p]
            dec_ref[i, p] = jnp.broadcast_to(jnp.exp(total[p]), (8, PAIR_W))
        return carry

    lax.fori_loop(0, WKV_GROUP, phase_a, 0)

    def phase_b(i, carry):
        rows = chunk_rows(i)
        s_old = [s_ref[p] for p in pairs]
        s_bf = [s.astype(BF16) for s in s_old]
        yt = [(_dot(s_bf[p], ahat_ref[i, p]) + yhat_ref[i, p]).astype(BF16) for p in pairs]
        for p in pairs:
            s_ref[p] = dec_ref[i, p, 0:1, :] * s_old[p] + kv_ref[i, p] - _dot(yt[p], b2s_ref[i, p])
        o = [_dot_nt(r0_ref[i, p], s_bf[p]) + mv_ref[i, p] - _unstack_heads(_dot_nt(mrb_ref[i, p], yt[p]))
             for p in pairs]
        for p in pairs:
            o_ref[0, rows, cols[p]] = o[p]
        return carry

    lax.fori_loop(0, WKV_GROUP, phase_b, 0, unroll=True)

    if has_state_out:
        @pl.when(g == pl.num_programs(2) - 1)
        def _():
            for p in pairs:
                s = s_ref[p]
                sout_ref[2 * p] = s[:HEAD_DIM, :HEAD_DIM]
                sout_ref[2 * p + 1] = s[HEAD_DIM:, HEAD_DIM:]


def _wkv_call(proj, lp, l, s0, row0, n_seq, seq_len, has_state_out):
    n_groups = seq_len // WKV_ROWS
    blk0 = row0 // WKV_ROWS
    has_init = s0 is not None

    def group(d, s, g):
        return s * n_groups + g + d * (n_groups - 1 - 2 * g)

    def col(cb):
        return lambda d, s, g: (blk0 + group(d, s, g), cb)

    per_dir_mat = pl.BlockSpec((None, None, LORA_W, D_RWKV), lambda d, s, g: (l, d, 0, 0))
    per_dir_vec = pl.BlockSpec((None, None, 1, D_RWKV), lambda d, s, g: (l, d, 0, 0))
    shared = pl.BlockSpec((None, 1, D_RWKV), lambda d, s, g: (l, 0, 0))
    in_specs = [
        pl.BlockSpec((WKV_ROWS, D_RWKV), col(0)),
        pl.BlockSpec((WKV_ROWS, D_RWKV), col(1)),
        pl.BlockSpec((WKV_ROWS, D_RWKV), col(2)),
        pl.BlockSpec((WKV_ROWS, TN_IN), col(LO_BLOCK)),
        per_dir_mat, per_dir_vec, per_dir_mat, per_dir_vec, shared, shared,
    ]
    args = [proj, proj, proj, proj, lp["w_up"], lp["w0"], lp["a_up"], lp["a0"], lp["k_k"], lp["k_a"]]
    if has_init:
        in_specs.append(pl.BlockSpec((None, None, None, N_HEADS, HEAD_DIM, HEAD_DIM),
                                     lambda d, s, g: (s, l, d, 0, 0, 0)))
        args.append(s0)
    n_rows = n_seq * seq_len
    out_shape = [jax.ShapeDtypeStruct((2, n_rows, D_RWKV), F32)]
    out_specs = [pl.BlockSpec((1, WKV_ROWS, D_RWKV), lambda d, s, g: (d, group(d, s, g), 0))]
    if has_state_out:
        out_shape.append(jax.ShapeDtypeStruct((n_seq, 2, N_HEADS, HEAD_DIM, HEAD_DIM), F32))
        out_specs.append(pl.BlockSpec((None, None, N_HEADS, HEAD_DIM, HEAD_DIM), lambda d, s, g: (s, d, 0, 0, 0)))
    kern = functools.partial(_wkv_kernel, has_init=has_init, has_state_out=has_state_out)
    per_chunk = (WKV_GROUP, N_PAIRS)
    return pl.pallas_call(
        kern,
        out_shape=out_shape,
        grid=(2, n_seq, n_groups),
        in_specs=in_specs,
        out_specs=out_specs,
        scratch_shapes=[
            pltpu.VMEM((N_PAIRS, PAIR_W, PAIR_W), F32),
            pltpu.VMEM((WKV_ROWS, D_RWKV), F32),
            pltpu.VMEM((WKV_ROWS, D_RWKV), F32),
            pltpu.VMEM((WKV_ROWS, D_RWKV), F32),
            pltpu.VMEM((WKV_ROWS, D_RWKV), F32),
            pltpu.VMEM((WKV_ROWS, D_RWKV), F32),
            pltpu.VMEM((4, PAIR_W, PAIR_W), F32),
            pltpu.VMEM(per_chunk + (PAIR_W, PAIR_W), BF16),
            pltpu.VMEM(per_chunk + (PAIR_W, PAIR_W), F32),
            pltpu.VMEM(per_chunk + (PAIR_W, PAIR_W), BF16),
            pltpu.VMEM(per_chunk + (CHUNK, PAIR_W), BF16),
            pltpu.VMEM(per_chunk + (PAIR_W, PAIR_W), BF16),
            pltpu.VMEM(per_chunk + (PAIR_W, PAIR_W), F32),
            pltpu.VMEM(per_chunk + (CHUNK, PAIR_W), F32),
            pltpu.VMEM(per_chunk + (8, PAIR_W), F32),
        ],
        compiler_params=_cparams(("parallel", "parallel", "arbitrary")),
        name="wkv",
    )(*args)


def _rwkvpost_kernel(o_ref, r_ref, k_ref, v_ref, lo_ref, aup_ref, a0_ref, ka_ref, rk_ref, lng_ref, lnb_ref,
                     gup_ref, out_ref):
    lo = lo_ref[...]
    xa = lo[:, LORA_W:LORA_W + LORA_A]
    sg = _sigmoid(lo[:, LORA_W + LORA_A:LORA_W + LORA_A + LORA_G_PAD])
    gi = lax.broadcasted_iota(jnp.int32, (PAIR_W, PAIR_W), 0)
    gj = lax.broadcasted_iota(jnp.int32, (PAIR_W, PAIR_W), 1)
    head_sum = jnp.where((gi >= HEAD_DIM) == (gj >= HEAD_DIM), 1.0, 0.0).astype(BF16)
    for p in range(N_PAIRS):
        cols = slice(p * PAIR_W, (p + 1) * PAIR_W)
        o = o_ref[0, :, cols] + o_ref[1, :, cols]
        mu = _dot2(o, head_sum) * (1.0 / HEAD_DIM)
        oc = o - mu
        var = _dot2(oc * oc, head_sum) * (1.0 / HEAD_DIM)
        y = oc * lax.rsqrt(var + GN_EPS) * lng_ref[:, cols] + lnb_ref[:, cols]
        k = k_ref[:, cols]
        ic0 = _sigmoid(a0_ref[0, :, cols] + _dot3(xa, aup_ref[0, :, cols]))
        ic1 = _sigmoid(a0_ref[1, :, cols] + _dot3(xa, aup_ref[1, :, cols]))
        ka = ka_ref[:, cols]
        kd_sum = k * (2.0 + (ic0 + ic1 - 2.0) * ka)
        bonus = _dot2(r_ref[:, cols] * kd_sum * rk_ref[:, cols], head_sum) * v_ref[:, cols]
        g = _dot(sg, gup_ref[:, cols])
        out_ref[:, cols] = ((y + bonus) * g).astype(BF16)


def _rwkvpost_call(o, proj, lp, l, row0):
    n_rows = o.shape[1]
    blk0 = row0 // TM_POST

    def col(cb):
        return lambda i: (blk0 + i, cb)

    shared = pl.BlockSpec((None, 1, D_RWKV), lambda i: (l, 0, 0))
    return pl.pallas_call(
        _rwkvpost_kernel,
        out_shape=jax.ShapeDtypeStruct((n_rows, D_RWKV), BF16),
        grid=(n_rows // TM_POST,),
        in_specs=[
            pl.BlockSpec((2, TM_POST, D_RWKV), lambda i: (0, i, 0)),
            pl.BlockSpec((TM_POST, D_RWKV), col(0)),
            pl.BlockSpec((TM_POST, D_RWKV), col(1)),
            pl.BlockSpec((TM_POST, D_RWKV), col(2)),
            pl.BlockSpec((TM_POST, TN_IN), col(LO_BLOCK)),
            pl.BlockSpec((None, 2, LORA_A, D_RWKV), lambda i: (l, 0, 0, 0)),
            pl.BlockSpec((None, 2, 1, D_RWKV), lambda i: (l, 0, 0, 0)),
            shared, shared, shared, shared,
            pl.BlockSpec((None, LORA_G_PAD, D_RWKV), lambda i: (l, 0, 0)),
        ],
        out_specs=pl.BlockSpec((TM_POST, D_RWKV), lambda i: (i, 0)),
        compiler_params=_cparams(("parallel",)),
        name="rwkvpost",
    )(o, proj, proj, proj, proj, lp["a_up"], lp["a0"], lp["k_a"], lp["r_k"], lp["ln_g"], lp["ln_b"], lp["g_up"])


def _dft_mats(L):
    idx = (np.arange(L)[:, None] * np.arange(L)[None, :]) % (2 * L)
    ang = np.pi * idx.astype(np.float64) / L
    alt = np.cos(np.pi * np.arange(L))
    fc = np.cos(ang)
    fs = -np.sin(ang)
    fs[0, :] = alt
    fwd = np.concatenate([fc, fs], axis=0)
    ic = 2.0 * np.cos(ang.T)
    ic[:, 0] = 1.0
    isn = -2.0 * np.sin(ang.T)
    isn[:, 0] = alt
    inv = np.concatenate([ic, isn], axis=1) / (2 * L)
    return fwd.astype(np.float32), inv.astype(np.float32)


def _filter_features(L):
    t = np.linspace(0.0, 1.0, L, dtype=np.float32)[:, None]
    w = (2.0 * math.pi / L) * np.arange(L, dtype=np.float32)[:, None]
    f = np.linspace(1e-4, FILT_BANDS - 1, FILT_BANDS, dtype=np.float32)[None, :]
    z = np.concatenate([t, np.cos(f * w), -np.sin(f * w)], axis=-1)
    zp = np.zeros((L, FILT_EMB_PAD), np.float32)
    zp[:, :FILT_EMB] = z
    return zp, t


def _filter_deltas():
    max_decay = math.log(FILT_TARGET) / FAST_DECAY_PCT
    min_decay = math.log(FILT_TARGET) / SLOW_DECAY_PCT
    return np.abs(np.linspace(min_decay, max_decay, D_HYENA, dtype=np.float32))[None, :]


def _hyfilt_kernel(z_ref, t_ref, dl_ref, fwd_ref, w1_ref, b1_ref, w2_ref, b2_ref,
                   w3f0_ref, w3b0_ref, w3f1_ref, w3b1_ref, fr_ref, o_ref):
    L = z_ref.shape[0]
    tc = o_ref.shape[3]
    h = jnp.sin(fr_ref[0:1, :] * (_dot3(z_ref[...], w1_ref[...]) + b1_ref[...]))
    h = jnp.sin(fr_ref[1:2, :] * (_dot3(h, w2_ref[...]) + b2_ref[...]))
    decay = jnp.exp(-t_ref[...] * dl_ref[...])
    first = lax.broadcasted_iota(jnp.int32, (L, tc), 0) == 0
    fwd = fwd_ref[...]
    for order, (wf_ref, wb_ref) in enumerate(((w3f0_ref, w3b0_ref), (w3f1_ref, w3b1_ref))):
        hf = _dot3(h, wf_ref[...]) * decay
        hb = _dot3(h, wb_ref[...]) * decay
        norm = jnp.sum(jnp.abs(hf) + jnp.abs(hb), axis=0, keepdims=True)
        hf = hf / norm
        hb = jnp.where(first, 0.0, hb / norm)
        ks = _dot3(fwd, hf + hb)
        kd = _dot3(fwd, hf - hb)
        kr = ks[:L]
        o_ref[order, 0] = kr
        o_ref[order, 1] = jnp.where(first, 0.0, kd[L:])
        o_ref[order, 2] = jnp.where(first, ks[L:L + 1], kr)


def _hyfilt_call(L, fp, l):
    z, t = _filter_features(L)
    fwd, _ = _dft_mats(L)
    nct = D_HYENA // TC_FILT
    full = lambda shape: pl.BlockSpec(shape, lambda j: tuple(0 for _ in shape))
    layer = lambda shape: pl.BlockSpec((None,) + shape, lambda j: (l,) + tuple(0 for _ in shape))
    w3_spec = lambda grp: pl.BlockSpec((None, FILT_HIDDEN, TC_FILT), lambda j: (l, 0, grp * nct + j))
    w3 = fp["w3"]
    return pl.pallas_call(
        _hyfilt_kernel,
        out_shape=jax.ShapeDtypeStruct((2, 3, L, D_HYENA), F32),
        grid=(nct,),
        in_specs=[
            full((L, FILT_EMB_PAD)), full((L, 1)), pl.BlockSpec((1, TC_FILT), lambda j: (0, j)), full((2 * L, L)),
            layer((FILT_EMB_PAD, FILT_HIDDEN)), layer((1, FILT_HIDDEN)),
            layer((FILT_HIDDEN, FILT_HIDDEN)), layer((1, FILT_HIDDEN)),
            w3_spec(0), w3_spec(1), w3_spec(2), w3_spec(3),
            layer((2, FILT_HIDDEN)),
        ],
        out_specs=pl.BlockSpec((2, 3, L, TC_FILT), lambda j: (0, 0, 0, j)),
        compiler_params=_cparams(("parallel",)),
        name="hyfilt",
    )(jnp.asarray(z), jnp.asarray(t), jnp.asarray(_filter_deltas()), jnp.asarray(fwd),
      fp["w1"], fp["b1"], fp["w2"], fp["b2"], w3, w3, w3, w3, fp["freq"])


def _hyena_kernel(u_ref, x1_ref, x2_ref, fwd_ref, inv_ref, spec_ref, bias_ref, out_ref):
    L = u_ref.shape[0]
    fwd = fwd_ref[...]
    inv = inv_ref[...]

    def long_conv(u, order):
        spec = _dot(fwd, u)
        ur, ui = spec[:L], spec[L:]
        kr, ki, kr2 = spec_ref[order, 0], spec_ref[order, 1], spec_ref[order, 2]
        yr = ur * kr - ui * ki
        yi = ur * ki + ui * kr2
        y = _dot(inv, jnp.concatenate([yr, yi], axis=0))
        return y + u * bias_ref[order:order + 1, :]

    z = x1_ref[...] * long_conv(u_ref[...], 0)
    out_ref[...] = (x2_ref[...] * long_conv(z, 1)).astype(BF16)


def _hyena_call(proj, spec, bias, l, row0, n_seq, L, tc):
    fwd, inv = _dft_mats(L)
    nct = D_HYENA // tc
    blk0 = row0 // L
    cb0 = 3 * D_RWKV // tc

    def col(which):
        return lambda s, j: (blk0 + s, cb0 + which * nct + j)

    return pl.pallas_call(
        _hyena_kernel,
        out_shape=jax.ShapeDtypeStruct((n_seq * L, D_HYENA), BF16),
        grid=(n_seq, nct),
        in_specs=[
            pl.BlockSpec((L, tc), col(0)),
            pl.BlockSpec((L, tc), col(1)),
            pl.BlockSpec((L, tc), col(2)),
            pl.BlockSpec((2 * L, L), lambda s, j: (0, 0)),
            pl.BlockSpec((L, 2 * L), lambda s, j: (0, 0)),
            pl.BlockSpec((2, 3, L, tc), lambda s, j: (0, 0, 0, j)),
            pl.BlockSpec((None, 2, tc), lambda s, j: (l, 0, j)),
        ],
        out_specs=pl.BlockSpec((L, tc), lambda s, j: (s, j)),
        compiler_params=_cparams(("parallel", "parallel")),
        name="hyena",
    )(proj, proj, proj, jnp.asarray(fwd).astype(BF16), jnp.asarray(inv).astype(BF16), spec, bias)


def _outproj_kernel(a_ref, b_ref, x_ref, g_ref, w_ref, lg_ref, lb_ref, o_ref):
    mix = (jnp.dot(a_ref[...], w_ref[0:D_RWKV, :], preferred_element_type=F32)
           + jnp.dot(b_ref[...], w_ref[D_RWKV:, :], preferred_element_type=F32))
    y = ALPHA * x_ref[...] + g_ref[0, 0] * mix
    o_ref[...] = _layer_norm_rows(y) * lg_ref[...] + lb_ref[...]


def _outproj_call(a_out, b_out, x, mod_all, w_out, ln_g, ln_b, l, n_ctx_units):
    grp = lambda i: _tile_group(i, TM_OUT, n_ctx_units)
    vec = pl.BlockSpec((None, 1, D_MODEL), lambda i: (l, 0, 0))
    return pl.pallas_call(
        _outproj_kernel,
        out_shape=jax.ShapeDtypeStruct(x.shape, F32),
        grid=(x.shape[0] // TM_OUT,),
        in_specs=[
            pl.BlockSpec((TM_OUT, D_RWKV), lambda i: (i, 0)),
            pl.BlockSpec((TM_OUT, D_HYENA), lambda i: (i, 0)),
            pl.BlockSpec((TM_OUT, D_MODEL), lambda i: (i, 0)),
            _mod_spec(l, 2, grp),
            pl.BlockSpec((None, D_MODEL, D_MODEL), lambda i: (l, 0, 0)),
            vec, vec,
        ],
        out_specs=pl.BlockSpec((TM_OUT, D_MODEL), lambda i: (i, 0)),
        compiler_params=_cparams(("parallel",)),
        name="outproj",
    )(a_out, b_out, x, mod_all, w_out, ln_g, ln_b)


def _mlp_kernel(x_ref, sh_ref, sc_ref, g_ref, w1_ref, w2_ref, lg_ref, lb_ref, o_ref, h_ref, acc_ref):
    f = pl.program_id(1)

    @pl.when(f == 0)
    def _():
        h = _layer_norm_rows(x_ref[...]) * (1.0 + sc_ref[0, 0]) + sh_ref[0, 0]
        h_ref[...] = h.astype(BF16)
        acc_ref[...] = jnp.zeros(acc_ref.shape, F32)

    hid = jnp.maximum(jnp.dot(h_ref[...], w1_ref[...], preferred_element_type=F32), 0.0)
    hid = (hid * hid).astype(BF16)
    acc_ref[...] += jnp.dot(hid, w2_ref[...], preferred_element_type=F32)

    @pl.when(f == pl.num_programs(1) - 1)
    def _():
        y = ALPHA * x_ref[...] + g_ref[0, 0] * acc_ref[...]
        o_ref[...] = _layer_norm_rows(y) * lg_ref[...] + lb_ref[...]


def _mlp_call(x, mod_all, w1, w2, ln_g, ln_b, l, n_ctx_units):
    grp = lambda i: _tile_group(i, TM_OUT, n_ctx_units)
    vec = pl.BlockSpec((None, 1, D_MODEL), lambda i, f: (l, 0, 0))
    return pl.pallas_call(
        _mlp_kernel,
        out_shape=jax.ShapeDtypeStruct(x.shape, F32),
        grid=(x.shape[0] // TM_OUT, D_FF // TF_MLP),
        in_specs=[
            pl.BlockSpec((TM_OUT, D_MODEL), lambda i, f: (i, 0)),
            _mod_spec(l, 3, grp),
            _mod_spec(l, 4, grp),
            _mod_spec(l, 5, grp),
            pl.BlockSpec((None, D_MODEL, TF_MLP), lambda i, f: (l, 0, f)),
            pl.BlockSpec((None, TF_MLP, D_MODEL), lambda i, f: (l, f, 0)),
            vec, vec,
        ],
        out_specs=pl.BlockSpec((TM_OUT, D_MODEL), lambda i, f: (i, 0)),
        scratch_shapes=[pltpu.VMEM((TM_OUT, D_MODEL), BF16), pltpu.VMEM((TM_OUT, D_MODEL), F32)],
        compiler_params=_cparams(("parallel", "arbitrary")),
        name="mlp",
    )(x, mod_all, mod_all, mod_all, w1, w2, ln_g, ln_b)


def kernel(x_prompt, x_sample, c, state_rwkv, c_ctx, w_ada, b_ada, w_in, conv_w, lora_w_up, lora_w0, lora_a_up, lora_a0, lora_g_up, rwkv_k_k, rwkv_k_a, rwkv_r_k, rwkv_ln_g, rwkv_ln_b, filt_w1, filt_b1, filt_w2, filt_b2, filt_w3, filt_freq, hyena_bias, w_out, ln1_g, ln1_b, ln2_g, ln2_b, mlp_w1, mlp_w2):
    n_ctx, ctx_seq, _ = x_prompt.shape
    n_lat, lat_seq, _ = x_sample.shape
    depth = w_ada.shape[0]
    assert ctx_seq == CTX_SEQ and lat_seq == LAT_SEQ and (n_ctx * ctx_seq) % TM == 0
    assert 1 + n_lat <= N_COND
    ctx_rows = n_ctx * ctx_seq
    lat_rows = n_lat * lat_seq
    n_ctx_units = ctx_rows // TM

    x = jnp.concatenate([x_prompt.reshape(ctx_rows, D_MODEL), x_sample.reshape(lat_rows, D_MODEL)], axis=0)
    cond = jnp.concatenate([c_ctx[None, :], c, jnp.zeros((N_COND - 1 - n_lat, D_MODEL), F32)], axis=0)
    mod_all = _mod_call(cond, w_ada, b_ada).reshape(depth, N_COND, 6, 1, D_MODEL)

    w_in_bf = w_in.astype(BF16)
    w_lo_bf = jnp.pad(w_in[:, :, N_CONV:].astype(BF16), ((0, 0), (0, 0), (0, TN_IN - (D_IN - N_CONV))))
    conv_w9 = conv_w.reshape(depth, 9, N_CONV)
    w_out_bf = w_out.astype(BF16)
    w1_bf = mlp_w1.astype(BF16)
    w2_bf = mlp_w2.astype(BF16)
    row = lambda a: a.reshape(depth, 1, a.shape[-1])
    lp = {
        "w_up": lora_w_up, "w0": lora_w0[:, :, None, :], "a_up": lora_a_up, "a0": lora_a0[:, :, None, :],
        "k_k": row(rwkv_k_k), "k_a": row(rwkv_k_a), "r_k": rwkv_r_k.reshape(depth, 1, D_RWKV),
        "ln_g": row(rwkv_ln_g), "ln_b": row(rwkv_ln_b),
        "g_up": jnp.pad(lora_g_up, ((0, 0), (0, LORA_G_PAD - LORA_G), (0, 0))),
    }
    fp = {
        "w1": jnp.pad(filt_w1, ((0, 0), (0, FILT_EMB_PAD - FILT_EMB), (0, 0))), "b1": row(filt_b1),
        "w2": filt_w2, "b2": row(filt_b2), "w3": filt_w3, "freq": filt_freq,
    }
    ln1_g, ln1_b, ln2_g, ln2_b = row(ln1_g), row(ln1_b), row(ln2_g), row(ln2_b)
    state_in = state_rwkv.astype(F32)

    ctx_states = []
    for l in range(depth):
        proj = _inproj_call(x, mod_all, w_in_bf, w_lo_bf, conv_w9, l, n_ctx_units)

        o_ctx, s_ctx = _wkv_call(proj, lp, l, None, 0, n_ctx, ctx_seq, True)
        (o_lat,) = _wkv_call(proj, lp, l, state_in, ctx_rows, n_lat, lat_seq, False)
        a_out = jnp.concatenate([_rwkvpost_call(o_ctx, proj, lp, l, 0),
                                 _rwkvpost_call(o_lat, proj, lp, l, ctx_rows)], axis=0)
        ctx_states.append(s_ctx)

        spec_ctx = _hyfilt_call(ctx_seq, fp, l)
        spec_lat = _hyfilt_call(lat_seq, fp, l)
        b_out = jnp.concatenate([_hyena_call(proj, spec_ctx, hyena_bias, l, 0, n_ctx, ctx_seq, D_HYENA),
                                 _hyena_call(proj, spec_lat, hyena_bias, l, ctx_rows, n_lat, lat_seq, 256)], axis=0)

        x = _outproj_call(a_out, b_out, x, mod_all, w_out_bf, ln1_g, ln1_b, l, n_ctx_units)
        x = _mlp_call(x, mod_all, w1_bf, w2_bf, ln2_g, ln2_b, l, n_ctx_units)

    y_prompt = x[:ctx_rows].reshape(n_ctx, ctx_seq, D_MODEL)
    y_sample = x[ctx_rows:].reshape(n_lat, lat_seq, D_MODEL)
    new_state = jnp.stack(ctx_states, axis=1).astype(x_prompt.dtype)
    return (y_prompt, y_sample, new_state)
```

```python
import functools
import math

import jax
import jax.numpy as jnp
import numpy as np
from jax import lax
from jax.experimental import pallas as pl
from jax.experimental.pallas import tpu as pltpu

F32 = jnp.float32
BF16 = jnp.bfloat16

D_MODEL = 2048
D_RWKV = 1024
D_HYENA = 1024
HEAD_DIM = 64
N_HEADS = D_RWKV // HEAD_DIM
N_PAIRS = N_HEADS // 2
PAIR_W = 2 * HEAD_DIM
LORA_W = 64
LORA_A = 64
LORA_G = 160
LORA_G_PAD = 256
N_CONV = 3 * D_RWKV + 3 * D_HYENA
D_IN = N_CONV + LORA_W + LORA_A + LORA_G
D_FF = 4 * D_MODEL
GRID_W = 64
CTX_SEQ = 256
LAT_SEQ = 1024
FILT_BANDS = 16
FILT_EMB = 1 + 2 * FILT_BANDS
FILT_EMB_PAD = 128
FILT_HIDDEN = 64
DEPTH = 2
ALPHA = (2 * DEPTH) ** 0.25
LN_EPS = 1e-5
GN_EPS = 64e-5
FILT_TARGET = 1e-2
FAST_DECAY_PCT = 0.3
SLOW_DECAY_PCT = 1.5
DECAY_SCALE = math.exp(-0.5)

TM = 1024
TN_IN = 512
NJ_CONV = N_CONV // TN_IN
D_IN_PAD = (NJ_CONV + 1) * TN_IN
LO_BLOCK = N_CONV // TN_IN
CONV_PAD = 72
CHUNK = 64
WKV_GROUP = 4
WKV_ROWS = WKV_GROUP * CHUNK
N_COND = 8
TN_MOD = 1024
TM_OUT = 512
TM_MLP = 1024
TF_MLP = 512
TM_POST = 256
TC_FILT = 256
VMEM_LIMIT = 56 * 1024 * 1024


def _cparams(sem):
    return pltpu.CompilerParams(dimension_semantics=sem, vmem_limit_bytes=VMEM_LIMIT)


def _dot(a, b):
    return jnp.dot(a.astype(BF16), b.astype(BF16), preferred_element_type=F32)


def _dot_nt(a, b):
    return lax.dot_general(a.astype(BF16), b.astype(BF16), (((1,), (1,)), ((), ())),
                           preferred_element_type=F32)


def _dot_tn(a, b):
    return lax.dot_general(a.astype(BF16), b.astype(BF16), (((0,), (0,)), ((), ())),
                           preferred_element_type=F32)


def _split(x):
    hi = x.astype(BF16)
    lo = (x - hi.astype(F32)).astype(BF16)
    return hi, lo


def _dot3(a, b):
    ah, al = _split(a)
    bh, bl = _split(b)
    return _dot(ah, bh) + (_dot(ah, bl) + _dot(al, bh))


def _dot2(a, b_exact):
    ah, al = _split(a)
    return _dot(ah, b_exact) + _dot(al, b_exact)


def _dot2_left(a_exact, b):
    bh, bl = _split(b)
    return _dot(a_exact, bh) + _dot(a_exact, bl)


def _sigmoid(x):
    return 1.0 / (1.0 + jnp.exp(-x))


def _layer_norm_rows(x):
    mu = jnp.mean(x, axis=-1, keepdims=True)
    xc = x - mu
    var = jnp.mean(xc * xc, axis=-1, keepdims=True)
    return xc * lax.rsqrt(var + LN_EPS)


def _unit_group(i, n_ctx_units):
    return jnp.maximum(i - n_ctx_units + 1, 0)


def _tile_group(i, tm, n_ctx_units):
    return _unit_group(i // (TM // tm), n_ctx_units)


def _mod_spec(l, which, grp):
    return pl.BlockSpec((None, 1, 1, 1, D_MODEL), lambda i, *_: (l, grp(i), which, 0, 0))


def _mod_kernel(c_ref, w_ref, b_ref, o_ref):
    c = c_ref[...]
    s = c * _sigmoid(c)
    o_ref[0] = _dot(s, w_ref[0]) + b_ref[0]


def _mod_call(cond, w_ada, b_ada):
    depth = w_ada.shape[0]
    n_out = w_ada.shape[2]
    return pl.pallas_call(
        _mod_kernel,
        out_shape=jax.ShapeDtypeStruct((depth, N_COND, n_out), F32),
        grid=(depth, n_out // TN_MOD),
        in_specs=[
            pl.BlockSpec((N_COND, D_MODEL), lambda l, j: (0, 0)),
            pl.BlockSpec((1, D_MODEL, TN_MOD), lambda l, j: (l, 0, j)),
            pl.BlockSpec((1, 1, TN_MOD), lambda l, j: (l, 0, j)),
        ],
        out_specs=pl.BlockSpec((1, N_COND, TN_MOD), lambda l, j: (l, 0, j)),
        compiler_params=_cparams(("parallel", "parallel")),
        name="mod",
    )(cond, w_ada, b_ada.reshape(depth, 1, n_out))


def _inproj_kernel(x_ref, sh_ref, sc_ref, w_ref, wlo_ref, cw_ref, o_ref, h_ref, acc_a, acc_b, *, n_ctx_units):
    i = pl.program_id(0)
    j = pl.program_id(1)
    tn = o_ref.shape[1]
    is_grid = i >= n_ctx_units
    mid = slice(CONV_PAD, CONV_PAD + TM)

    def matmul_into(dst_ref):
        dst_ref[mid, :] = jnp.dot(h_ref[...], w_ref[...], preferred_element_type=F32)

    def conv_from(src_ref):
        row = lax.broadcasted_iota(jnp.int32, (GRID_W, tn), 0)
        first = row == 0
        last = row == GRID_W - 1
        first_grid = jnp.logical_and(first, is_grid)
        last_grid = jnp.logical_and(last, is_grid)
        tap_row = lax.broadcasted_iota(jnp.int32, (9, tn), 0)
        centre = jnp.logical_and(tap_row >= 3, tap_row < 6)
        cw = jnp.where(jnp.logical_or(centre, is_grid), cw_ref[...], 0.0)
        per_seq = CTX_SEQ // GRID_W
        for gi in range(TM // GRID_W):
            y = None
            for a in range(3):
                start = CONV_PAD + (gi + a - 1) * GRID_W
                uc = src_ref[start:start + GRID_W, :]
                ul = src_ref[start - 1:start - 1 + GRID_W, :]
                ur = src_ref[start + 1:start + 1 + GRID_W, :]
                seq_first = a == 1 and gi % per_seq == 0
                seq_last = a == 1 and gi % per_seq == per_seq - 1
                ul = jnp.where(first if (a != 1 or seq_first) else first_grid, 0.0, ul)
                ur = jnp.where(last if (a != 1 or seq_last) else last_grid, 0.0, ur)
                t = (ul * cw[3 * a:3 * a + 1, :] + uc * cw[3 * a + 1:3 * a + 2, :]
                     + ur * cw[3 * a + 2:3 * a + 3, :])
                y = t if y is None else y + t
            o_ref[gi * GRID_W:(gi + 1) * GRID_W, :] = y

    @pl.when(j == 0)
    def _():
        sc = 1.0 + sc_ref[0, 0]
        sh = sh_ref[0, 0]

        def body(rb, carry):
            rows = pl.ds(pl.multiple_of(rb * 128, 128), 128)
            h = _layer_norm_rows(x_ref[rows, :]) * sc + sh
            h_ref[rows, :] = h.astype(BF16)
            return carry

        lax.fori_loop(0, TM // 128, body, 0)
        for acc in (acc_a, acc_b):
            acc[0:CONV_PAD, :] = jnp.zeros((CONV_PAD, tn), F32)
            acc[CONV_PAD + TM:, :] = jnp.zeros((CONV_PAD, tn), F32)
        o_ref[...] = jnp.dot(h_ref[...], wlo_ref[...], preferred_element_type=F32)

    @pl.when(j == 1)
    def _():
        matmul_into(acc_a)

    main = jnp.logical_and(j >= 2, j <= NJ_CONV)
    odd = jnp.bitwise_and(j, 1) == 1

    @pl.when(jnp.logical_and(main, jnp.logical_not(odd)))
    def _():
        conv_from(acc_a)
        matmul_into(acc_b)

    @pl.when(jnp.logical_and(main, odd))
    def _():
        conv_from(acc_b)
        matmul_into(acc_a)

    @pl.when(j == NJ_CONV + 1)
    def _():
        conv_from(acc_b if NJ_CONV % 2 == 0 else acc_a)


def _inproj_call(x, mod_all, w_in, w_lo, conv_w, l, n_ctx_units):
    n_units = x.shape[0] // TM
    kern = functools.partial(_inproj_kernel, n_ctx_units=n_ctx_units)
    grp = lambda i: _unit_group(i, n_ctx_units)
    tile = lambda t: jnp.clip(t, 0, NJ_CONV - 1)
    return pl.pallas_call(
        kern,
        out_shape=jax.ShapeDtypeStruct((x.shape[0], D_IN_PAD), F32),
        grid=(n_units, NJ_CONV + 2),
        in_specs=[
            pl.BlockSpec((TM, D_MODEL), lambda i, j: (i, 0)),
            _mod_spec(l, 0, grp),
            _mod_spec(l, 1, grp),
            pl.BlockSpec((None, D_MODEL, TN_IN), lambda i, j: (l, 0, tile(j - 1))),
            pl.BlockSpec((None, D_MODEL, TN_IN), lambda i, j: (l, 0, 0)),
            pl.BlockSpec((None, 9, TN_IN), lambda i, j: (l, 0, tile(j - 2))),
        ],
        out_specs=pl.BlockSpec((TM, TN_IN), lambda i, j: (i, jnp.where(j == 0, LO_BLOCK, tile(j - 2)))),
        scratch_shapes=[
            pltpu.VMEM((TM, D_MODEL), BF16),
            pltpu.VMEM((TM + 2 * CONV_PAD, TN_IN), F32),
            pltpu.VMEM((TM + 2 * CONV_PAD, TN_IN), F32),
        ],
        compiler_params=_cparams(("parallel", "arbitrary")),
        name="inproj",
    )(x, mod_all, mod_all, w_in, w_lo, conv_w)


def _stack_heads(z, head0):
    return jnp.concatenate([jnp.where(head0, z, 0.0), jnp.where(head0, 0.0, z)], axis=0)


def _unstack_heads(z):
    c = z.shape[0] // 2
    return z[:c] + z[c:]


def _wkv_kernel(*refs, has_init, has_state_out, has_alias):
    (r_ref, k_ref, v_ref, lo_ref, wup_ref, w0_ref, aup_ref, a0_ref, kk_ref, ka_ref) = refs[:10]
    pos = 10
    s0_ref = None
    if has_init:
        s0_ref = refs[pos]
        pos += 1
    if has_alias:
        pos += 1
    o_ref = refs[pos]
    pos += 1
    sout_ref = None
    if has_state_out:
        sout_ref = refs[pos]
        pos += 1
    (s_ref, lw_ref, cum_ref, kkn_ref, b_ref, kd_ref, msk_ref,
     ahat_ref, yhat_ref, mrb_ref, r0_ref, b2s_ref, kv_ref, mv_ref, dec_ref) = refs[pos:]

    d = pl.program_id(0)
    g = pl.program_id(2)
    C = CHUNK
    pairs = range(N_PAIRS)
    cols = [slice(p * PAIR_W, (p + 1) * PAIR_W) for p in pairs]

    @pl.when(g == 0)
    def _():
        if has_init:
            zero = jnp.zeros((HEAD_DIM, HEAD_DIM), F32)
            for p in pairs:
                top = jnp.concatenate([s0_ref[2 * p], zero], axis=1)
                bottom = jnp.concatenate([zero, s0_ref[2 * p + 1]], axis=1)
                s_ref[p] = jnp.concatenate([top, bottom], axis=0)
        else:
            s_ref[...] = jnp.zeros(s_ref.shape, F32)

    sgn = 1 - 2 * d
    si = lax.broadcasted_iota(jnp.int32, (2 * C, 2 * C), 0)
    sj = lax.broadcasted_iota(jnp.int32, (2 * C, 2 * C), 1)
    same_head = (si >= C) == (sj >= C)
    dlt = (jnp.bitwise_and(si, C - 1) - jnp.bitwise_and(sj, C - 1)) * sgn
    msk_ref[0] = jnp.where(jnp.logical_and(same_head, dlt < 0), 1.0, 0.0)
    msk_ref[1] = jnp.where(jnp.logical_and(same_head, dlt >= 0), 1.0, 0.0)
    msk_ref[2] = jnp.where(si == sj, 1.0, 0.0)
    blockdiag = (si >= HEAD_DIM) == (sj >= HEAD_DIM)
    msk_ref[3] = jnp.where(blockdiag, 1.0, 0.0)
    head_sum = jnp.where(blockdiag, 1.0, 0.0).astype(BF16)

    lo = lo_ref[...]
    xw = lo[:, 0:LORA_W]
    xa = lo[:, LORA_W:LORA_W + LORA_A]
    logw = -DECAY_SCALE * _sigmoid(w0_ref[...] + _dot3(jnp.tanh(xw), wup_ref[...]))
    lw_ref[...] = logw
    gi = lax.broadcasted_iota(jnp.int32, (WKV_ROWS, WKV_ROWS), 0)
    gj = lax.broadcasted_iota(jnp.int32, (WKV_ROWS, WKV_ROWS), 1)
    same_chunk = jnp.bitwise_and(gi, -C) == jnp.bitwise_and(gj, -C)
    cum_mask = jnp.where(jnp.logical_and(same_chunk, (gi - gj) * sgn >= 0), 1.0, 0.0).astype(BF16)
    cum_ref[...] = _dot2_left(cum_mask, logw)
    iclr = _sigmoid(a0_ref[...] + _dot3(xa, aup_ref[...]))
    k = k_ref[...]
    kd_ref[...] = k * (1.0 + (iclr - 1.0) * ka_ref[...])
    kkr = k * kk_ref[...]
    for p in pairs:
        slab = kkr[:, cols[p]]
        kkn = slab * lax.rsqrt(_dot(slab * slab, head_sum) + 1e-12)
        kkn_ref[:, cols[p]] = kkn
        b_ref[:, cols[p]] = kkn * iclr[:, cols[p]]

    def chunk_rows(i):
        cc = jnp.where(d == 0, i, WKV_GROUP - 1 - i)
        return pl.ds(pl.multiple_of(cc * C, C), C)

    def phase_a(i, carry):
        rows = chunk_rows(i)
        head0 = lax.broadcasted_iota(jnp.int32, (C, PAIR_W), 1) < HEAD_DIM
        strict_t = msk_ref[0]
        incl = msk_ref[1]
        eye = msk_ref[2]
        bd_f = msk_ref[3]

        lw = [lw_ref[rows, c] for c in cols]
        cum = [cum_ref[rows, c] for c in cols]
        kkn = [kkn_ref[rows, c] for c in cols]
        bb = [b_ref[rows, c] for c in cols]
        kd = [kd_ref[rows, c] for c in cols]
        rr = [r_ref[rows, c] for c in cols]
        vv = [v_ref[rows, c] for c in cols]
        cum_prev = [cm - x for cm, x in zip(cum, lw)]
        mid = [cm[C // 2:C // 2 + 1, :] for cm in cum]
        total = [jnp.where(d == 0, cm[C - 1:C, :], cm[0:1, :]) for cm in cum]
        e_in = [jnp.exp(cm - m) for cm, m in zip(cum, mid)]
        e_in_prev = [jnp.exp(cm - m) for cm, m in zip(cum_prev, mid)]
        e_out = [jnp.exp(m - cm) for cm, m in zip(cum, mid)]
        e_tail = [jnp.exp(t - cm) for cm, t in zip(cum, total)]
        a1s = [_stack_heads(x * e, head0) for x, e in zip(kkn, e_in_prev)]
        r1s = [_stack_heads(x * e, head0) for x, e in zip(rr, e_in)]
        b1 = [x * e for x, e in zip(bb, e_out)]
        k1 = [x * e for x, e in zip(kd, e_out)]
        a0s = [_stack_heads(x * jnp.exp(cm), head0) for x, cm in zip(kkn, cum_prev)]
        vs = [_stack_heads(x, head0) for x in vv]
        g_t = [_dot_nt(jnp.concatenate([b, k], axis=0), a) for a, b, k in zip(a1s, b1, k1)]
        l_ab_t = [jnp.concatenate([g_[:C]] * 2, axis=0) * strict_t for g_ in g_t]
        l_ak_t = [jnp.concatenate([g_[C:]] * 2, axis=0) * strict_t for g_ in g_t]
        g_r = [_dot_nt(a, jnp.concatenate([b, b, k, k], axis=0)) for a, b, k in zip(r1s, b1, k1)]
        m_rb = [g_[:, :PAIR_W] * incl for g_ in g_r]
        m_rk = [g_[:, PAIR_W:] * incl for g_ in g_r]
        lakv_t = [_dot_tn(a, b) for a, b in zip(vs, l_ak_t)]
        mv = [_unstack_heads(_dot(a, b)) for a, b in zip(m_rk, vs)]
        kv = [_dot_tn(a, b * e) * bd_f for a, b, e in zip(vv, kd, e_tail)]
        x = [eye - l_ for l_ in l_ab_t]
        pw = l_ab_t
        step = 1
        while 2 * step < C:
            pw = [_dot(q, q) for q in pw]
            x = [xx + _dot(xx, q) for xx, q in zip(x, pw)]
            step *= 2
        a_hat_t = [_dot_tn(a, xx) for xx, a in zip(x, a0s)]
        y_hat_t = [_dot(a, xx) for xx, a in zip(x, lakv_t)]
        for p in pairs:
            ahat_ref[i, p] = a_hat_t[p].astype(BF16)
            yhat_ref[i, p] = y_hat_t[p]
            mrb_ref[i, p] = m_rb[p].astype(BF16)
            r0_ref[i, p] = (rr[p] * jnp.exp(cum[p])).astype(BF16)
            b2s_ref[i, p] = _stack_heads(bb[p] * e_tail[p], head0).astype(BF16)
            kv_ref[i, p] = kv[p]
            mv_ref[i, p] = mv[p]
            dec_ref[i, p] = jnp.broadcast_to(jnp.exp(total[p]), (8, PAIR_W))
        return carry

    lax.fori_loop(0, WKV_GROUP, phase_a, 0, unroll=2)

    def phase_b(i, carry):
        rows = chunk_rows(i)
        s_old = [s_ref[p] for p in pairs]
        s_bf = [s.astype(BF16) for s in s_old]
        yt = [(_dot(s_bf[p], ahat_ref[i, p]) + yhat_ref[i, p]).astype(BF16) for p in pairs]
        for p in pairs:
            s_ref[p] = dec_ref[i, p, 0:1, :] * s_old[p] + kv_ref[i, p] - _dot(yt[p], b2s_ref[i, p])
        o = [_dot_nt(r0_ref[i, p], s_bf[p]) + mv_ref[i, p] - _unstack_heads(_dot_nt(mrb_ref[i, p], yt[p]))
             for p in pairs]
        for p in pairs:
            o_ref[0, rows, cols[p]] = o[p]
        return carry

    lax.fori_loop(0, WKV_GROUP, phase_b, 0, unroll=True)

    if has_state_out:
        @pl.when(g == pl.num_programs(2) - 1)
        def _():
            for p in pairs:
                s = s_ref[p]
                sout_ref[2 * p] = s[:HEAD_DIM, :HEAD_DIM]
                sout_ref[2 * p + 1] = s[HEAD_DIM:, HEAD_DIM:]


def _wkv_call(proj, lp, l, s0, o_prev, row0, n_seq, seq_len, has_state_out):
    n_groups = seq_len // WKV_ROWS
    blk0 = row0 // WKV_ROWS
    has_init = s0 is not None
    has_alias = o_prev is not None

    def group(d, s, g):
        return s * n_groups + g + d * (n_groups - 1 - 2 * g)

    def col(cb):
        return lambda d, s, g: (blk0 + group(d, s, g), cb)

    per_dir_mat = pl.BlockSpec((None, None, LORA_W, D_RWKV), lambda d, s, g: (l, d, 0, 0))
    per_dir_vec = pl.BlockSpec((None, None, 1, D_RWKV), lambda d, s, g: (l, d, 0, 0))
    shared = pl.BlockSpec((None, 1, D_RWKV), lambda d, s, g: (l, 0, 0))
    in_specs = [
        pl.BlockSpec((WKV_ROWS, D_RWKV), col(0)),
        pl.BlockSpec((WKV_ROWS, D_RWKV), col(1)),
        pl.BlockSpec((WKV_ROWS, D_RWKV), col(2)),
        pl.BlockSpec((WKV_ROWS, TN_IN), col(LO_BLOCK)),
        per_dir_mat, per_dir_vec, per_dir_mat, per_dir_vec, shared, shared,
    ]
    args = [proj, proj, proj, proj, lp["w_up"], lp["w0"], lp["a_up"], lp["a0"], lp["k_k"], lp["k_a"]]
    if has_init:
        in_specs.append(pl.BlockSpec((None, None, None, N_HEADS, HEAD_DIM, HEAD_DIM),
                                     lambda d, s, g: (s, l, d, 0, 0, 0)))
        args.append(s0)
    aliases = {}
    if has_alias:
        aliases = {len(args): 0}
        in_specs.append(pl.BlockSpec(memory_space=pl.ANY))
        args.append(o_prev)
    out_shape = [jax.ShapeDtypeStruct((2, proj.shape[0], D_RWKV), F32)]
    out_specs = [pl.BlockSpec((1, WKV_ROWS, D_RWKV), lambda d, s, g: (d, blk0 + group(d, s, g), 0))]
    if has_state_out:
        out_shape.append(jax.ShapeDtypeStruct((n_seq, 2, N_HEADS, HEAD_DIM, HEAD_DIM), F32))
        out_specs.append(pl.BlockSpec((None, None, N_HEADS, HEAD_DIM, HEAD_DIM), lambda d, s, g: (s, d, 0, 0, 0)))
    kern = functools.partial(_wkv_kernel, has_init=has_init, has_state_out=has_state_out, has_alias=has_alias)
    per_chunk = (WKV_GROUP, N_PAIRS)
    return pl.pallas_call(
        kern,
        out_shape=out_shape,
        grid=(2, n_seq, n_groups),
        in_specs=in_specs,
        out_specs=out_specs,
        input_output_aliases=aliases,
        scratch_shapes=[
            pltpu.VMEM((N_PAIRS, PAIR_W, PAIR_W), F32),
            pltpu.VMEM((WKV_ROWS, D_RWKV), F32),
            pltpu.VMEM((WKV_ROWS, D_RWKV), F32),
            pltpu.VMEM((WKV_ROWS, D_RWKV), F32),
            pltpu.VMEM((WKV_ROWS, D_RWKV), F32),
            pltpu.VMEM((WKV_ROWS, D_RWKV), F32),
            pltpu.VMEM((4, PAIR_W, PAIR_W), F32),
            pltpu.VMEM(per_chunk + (PAIR_W, PAIR_W), BF16),
            pltpu.VMEM(per_chunk + (PAIR_W, PAIR_W), F32),
            pltpu.VMEM(per_chunk + (PAIR_W, PAIR_W), BF16),
            pltpu.VMEM(per_chunk + (CHUNK, PAIR_W), BF16),
            pltpu.VMEM(per_chunk + (PAIR_W, PAIR_W), BF16),
            pltpu.VMEM(per_chunk + (PAIR_W, PAIR_W), F32),
            pltpu.VMEM(per_chunk + (CHUNK, PAIR_W), F32),
            pltpu.VMEM(per_chunk + (8, PAIR_W), F32),
        ],
        compiler_params=_cparams(("parallel", "parallel", "arbitrary")),
        name="wkv",
    )(*args)


def _rwkvpost_kernel(o_ref, r_ref, k_ref, v_ref, lo_ref, aup_ref, a0_ref, ka_ref, rk_ref, lng_ref, lnb_ref,
                     gup_ref, out_ref):
    lo = lo_ref[...]
    xa = lo[:, LORA_W:LORA_W + LORA_A]
    sg = _sigmoid(lo[:, LORA_W + LORA_A:LORA_W + LORA_A + LORA_G_PAD])
    gi = lax.broadcasted_iota(jnp.int32, (PAIR_W, PAIR_W), 0)
    gj = lax.broadcasted_iota(jnp.int32, (PAIR_W, PAIR_W), 1)
    head_sum = jnp.where((gi >= HEAD_DIM) == (gj >= HEAD_DIM), 1.0, 0.0).astype(BF16)
    for p in range(N_PAIRS):
        cols = slice(p * PAIR_W, (p + 1) * PAIR_W)
        o = o_ref[0, :, cols] + o_ref[1, :, cols]
        mu = _dot2(o, head_sum) * (1.0 / HEAD_DIM)
        oc = o - mu
        var = _dot2(oc * oc, head_sum) * (1.0 / HEAD_DIM)
        y = oc * lax.rsqrt(var + GN_EPS) * lng_ref[:, cols] + lnb_ref[:, cols]
        k = k_ref[:, cols]
        ic0 = _sigmoid(a0_ref[0, :, cols] + _dot3(xa, aup_ref[0, :, cols]))
        ic1 = _sigmoid(a0_ref[1, :, cols] + _dot3(xa, aup_ref[1, :, cols]))
        ka = ka_ref[:, cols]
        kd_sum = k * (2.0 + (ic0 + ic1 - 2.0) * ka)
        bonus = _dot2(r_ref[:, cols] * kd_sum * rk_ref[:, cols], head_sum) * v_ref[:, cols]
        g = _dot(sg, gup_ref[:, cols])
        out_ref[:, cols] = ((y + bonus) * g).astype(BF16)


def _rwkvpost_call(o, proj, lp, l):
    n_rows = o.shape[1]

    def col(cb):
        return lambda i: (i, cb)

    shared = pl.BlockSpec((None, 1, D_RWKV), lambda i: (l, 0, 0))
    return pl.pallas_call(
        _rwkvpost_kernel,
        out_shape=jax.ShapeDtypeStruct((n_rows, D_RWKV), BF16),
        grid=(n_rows // TM_POST,),
        in_specs=[
            pl.BlockSpec((2, TM_POST, D_RWKV), lambda i: (0, i, 0)),
            pl.BlockSpec((TM_POST, D_RWKV), col(0)),
            pl.BlockSpec((TM_POST, D_RWKV), col(1)),
            pl.BlockSpec((TM_POST, D_RWKV), col(2)),
            pl.BlockSpec((TM_POST, TN_IN), col(LO_BLOCK)),
            pl.BlockSpec((None, 2, LORA_A, D_RWKV), lambda i: (l, 0, 0, 0)),
            pl.BlockSpec((None, 2, 1, D_RWKV), lambda i: (l, 0, 0, 0)),
            shared, shared, shared, shared,
            pl.BlockSpec((None, LORA_G_PAD, D_RWKV), lambda i: (l, 0, 0)),
        ],
        out_specs=pl.BlockSpec((TM_POST, D_RWKV), lambda i: (i, 0)),
        compiler_params=_cparams(("parallel",)),
        name="rwkvpost",
    )(o, proj, proj, proj, proj, lp["a_up"], lp["a0"], lp["k_a"], lp["r_k"], lp["ln_g"], lp["ln_b"], lp["g_up"])


def _dft_mats(L):
    idx = (np.arange(L)[:, None] * np.arange(L)[None, :]) % (2 * L)
    ang = np.pi * idx.astype(np.float64) / L
    alt = np.cos(np.pi * np.arange(L))
    fc = np.cos(ang)
    fs = -np.sin(ang)
    fs[0, :] = alt
    fwd = np.concatenate([fc, fs], axis=0)
    ic = 2.0 * np.cos(ang.T)
    ic[:, 0] = 1.0
    isn = -2.0 * np.sin(ang.T)
    isn[:, 0] = alt
    inv = np.concatenate([ic, isn], axis=1) / (2 * L)
    return fwd.astype(np.float32), inv.astype(np.float32)


def _filter_features(L):
    t = np.linspace(0.0, 1.0, L, dtype=np.float32)[:, None]
    w = (2.0 * math.pi / L) * np.arange(L, dtype=np.float32)[:, None]
    f = np.linspace(1e-4, FILT_BANDS - 1, FILT_BANDS, dtype=np.float32)[None, :]
    z = np.concatenate([t, np.cos(f * w), -np.sin(f * w)], axis=-1)
    zp = np.zeros((L, FILT_EMB_PAD), np.float32)
    zp[:, :FILT_EMB] = z
    return zp, t


def _filter_deltas():
    max_decay = math.log(FILT_TARGET) / FAST_DECAY_PCT
    min_decay = math.log(FILT_TARGET) / SLOW_DECAY_PCT
    return np.abs(np.linspace(min_decay, max_decay, D_HYENA, dtype=np.float32))[None, :]


def _hyfilt_kernel(z_ref, t_ref, dl_ref, fwd_ref, w1_ref, b1_ref, w2_ref, b2_ref,
                   w3f0_ref, w3b0_ref, w3f1_ref, w3b1_ref, fr_ref, o_ref):
    L = z_ref.shape[0]
    tc = o_ref.shape[3]
    h = jnp.sin(fr_ref[0:1, :] * (_dot3(z_ref[...], w1_ref[...]) + b1_ref[...]))
    h = jnp.sin(fr_ref[1:2, :] * (_dot3(h, w2_ref[...]) + b2_ref[...]))
    decay = jnp.exp(-t_ref[...] * dl_ref[...])
    first = lax.broadcasted_iota(jnp.int32, (L, tc), 0) == 0
    fwd = fwd_ref[...]
    for order, (wf_ref, wb_ref) in enumerate(((w3f0_ref, w3b0_ref), (w3f1_ref, w3b1_ref))):
        hf = _dot3(h, wf_ref[...]) * decay
        hb = _dot3(h, wb_ref[...]) * decay
        norm = jnp.sum(jnp.abs(hf) + jnp.abs(hb), axis=0, keepdims=True)
        hf = hf / norm
        hb = jnp.where(first, 0.0, hb / norm)
        ks = _dot(fwd, hf + hb)
        kd = _dot(fwd, hf - hb)
        kr = ks[:L]
        o_ref[order, 0] = kr
        o_ref[order, 1] = jnp.where(first, 0.0, kd[L:])
        o_ref[order, 2] = jnp.where(first, ks[L:L + 1], kr)


def _hyfilt_call(L, fp, l):
    z, t = _filter_features(L)
    fwd, _ = _dft_mats(L)
    nct = D_HYENA // TC_FILT
    full = lambda shape: pl.BlockSpec(shape, lambda j: tuple(0 for _ in shape))
    layer = lambda shape: pl.BlockSpec((None,) + shape, lambda j: (l,) + tuple(0 for _ in shape))
    w3_spec = lambda grp: pl.BlockSpec((None, FILT_HIDDEN, TC_FILT), lambda j: (l, 0, grp * nct + j))
    w3 = fp["w3"]
    return pl.pallas_call(
        _hyfilt_kernel,
        out_shape=jax.ShapeDtypeStruct((2, 3, L, D_HYENA), F32),
        grid=(nct,),
        in_specs=[
            full((L, FILT_EMB_PAD)), full((L, 1)), pl.BlockSpec((1, TC_FILT), lambda j: (0, j)), full((2 * L, L)),
            layer((FILT_EMB_PAD, FILT_HIDDEN)), layer((1, FILT_HIDDEN)),
            layer((FILT_HIDDEN, FILT_HIDDEN)), layer((1, FILT_HIDDEN)),
            w3_spec(0), w3_spec(1), w3_spec(2), w3_spec(3),
            layer((2, FILT_HIDDEN)),
        ],
        out_specs=pl.BlockSpec((2, 3, L, TC_FILT), lambda j: (0, 0, 0, j)),
        compiler_params=_cparams(("parallel",)),
        name="hyfilt",
    )(jnp.asarray(z), jnp.asarray(t), jnp.asarray(_filter_deltas()), jnp.asarray(fwd).astype(BF16),
      fp["w1"], fp["b1"], fp["w2"], fp["b2"], w3, w3, w3, w3, fp["freq"])


def _hyena_kernel(u_ref, x1_ref, x2_ref, fwd_ref, inv_ref, spec_ref, bias_ref, *rest):
    out_ref = rest[-1]
    L = u_ref.shape[0]
    fwd = fwd_ref[...]
    inv = inv_ref[...]

    def long_conv(u, order):
        spec = _dot(fwd, u)
        ur, ui = spec[:L], spec[L:]
        kr, ki, kr2 = spec_ref[order, 0], spec_ref[order, 1], spec_ref[order, 2]
        yr = ur * kr - ui * ki
        yi = ur * ki + ui * kr2
        y = _dot(inv, jnp.concatenate([yr, yi], axis=0))
        return y + u * bias_ref[order:order + 1, :]

    z = x1_ref[...] * long_conv(u_ref[...], 0)
    out_ref[...] = (x2_ref[...] * long_conv(z, 1)).astype(BF16)


def _hyena_call(proj, spec, bias, l, out_prev, row0, n_seq, L, tc):
    fwd, inv = _dft_mats(L)
    nct = D_HYENA // tc
    blk0 = row0 // L
    cb0 = 3 * D_RWKV // tc

    def col(which):
        return lambda s, j: (blk0 + s, cb0 + which * nct + j)

    in_specs = [
        pl.BlockSpec((L, tc), col(0)),
        pl.BlockSpec((L, tc), col(1)),
        pl.BlockSpec((L, tc), col(2)),
        pl.BlockSpec((2 * L, L), lambda s, j: (0, 0)),
        pl.BlockSpec((L, 2 * L), lambda s, j: (0, 0)),
        pl.BlockSpec((2, 3, L, tc), lambda s, j: (0, 0, 0, j)),
        pl.BlockSpec((None, 2, tc), lambda s, j: (l, 0, j)),
    ]
    args = [proj, proj, proj, jnp.asarray(fwd).astype(BF16), jnp.asarray(inv).astype(BF16), spec, bias]
    aliases = {}
    if out_prev is not None:
        aliases = {len(args): 0}
        in_specs.append(pl.BlockSpec(memory_space=pl.ANY))
        args.append(out_prev)
    return pl.pallas_call(
        _hyena_kernel,
        out_shape=jax.ShapeDtypeStruct((proj.shape[0], D_HYENA), BF16),
        grid=(n_seq, nct),
        in_specs=in_specs,
        out_specs=pl.BlockSpec((L, tc), lambda s, j: (blk0 + s, j)),
        input_output_aliases=aliases,
        compiler_params=_cparams(("parallel", "parallel")),
        name="hyena",
    )(*args)


def _outproj_kernel(a_ref, b_ref, x_ref, g_ref, w_ref, lg_ref, lb_ref, o_ref):
    mix = (jnp.dot(a_ref[...], w_ref[0:D_RWKV, :], preferred_element_type=F32)
           + jnp.dot(b_ref[...], w_ref[D_RWKV:, :], preferred_element_type=F32))
    y = ALPHA * x_ref[...] + g_ref[0, 0] * mix
    o_ref[...] = _layer_norm_rows(y) * lg_ref[...] + lb_ref[...]


def _outproj_call(a_out, b_out, x, mod_all, w_out, ln_g, ln_b, l, n_ctx_units):
    grp = lambda i: _tile_group(i, TM_OUT, n_ctx_units)
    vec = pl.BlockSpec((None, 1, D_MODEL), lambda i: (l, 0, 0))
    return pl.pallas_call(
        _outproj_kernel,
        out_shape=jax.ShapeDtypeStruct(x.shape, F32),
        grid=(x.shape[0] // TM_OUT,),
        in_specs=[
            pl.BlockSpec((TM_OUT, D_RWKV), lambda i: (i, 0)),
            pl.BlockSpec((TM_OUT, D_HYENA), lambda i: (i, 0)),
            pl.BlockSpec((TM_OUT, D_MODEL), lambda i: (i, 0)),
            _mod_spec(l, 2, grp),
            pl.BlockSpec((None, D_MODEL, D_MODEL), lambda i: (l, 0, 0)),
            vec, vec,
        ],
        out_specs=pl.BlockSpec((TM_OUT, D_MODEL), lambda i: (i, 0)),
        compiler_params=_cparams(("parallel",)),
        name="outproj",
    )(a_out, b_out, x, mod_all, w_out, ln_g, ln_b)


def _mlp_kernel(x_ref, sh_ref, sc_ref, g_ref, w1_ref, w2_ref, lg_ref, lb_ref, o_ref, h_ref):
    f = pl.program_id(1)
    row_blocks = o_ref.shape[0] // 128

    @pl.when(f == 0)
    def _():
        sc = 1.0 + sc_ref[0, 0]
        sh = sh_ref[0, 0]

        def body(rb, carry):
            rows = pl.ds(pl.multiple_of(rb * 128, 128), 128)
            h_ref[rows, :] = (_layer_norm_rows(x_ref[rows, :]) * sc + sh).astype(BF16)
            return carry

        lax.fori_loop(0, row_blocks, body, 0)
        o_ref[...] = jnp.zeros(o_ref.shape, F32)

    hid = jnp.maximum(jnp.dot(h_ref[...], w1_ref[...].astype(BF16), preferred_element_type=F32), 0.0)
    hid = (hid * hid).astype(BF16)
    o_ref[...] += jnp.dot(hid, w2_ref[...].astype(BF16), preferred_element_type=F32)

    @pl.when(f == pl.num_programs(1) - 1)
    def _():
        gate = g_ref[0, 0]
        lg = lg_ref[...]
        lb = lb_ref[...]

        def body(rb, carry):
            rows = pl.ds(pl.multiple_of(rb * 128, 128), 128)
            y = ALPHA * x_ref[rows, :] + gate * o_ref[rows, :]
            o_ref[rows, :] = _layer_norm_rows(y) * lg + lb
            return carry

        lax.fori_loop(0, row_blocks, body, 0)


def _mlp_call(x, mod_all, w1, w2, ln_g, ln_b, l, n_ctx_units):
    grp = lambda i: _tile_group(i, TM_MLP, n_ctx_units)
    vec = pl.BlockSpec((None, 1, D_MODEL), lambda i, f: (l, 0, 0))
    return pl.pallas_call(
        _mlp_kernel,
        out_shape=jax.ShapeDtypeStruct(x.shape, F32),
        grid=(x.shape[0] // TM_MLP, D_FF // TF_MLP),
        in_specs=[
            pl.BlockSpec((TM_MLP, D_MODEL), lambda i, f: (i, 0), pipeline_mode=pl.Buffered(1)),
            _mod_spec(l, 3, grp),
            _mod_spec(l, 4, grp),
            _mod_spec(l, 5, grp),
            pl.BlockSpec((None, D_MODEL, TF_MLP), lambda i, f: (l, 0, f)),
            pl.BlockSpec((None, TF_MLP, D_MODEL), lambda i, f: (l, f, 0)),
            vec, vec,
        ],
        out_specs=pl.BlockSpec((TM_MLP, D_MODEL), lambda i, f: (i, 0)),
        scratch_shapes=[pltpu.VMEM((TM_MLP, D_MODEL), BF16)],
        compiler_params=_cparams(("parallel", "arbitrary")),
        name="mlp",
    )(x, mod_all, mod_all, mod_all, w1, w2, ln_g, ln_b)


def kernel(x_prompt, x_sample, c, state_rwkv, c_ctx, w_ada, b_ada, w_in, conv_w, lora_w_up, lora_w0, lora_a_up, lora_a0, lora_g_up, rwkv_k_k, rwkv_k_a, rwkv_r_k, rwkv_ln_g, rwkv_ln_b, filt_w1, filt_b1, filt_w2, filt_b2, filt_w3, filt_freq, hyena_bias, w_out, ln1_g, ln1_b, ln2_g, ln2_b, mlp_w1, mlp_w2):
    n_ctx, ctx_seq, _ = x_prompt.shape
    n_lat, lat_seq, _ = x_sample.shape
    depth = w_ada.shape[0]
    assert ctx_seq == CTX_SEQ and lat_seq == LAT_SEQ and (n_ctx * ctx_seq) % TM == 0
    assert 1 + n_lat <= N_COND
    ctx_rows = n_ctx * ctx_seq
    lat_rows = n_lat * lat_seq
    n_ctx_units = ctx_rows // TM

    x = jnp.concatenate([x_prompt.reshape(ctx_rows, D_MODEL), x_sample.reshape(lat_rows, D_MODEL)], axis=0)
    cond = jnp.concatenate([c_ctx[None, :], c, jnp.zeros((N_COND - 1 - n_lat, D_MODEL), F32)], axis=0)
    mod_all = _mod_call(cond, w_ada, b_ada).reshape(depth, N_COND, 6, 1, D_MODEL)

    w_in_bf = w_in.astype(BF16)
    w_lo_bf = jnp.pad(w_in[:, :, N_CONV:].astype(BF16), ((0, 0), (0, 0), (0, TN_IN - (D_IN - N_CONV))))
    conv_w9 = conv_w.reshape(depth, 9, N_CONV)
    w_out_bf = w_out.astype(BF16)
    row = lambda a: a.reshape(depth, 1, a.shape[-1])
    lp = {
        "w_up": lora_w_up, "w0": lora_w0[:, :, None, :], "a_up": lora_a_up, "a0": lora_a0[:, :, None, :],
        "k_k": row(rwkv_k_k), "k_a": row(rwkv_k_a), "r_k": rwkv_r_k.reshape(depth, 1, D_RWKV),
        "ln_g": row(rwkv_ln_g), "ln_b": row(rwkv_ln_b),
        "g_up": jnp.pad(lora_g_up, ((0, 0), (0, LORA_G_PAD - LORA_G), (0, 0))),
    }
    fp = {
        "w1": jnp.pad(filt_w1, ((0, 0), (0, FILT_EMB_PAD - FILT_EMB), (0, 0))), "b1": row(filt_b1),
        "w2": filt_w2, "b2": row(filt_b2), "w3": filt_w3, "freq": filt_freq,
    }
    ln1_g, ln1_b, ln2_g, ln2_b = row(ln1_g), row(ln1_b), row(ln2_g), row(ln2_b)
    state_in = state_rwkv.astype(F32)

    ctx_states = []
    for l in range(depth):
        proj = _inproj_call(x, mod_all, w_in_bf, w_lo_bf, conv_w9, l, n_ctx_units)

        o, s_ctx = _wkv_call(proj, lp, l, None, None, 0, n_ctx, ctx_seq, True)
        (o,) = _wkv_call(proj, lp, l, state_in, o, ctx_rows, n_lat, lat_seq, False)
        a_out = _rwkvpost_call(o, proj, lp, l)
        ctx_states.append(s_ctx)

        spec_ctx = _hyfilt_call(ctx_seq, fp, l)
        spec_lat = _hyfilt_call(lat_seq, fp, l)
        b_out = _hyena_call(proj, spec_ctx, hyena_bias, l, None, 0, n_ctx, ctx_seq, D_HYENA)
        b_out = _hyena_call(proj, spec_lat, hyena_bias, l, b_out, ctx_rows, n_lat, lat_seq, 256)

        x = _outproj_call(a_out, b_out, x, mod_all, w_out_bf, ln1_g, ln1_b, l, n_ctx_units)
        x = _mlp_call(x, mod_all, mlp_w1, mlp_w2, ln2_g, ln2_b, l, n_ctx_units)

    y_prompt = x[:ctx_rows].reshape(n_ctx, ctx_seq, D_MODEL)
    y_sample = x[ctx_rows:].reshape(n_lat, lat_seq, D_MODEL)
    new_state = jnp.stack(ctx_states, axis=1).astype(x_prompt.dtype)
    return (y_prompt, y_sample, new_state)
```

```python
import functools
import math

import jax
import jax.numpy as jnp
import numpy as np
from jax import lax
from jax.experimental import pallas as pl
from jax.experimental.pallas import tpu as pltpu

F32 = jnp.float32
BF16 = jnp.bfloat16

D_MODEL = 2048
D_RWKV = 1024
D_HYENA = 1024
HEAD_DIM = 64
N_HEADS = D_RWKV // HEAD_DIM
N_PAIRS = N_HEADS // 2
PAIR_W = 2 * HEAD_DIM
LORA_W = 64
LORA_A = 64
LORA_G = 160
LORA_G_PAD = 256
N_CONV = 3 * D_RWKV + 3 * D_HYENA
D_IN = N_CONV + LORA_W + LORA_A + LORA_G
D_FF = 4 * D_MODEL
GRID_W = 64
CTX_SEQ = 256
LAT_SEQ = 1024
FILT_BANDS = 16
FILT_EMB = 1 + 2 * FILT_BANDS
FILT_EMB_PAD = 128
FILT_HIDDEN = 64
DEPTH = 2
ALPHA = (2 * DEPTH) ** 0.25
LN_EPS = 1e-5
GN_EPS = 64e-5
FILT_TARGET = 1e-2
FAST_DECAY_PCT = 0.3
SLOW_DECAY_PCT = 1.5
DECAY_SCALE = math.exp(-0.5)

TM = 1024
TN_IN = 512
NJ_CONV = N_CONV // TN_IN
D_IN_PAD = (NJ_CONV + 1) * TN_IN
LO_BLOCK = N_CONV // TN_IN
CONV_PAD = 72
CHUNK = 64
WKV_GROUP = 4
WKV_ROWS = WKV_GROUP * CHUNK
N_COND = 8
TN_MOD = 1024
TM_OUT = 512
TM_MLP = 1024
TF_MLP = 512
TM_POST = 256
TC_FILT = 256
VMEM_LIMIT = 56 * 1024 * 1024


def _cparams(sem):
    return pltpu.CompilerParams(dimension_semantics=sem, vmem_limit_bytes=VMEM_LIMIT)


def _dot(a, b):
    return jnp.dot(a.astype(BF16), b.astype(BF16), preferred_element_type=F32)


def _dot_nt(a, b):
    return lax.dot_general(a.astype(BF16), b.astype(BF16), (((1,), (1,)), ((), ())),
                           preferred_element_type=F32)


def _dot_tn(a, b):
    return lax.dot_general(a.astype(BF16), b.astype(BF16), (((0,), (0,)), ((), ())),
                           preferred_element_type=F32)


def _split(x):
    hi = x.astype(BF16)
    lo = (x - hi.astype(F32)).astype(BF16)
    return hi, lo


def _dot3(a, b):
    ah, al = _split(a)
    bh, bl = _split(b)
    return _dot(ah, bh) + (_dot(ah, bl) + _dot(al, bh))


def _dot2(a, b_exact):
    ah, al = _split(a)
    return _dot(ah, b_exact) + _dot(al, b_exact)


def _dot2_left(a_exact, b):
    bh, bl = _split(b)
    return _dot(a_exact, bh) + _dot(a_exact, bl)


def _sigmoid(x):
    return 1.0 / (1.0 + jnp.exp(-x))


def _layer_norm_rows(x):
    mu = jnp.mean(x, axis=-1, keepdims=True)
    xc = x - mu
    var = jnp.mean(xc * xc, axis=-1, keepdims=True)
    return xc * lax.rsqrt(var + LN_EPS)


def _unit_group(i, n_ctx_units):
    return jnp.maximum(i - n_ctx_units + 1, 0)


def _tile_group(i, tm, n_ctx_units):
    return _unit_group(i // (TM // tm), n_ctx_units)


def _mod_spec(l, which, grp):
    return pl.BlockSpec((None, 1, 1, 1, D_MODEL), lambda i, *_: (l, grp(i), which, 0, 0))


def _mod_kernel(c_ref, w_ref, b_ref, o_ref):
    c = c_ref[...]
    s = c * _sigmoid(c)
    o_ref[0] = _dot(s, w_ref[0]) + b_ref[0]


def _mod_call(cond, w_ada, b_ada):
    depth = w_ada.shape[0]
    n_out = w_ada.shape[2]
    return pl.pallas_call(
        _mod_kernel,
        out_shape=jax.ShapeDtypeStruct((depth, N_COND, n_out), F32),
        grid=(depth, n_out // TN_MOD),
        in_specs=[
            pl.BlockSpec((N_COND, D_MODEL), lambda l, j: (0, 0)),
            pl.BlockSpec((1, D_MODEL, TN_MOD), lambda l, j: (l, 0, j)),
            pl.BlockSpec((1, 1, TN_MOD), lambda l, j: (l, 0, j)),
        ],
        out_specs=pl.BlockSpec((1, N_COND, TN_MOD), lambda l, j: (l, 0, j)),
        compiler_params=_cparams(("parallel", "parallel")),
        name="mod",
    )(cond, w_ada, b_ada.reshape(depth, 1, n_out))


def _inproj_kernel(x_ref, sh_ref, sc_ref, w_ref, wlo_ref, cw_ref, o_ref, h_ref, wbf_ref, acc_a, acc_b, *,
                   n_ctx_units):
    i = pl.program_id(0)
    j = pl.program_id(1)
    tn = o_ref.shape[1]
    is_grid = i >= n_ctx_units
    mid = slice(CONV_PAD, CONV_PAD + TM)

    n_blocks = TM // GRID_W
    n_parts = 4
    lane_w = 128
    k_chunk = 256

    def round_weights():
        for kc in range(D_MODEL // k_chunk):
            rows = slice(kc * k_chunk, (kc + 1) * k_chunk)
            wbf_ref[rows, :] = w_ref[rows, :].astype(BF16)

    def matmul_into(dst_ref, part=None):
        if part is None:
            dst_ref[mid, :] = jnp.dot(h_ref[...], wbf_ref[...], preferred_element_type=F32)
        else:
            r0, r1 = part * (TM // n_parts), (part + 1) * (TM // n_parts)
            dst_ref[CONV_PAD + r0:CONV_PAD + r1, :] = jnp.dot(h_ref[r0:r1, :], wbf_ref[...],
                                                              preferred_element_type=F32)

    def conv_from(src_ref, g0=0, g1=n_blocks):
        row = lax.broadcasted_iota(jnp.int32, (GRID_W, lane_w), 0)
        first = row == 0
        last = row == GRID_W - 1
        first_grid = jnp.logical_and(first, is_grid)
        last_grid = jnp.logical_and(last, is_grid)
        tap_row = lax.broadcasted_iota(jnp.int32, (9, lane_w), 0)
        centre = jnp.logical_and(tap_row >= 3, tap_row < 6)
        keep_tap = jnp.logical_or(centre, is_grid)
        per_seq = CTX_SEQ // GRID_W
        for c in range(tn // lane_w):
            lanes = slice(c * lane_w, (c + 1) * lane_w)
            cw = jnp.where(keep_tap, cw_ref[:, lanes], 0.0)
            partial = {}
            for s in range(max(g0 - 1, 0), min(g1 + 1, n_blocks)):
                start = CONV_PAD + s * GRID_W
                uc = src_ref[start:start + GRID_W, lanes]
                ul = src_ref[start - 1:start - 1 + GRID_W, lanes]
                ur = src_ref[start + 1:start + 1 + GRID_W, lanes]
                ul = jnp.where(first if s % per_seq == 0 else first_grid, 0.0, ul)
                ur = jnp.where(last if s % per_seq == per_seq - 1 else last_grid, 0.0, ur)
                for a in range(3):
                    gi = s - (a - 1)
                    if g0 <= gi < g1:
                        t = (ul * cw[3 * a:3 * a + 1, :] + uc * cw[3 * a + 1:3 * a + 2, :]
                             + ur * cw[3 * a + 2:3 * a + 3, :])
                        partial[gi] = t if gi not in partial else partial[gi] + t
                if s - 1 in partial:
                    o_ref[(s - 1) * GRID_W:s * GRID_W, lanes] = partial.pop(s - 1)
            for gi in sorted(partial):
                o_ref[gi * GRID_W:(gi + 1) * GRID_W, lanes] = partial.pop(gi)

    @pl.when(j == 0)
    def _():
        sc = 1.0 + sc_ref[0, 0]
        sh = sh_ref[0, 0]

        def body(rb, carry):
            rows = pl.ds(pl.multiple_of(rb * 128, 128), 128)
            h = _layer_norm_rows(x_ref[rows, :]) * sc + sh
            h_ref[rows, :] = h.astype(BF16)
            return carry

        lax.fori_loop(0, TM // 128, body, 0)
        for acc in (acc_a, acc_b):
            acc[0:CONV_PAD, :] = jnp.zeros((CONV_PAD, tn), F32)
            acc[CONV_PAD + TM:, :] = jnp.zeros((CONV_PAD, tn), F32)
        o_ref[...] = jnp.dot(h_ref[...], wlo_ref[...], preferred_element_type=F32)

    @pl.when(j == 1)
    def _():
        round_weights()
        matmul_into(acc_a)

    main = jnp.logical_and(j >= 2, j <= NJ_CONV)
    odd = jnp.bitwise_and(j, 1) == 1

    def overlapped(src_ref, dst_ref):
        round_weights()
        per_part = n_blocks // n_parts
        for part in range(n_parts):
            matmul_into(dst_ref, part)
            conv_from(src_ref, part * per_part, (part + 1) * per_part)

    @pl.when(jnp.logical_and(main, jnp.logical_not(odd)))
    def _():
        overlapped(acc_a, acc_b)

    @pl.when(jnp.logical_and(main, odd))
    def _():
        overlapped(acc_b, acc_a)

    @pl.when(j == NJ_CONV + 1)
    def _():
        conv_from(acc_b if NJ_CONV % 2 == 0 else acc_a)


def _inproj_call(x, mod_all, w_in, w_lo, conv_w, l, n_ctx_units):
    n_units = x.shape[0] // TM
    kern = functools.partial(_inproj_kernel, n_ctx_units=n_ctx_units)
    grp = lambda i: _unit_group(i, n_ctx_units)
    tile = lambda t: jnp.clip(t, 0, NJ_CONV - 1)
    return pl.pallas_call(
        kern,
        out_shape=jax.ShapeDtypeStruct((x.shape[0], D_IN_PAD), F32),
        grid=(n_units, NJ_CONV + 2),
        in_specs=[
            pl.BlockSpec((TM, D_MODEL), lambda i, j: (i, 0)),
            _mod_spec(l, 0, grp),
            _mod_spec(l, 1, grp),
            pl.BlockSpec((None, D_MODEL, TN_IN), lambda i, j: (l, 0, tile(j - 1))),
            pl.BlockSpec((None, D_MODEL, TN_IN), lambda i, j: (l, 0, 0)),
            pl.BlockSpec((None, 9, TN_IN), lambda i, j: (l, 0, tile(j - 2))),
        ],
        out_specs=pl.BlockSpec((TM, TN_IN), lambda i, j: (i, jnp.where(j == 0, LO_BLOCK, tile(j - 2)))),
        scratch_shapes=[
            pltpu.VMEM((TM, D_MODEL), BF16),
            pltpu.VMEM((D_MODEL, TN_IN), BF16),
            pltpu.VMEM((TM + 2 * CONV_PAD, TN_IN), F32),
            pltpu.VMEM((TM + 2 * CONV_PAD, TN_IN), F32),
        ],
        compiler_params=_cparams(("parallel", "arbitrary")),
        name="inproj",
    )(x, mod_all, mod_all, w_in, w_lo, conv_w)


def _stack_heads(z, head0):
    return jnp.concatenate([jnp.where(head0, z, 0.0), jnp.where(head0, 0.0, z)], axis=0)


def _unstack_heads(z):
    c = z.shape[0] // 2
    return z[:c] + z[c:]


def _wkv_kernel(*refs, has_init, has_state_out, has_alias):
    (r_ref, k_ref, v_ref, lo_ref, wup_ref, w0_ref, aup_ref, a0_ref, kk_ref, ka_ref) = refs[:10]
    pos = 10
    s0_ref = None
    if has_init:
        s0_ref = refs[pos]
        pos += 1
    if has_alias:
        pos += 1
    o_ref = refs[pos]
    pos += 1
    sout_ref = None
    if has_state_out:
        sout_ref = refs[pos]
        pos += 1
    (s_ref, lw_ref, cum_ref, kkn_ref, b_ref, kd_ref, msk_ref,
     ahat_ref, yhat_ref, mrb_ref, r0_ref, b2s_ref, kv_ref, mv_ref, dec_ref) = refs[pos:]

    d = pl.program_id(0)
    g = pl.program_id(2)
    C = CHUNK
    pairs = range(N_PAIRS)
    cols = [slice(p * PAIR_W, (p + 1) * PAIR_W) for p in pairs]

    @pl.when(g == 0)
    def _():
        if has_init:
            zero = jnp.zeros((HEAD_DIM, HEAD_DIM), F32)
            for p in pairs:
                top = jnp.concatenate([s0_ref[2 * p], zero], axis=1)
                bottom = jnp.concatenate([zero, s0_ref[2 * p + 1]], axis=1)
                s_ref[p] = jnp.concatenate([top, bottom], axis=0)
        else:
            s_ref[...] = jnp.zeros(s_ref.shape, F32)

    sgn = 1 - 2 * d
    si = lax.broadcasted_iota(jnp.int32, (2 * C, 2 * C), 0)
    sj = lax.broadcasted_iota(jnp.int32, (2 * C, 2 * C), 1)
    same_head = (si >= C) == (sj >= C)
    dlt = (jnp.bitwise_and(si, C - 1) - jnp.bitwise_and(sj, C - 1)) * sgn
    msk_ref[0] = jnp.where(jnp.logical_and(same_head, dlt < 0), 1.0, 0.0)
    msk_ref[1] = jnp.where(jnp.logical_and(same_head, dlt >= 0), 1.0, 0.0)
    msk_ref[2] = jnp.where(si == sj, 1.0, 0.0)
    blockdiag = (si >= HEAD_DIM) == (sj >= HEAD_DIM)
    msk_ref[3] = jnp.where(blockdiag, 1.0, 0.0)
    head_sum = jnp.where(blockdiag, 1.0, 0.0).astype(BF16)

    lo = lo_ref[...]
    xw = lo[:, 0:LORA_W]
    xa = lo[:, LORA_W:LORA_W + LORA_A]
    logw = -DECAY_SCALE * _sigmoid(w0_ref[...] + _dot3(jnp.tanh(xw), wup_ref[...]))
    lw_ref[...] = logw
    gi = lax.broadcasted_iota(jnp.int32, (WKV_ROWS, WKV_ROWS), 0)
    gj = lax.broadcasted_iota(jnp.int32, (WKV_ROWS, WKV_ROWS), 1)
    same_chunk = jnp.bitwise_and(gi, -C) == jnp.bitwise_and(gj, -C)
    cum_mask = jnp.where(jnp.logical_and(same_chunk, (gi - gj) * sgn >= 0), 1.0, 0.0).astype(BF16)
    cum_ref[...] = _dot2_left(cum_mask, logw)
    iclr = _sigmoid(a0_ref[...] + _dot3(xa, aup_ref[...]))
    k = k_ref[...]
    kd_ref[...] = k * (1.0 + (iclr - 1.0) * ka_ref[...])
    kkr = k * kk_ref[...]
    for p in pairs:
        slab = kkr[:, cols[p]]
        kkn = slab * lax.rsqrt(_dot(slab * slab, head_sum) + 1e-12)
        kkn_ref[:, cols[p]] = kkn
        b_ref[:, cols[p]] = kkn * iclr[:, cols[p]]

    def chunk_rows(i):
        cc = jnp.where(d == 0, i, WKV_GROUP - 1 - i)
        return pl.ds(pl.multiple_of(cc * C, C), C)

    def phase_a(i, carry):
        rows = chunk_rows(i)
        head0 = lax.broadcasted_iota(jnp.int32, (C, PAIR_W), 1) < HEAD_DIM
        strict_t = msk_ref[0]
        incl = msk_ref[1]
        eye = msk_ref[2]
        bd_f = msk_ref[3]

        lw = [lw_ref[rows, c] for c in cols]
        cum = [cum_ref[rows, c] for c in cols]
        kkn = [kkn_ref[rows, c] for c in cols]
        bb = [b_ref[rows, c] for c in cols]
        kd = [kd_ref[rows, c] for c in cols]
        rr = [r_ref[rows, c] for c in cols]
        vv = [v_ref[rows, c] for c in cols]
        cum_prev = [cm - x for cm, x in zip(cum, lw)]
        mid = [cm[C // 2:C // 2 + 1, :] for cm in cum]
        total = [jnp.where(d == 0, cm[C - 1:C, :], cm[0:1, :]) for cm in cum]
        e_in = [jnp.exp(cm - m) for cm, m in zip(cum, mid)]
        e_in_prev = [jnp.exp(cm - m) for cm, m in zip(cum_prev, mid)]
        e_out = [jnp.exp(m - cm) for cm, m in zip(cum, mid)]
        e_tail = [jnp.exp(t - cm) for cm, t in zip(cum, total)]
        a1s = [_stack_heads(x * e, head0) for x, e in zip(kkn, e_in_prev)]
        r1s = [_stack_heads(x * e, head0) for x, e in zip(rr, e_in)]
        b1 = [x * e for x, e in zip(bb, e_out)]
        k1 = [x * e for x, e in zip(kd, e_out)]
        a0s = [_stack_heads(x * jnp.exp(cm), head0) for x, cm in zip(kkn, cum_prev)]
        vs = [_stack_heads(x, head0) for x in vv]
        g_t = [_dot_nt(jnp.concatenate([b, k], axis=0), a) for a, b, k in zip(a1s, b1, k1)]
        l_ab_t = [jnp.concatenate([g_[:C]] * 2, axis=0) * strict_t for g_ in g_t]
        l_ak_t = [jnp.concatenate([g_[C:]] * 2, axis=0) * strict_t for g_ in g_t]
        g_r = [_dot_nt(a, jnp.concatenate([b, b, k, k], axis=0)) for a, b, k in zip(r1s, b1, k1)]
        m_rb = [g_[:, :PAIR_W] * incl for g_ in g_r]
        m_rk = [g_[:, PAIR_W:] * incl for g_ in g_r]
        lakv_t = [_dot_tn(a, b) for a, b in zip(vs, l_ak_t)]
        mv = [_unstack_heads(_dot(a, b)) for a, b in zip(m_rk, vs)]
        kv = [_dot_tn(a, b * e) * bd_f for a, b, e in zip(vv, kd, e_tail)]
        x = [eye - l_ for l_ in l_ab_t]
        pw = l_ab_t
        step = 1
        while 2 * step < C:
            pw = [_dot(q, q) for q in pw]
            x = [xx + _dot(xx, q) for xx, q in zip(x, pw)]
            step *= 2
        a_hat_t = [_dot_tn(a, xx) for xx, a in zip(x, a0s)]
        y_hat_t = [_dot(a, xx) for xx, a in zip(x, lakv_t)]
        for p in pairs:
            ahat_ref[i, p] = a_hat_t[p].astype(BF16)
            yhat_ref[i, p] = y_hat_t[p]
            mrb_ref[i, p] = m_rb[p].astype(BF16)
            r0_ref[i, p] = (rr[p] * jnp.exp(cum[p])).astype(BF16)
            b2s_ref[i, p] = _stack_heads(bb[p] * e_tail[p], head0).astype(BF16)
            kv_ref[i, p] = kv[p]
            mv_ref[i, p] = mv[p]
            dec_ref[i, p] = jnp.broadcast_to(jnp.exp(total[p]), (8, PAIR_W))
        return carry

    lax.fori_loop(0, WKV_GROUP, phase_a, 0, unroll=2)

    def phase_b(i, carry):
        rows = chunk_rows(i)
        s_old = [s_ref[p] for p in pairs]
        s_bf = [s.astype(BF16) for s in s_old]
        yt = [(_dot(s_bf[p], ahat_ref[i, p]) + yhat_ref[i, p]).astype(BF16) for p in pairs]
        for p in pairs:
            s_ref[p] = dec_ref[i, p, 0:1, :] * s_old[p] + kv_ref[i, p] - _dot(yt[p], b2s_ref[i, p])
        o = [_dot_nt(r0_ref[i, p], s_bf[p]) + mv_ref[i, p] - _unstack_heads(_dot_nt(mrb_ref[i, p], yt[p]))
             for p in pairs]
        for p in pairs:
            o_ref[0, rows, cols[p]] = o[p]
        return carry

    lax.fori_loop(0, WKV_GROUP, phase_b, 0, unroll=True)

    if has_state_out:
        @pl.when(g == pl.num_programs(2) - 1)
        def _():
            for p in pairs:
                s = s_ref[p]
                sout_ref[2 * p] = s[:HEAD_DIM, :HEAD_DIM]
                sout_ref[2 * p + 1] = s[HEAD_DIM:, HEAD_DIM:]


def _wkv_call(proj, lp, l, s0, o_prev, row0, n_seq, seq_len, has_state_out):
    n_groups = seq_len // WKV_ROWS
    blk0 = row0 // WKV_ROWS
    has_init = s0 is not None
    has_alias = o_prev is not None

    def group(d, s, g):
        return s * n_groups + g + d * (n_groups - 1 - 2 * g)

    def col(cb):
        return lambda d, s, g: (blk0 + group(d, s, g), cb)

    per_dir_mat = pl.BlockSpec((None, None, LORA_W, D_RWKV), lambda d, s, g: (l, d, 0, 0))
    per_dir_vec = pl.BlockSpec((None, None, 1, D_RWKV), lambda d, s, g: (l, d, 0, 0))
    shared = pl.BlockSpec((None, 1, D_RWKV), lambda d, s, g: (l, 0, 0))
    in_specs = [
        pl.BlockSpec((WKV_ROWS, D_RWKV), col(0)),
        pl.BlockSpec((WKV_ROWS, D_RWKV), col(1)),
        pl.BlockSpec((WKV_ROWS, D_RWKV), col(2)),
        pl.BlockSpec((WKV_ROWS, TN_IN), col(LO_BLOCK)),
        per_dir_mat, per_dir_vec, per_dir_mat, per_dir_vec, shared, shared,
    ]
    args = [proj, proj, proj, proj, lp["w_up"], lp["w0"], lp["a_up"], lp["a0"], lp["k_k"], lp["k_a"]]
    if has_init:
        in_specs.append(pl.BlockSpec((None, None, None, N_HEADS, HEAD_DIM, HEAD_DIM),
                                     lambda d, s, g: (s, l, d, 0, 0, 0)))
        args.append(s0)
    aliases = {}
    if has_alias:
        aliases = {len(args): 0}
        in_specs.append(pl.BlockSpec(memory_space=pl.ANY))
        args.append(o_prev)
    out_shape = [jax.ShapeDtypeStruct((2, proj.shape[0], D_RWKV), F32)]
    out_specs = [pl.BlockSpec((1, WKV_ROWS, D_RWKV), lambda d, s, g: (d, blk0 + group(d, s, g), 0))]
    if has_state_out:
        out_shape.append(jax.ShapeDtypeStruct((n_seq, 2, N_HEADS, HEAD_DIM, HEAD_DIM), F32))
        out_specs.append(pl.BlockSpec((None, None, N_HEADS, HEAD_DIM, HEAD_DIM), lambda d, s, g: (s, d, 0, 0, 0)))
    kern = functools.partial(_wkv_kernel, has_init=has_init, has_state_out=has_state_out, has_alias=has_alias)
    per_chunk = (WKV_GROUP, N_PAIRS)
    return pl.pallas_call(
        kern,
        out_shape=out_shape,
        grid=(2, n_seq, n_groups),
        in_specs=in_specs,
        out_specs=out_specs,
        input_output_aliases=aliases,
        scratch_shapes=[
            pltpu.VMEM((N_PAIRS, PAIR_W, PAIR_W), F32),
            pltpu.VMEM((WKV_ROWS, D_RWKV), F32),
            pltpu.VMEM((WKV_ROWS, D_RWKV), F32),
            pltpu.VMEM((WKV_ROWS, D_RWKV), F32),
            pltpu.VMEM((WKV_ROWS, D_RWKV), F32),
            pltpu.VMEM((WKV_ROWS, D_RWKV), F32),
            pltpu.VMEM((4, PAIR_W, PAIR_W), F32),
            pltpu.VMEM(per_chunk + (PAIR_W, PAIR_W), BF16),
            pltpu.VMEM(per_chunk + (PAIR_W, PAIR_W), F32),
            pltpu.VMEM(per_chunk + (PAIR_W, PAIR_W), BF16),
            pltpu.VMEM(per_chunk + (CHUNK, PAIR_W), BF16),
            pltpu.VMEM(per_chunk + (PAIR_W, PAIR_W), BF16),
            pltpu.VMEM(per_chunk + (PAIR_W, PAIR_W), F32),
            pltpu.VMEM(per_chunk + (CHUNK, PAIR_W), F32),
            pltpu.VMEM(per_chunk + (8, PAIR_W), F32),
        ],
        compiler_params=_cparams(("parallel", "parallel", "arbitrary")),
        name="wkv",
    )(*args)


def _rwkvpost_kernel(o_ref, r_ref, k_ref, v_ref, lo_ref, aup_ref, a0_ref, ka_ref, rk_ref, lng_ref, lnb_ref,
                     gup_ref, out_ref):
    lo = lo_ref[...]
    xa = lo[:, LORA_W:LORA_W + LORA_A]
    sg = _sigmoid(lo[:, LORA_W + LORA_A:LORA_W + LORA_A + LORA_G_PAD])
    gi = lax.broadcasted_iota(jnp.int32, (PAIR_W, PAIR_W), 0)
    gj = lax.broadcasted_iota(jnp.int32, (PAIR_W, PAIR_W), 1)
    head_sum = jnp.where((gi >= HEAD_DIM) == (gj >= HEAD_DIM), 1.0, 0.0).astype(BF16)
    cols = [slice(p * PAIR_W, (p + 1) * PAIR_W) for p in range(N_PAIRS)]
    gate = _dot(sg, gup_ref[...])
    ic = _sigmoid(a0_ref[0] + _dot(xa, aup_ref[0])) + _sigmoid(a0_ref[1] + _dot(xa, aup_ref[1]))
    kd_sum = k_ref[...] * (2.0 + (ic - 2.0) * ka_ref[...])
    bterm = r_ref[...] * kd_sum * rk_ref[...]
    o = [o_ref[0, :, c] + o_ref[1, :, c] for c in cols]
    mu = [_dot2(x, head_sum) * (1.0 / HEAD_DIM) for x in o]
    oc = [x - m for x, m in zip(o, mu)]
    var = [_dot(x * x, head_sum) * (1.0 / HEAD_DIM) for x in oc]
    bonus = [_dot(bterm[:, c], head_sum) * v_ref[:, c] for c in cols]
    for c, x, vr, bn in zip(cols, oc, var, bonus):
        y = x * lax.rsqrt(vr + GN_EPS) * lng_ref[:, c] + lnb_ref[:, c]
        out_ref[:, c] = ((y + bn) * gate[:, c]).astype(BF16)


def _rwkvpost_call(o, proj, lp, l):
    n_rows = o.shape[1]

    def col(cb):
        return lambda i: (i, cb)

    shared = pl.BlockSpec((None, 1, D_RWKV), lambda i: (l, 0, 0))
    return pl.pallas_call(
        _rwkvpost_kernel,
        out_shape=jax.ShapeDtypeStruct((n_rows, D_RWKV), BF16),
        grid=(n_rows // TM_POST,),
        in_specs=[
            pl.BlockSpec((2, TM_POST, D_RWKV), lambda i: (0, i, 0)),
            pl.BlockSpec((TM_POST, D_RWKV), col(0)),
            pl.BlockSpec((TM_POST, D_RWKV), col(1)),
            pl.BlockSpec((TM_POST, D_RWKV), col(2)),
            pl.BlockSpec((TM_POST, TN_IN), col(LO_BLOCK)),
            pl.BlockSpec((None, 2, LORA_A, D_RWKV), lambda i: (l, 0, 0, 0)),
            pl.BlockSpec((None, 2, 1, D_RWKV), lambda i: (l, 0, 0, 0)),
            shared, shared, shared, shared,
            pl.BlockSpec((None, LORA_G_PAD, D_RWKV), lambda i: (l, 0, 0)),
        ],
        out_specs=pl.BlockSpec((TM_POST, D_RWKV), lambda i: (i, 0)),
        compiler_params=_cparams(("parallel",)),
        name="rwkvpost",
    )(o, proj, proj, proj, proj, lp["a_up"], lp["a0"], lp["k_a"], lp["r_k"], lp["ln_g"], lp["ln_b"], lp["g_up"])


def _dft_mats(L):
    idx = (np.arange(L)[:, None] * np.arange(L)[None, :]) % (2 * L)
    ang = np.pi * idx.astype(np.float64) / L
    alt = np.cos(np.pi * np.arange(L))
    fc = np.cos(ang)
    fs = -np.sin(ang)
    fs[0, :] = alt
    fwd = np.concatenate([fc, fs], axis=0)
    ic = 2.0 * np.cos(ang.T)
    ic[:, 0] = 1.0
    isn = -2.0 * np.sin(ang.T)
    isn[:, 0] = alt
    inv = np.concatenate([ic, isn], axis=1) / (2 * L)
    return fwd.astype(np.float32), inv.astype(np.float32)


def _filter_features(L):
    t = np.linspace(0.0, 1.0, L, dtype=np.float32)[:, None]
    w = (2.0 * math.pi / L) * np.arange(L, dtype=np.float32)[:, None]
    f = np.linspace(1e-4, FILT_BANDS - 1, FILT_BANDS, dtype=np.float32)[None, :]
    z = np.concatenate([t, np.cos(f * w), -np.sin(f * w)], axis=-1)
    zp = np.zeros((L, FILT_EMB_PAD), np.float32)
    zp[:, :FILT_EMB] = z
    return zp, t


def _filter_deltas():
    max_decay = math.log(FILT_TARGET) / FAST_DECAY_PCT
    min_decay = math.log(FILT_TARGET) / SLOW_DECAY_PCT
    return np.abs(np.linspace(min_decay, max_decay, D_HYENA, dtype=np.float32))[None, :]


def _hyfilt_kernel(z_ref, t_ref, dl_ref, fwd_ref, w1_ref, b1_ref, w2_ref, b2_ref,
                   w3f0_ref, w3b0_ref, w3f1_ref, w3b1_ref, fr_ref, o_ref):
    L = z_ref.shape[0]
    tc = o_ref.shape[3]
    h = jnp.sin(fr_ref[0:1, :] * (_dot3(z_ref[...], w1_ref[...]) + b1_ref[...]))
    h = jnp.sin(fr_ref[1:2, :] * (_dot3(h, w2_ref[...]) + b2_ref[...]))
    decay = jnp.exp(-t_ref[...] * dl_ref[...])
    first = lax.broadcasted_iota(jnp.int32, (L, tc), 0) == 0
    fwd = fwd_ref[...]
    for order, (wf_ref, wb_ref) in enumerate(((w3f0_ref, w3b0_ref), (w3f1_ref, w3b1_ref))):
        hf = _dot3(h, wf_ref[...]) * decay
        hb = _dot3(h, wb_ref[...]) * decay
        norm = jnp.sum(jnp.abs(hf) + jnp.abs(hb), axis=0, keepdims=True)
        hf = hf / norm
        hb = jnp.where(first, 0.0, hb / norm)
        ks = _dot(fwd, hf + hb)
        kd = _dot(fwd, hf - hb)
        kr = ks[:L]
        o_ref[order, 0] = kr
        o_ref[order, 1] = jnp.where(first, 0.0, kd[L:])
        o_ref[order, 2] = jnp.where(first, ks[L:L + 1], kr)


def _hyfilt_call(L, fp, l):
    z, t = _filter_features(L)
    fwd, _ = _dft_mats(L)
    nct = D_HYENA // TC_FILT
    full = lambda shape: pl.BlockSpec(shape, lambda j: tuple(0 for _ in shape))
    layer = lambda shape: pl.BlockSpec((None,) + shape, lambda j: (l,) + tuple(0 for _ in shape))
    w3_spec = lambda grp: pl.BlockSpec((None, FILT_HIDDEN, TC_FILT), lambda j: (l, 0, grp * nct + j))
    w3 = fp["w3"]
    return pl.pallas_call(
        _hyfilt_kernel,
        out_shape=jax.ShapeDtypeStruct((2, 3, L, D_HYENA), F32),
        grid=(nct,),
        in_specs=[
            full((L, FILT_EMB_PAD)), full((L, 1)), pl.BlockSpec((1, TC_FILT), lambda j: (0, j)), full((2 * L, L)),
            layer((FILT_EMB_PAD, FILT_HIDDEN)), layer((1, FILT_HIDDEN)),
            layer((FILT_HIDDEN, FILT_HIDDEN)), layer((1, FILT_HIDDEN)),
            w3_spec(0), w3_spec(1), w3_spec(2), w3_spec(3),
            layer((2, FILT_HIDDEN)),
        ],
        out_specs=pl.BlockSpec((2, 3, L, TC_FILT), lambda j: (0, 0, 0, j)),
        compiler_params=_cparams(("parallel",)),
        name="hyfilt",
    )(jnp.asarray(z), jnp.asarray(t), jnp.asarray(_filter_deltas()), jnp.asarray(fwd).astype(BF16),
      fp["w1"], fp["b1"], fp["w2"], fp["b2"], w3, w3, w3, w3, fp["freq"])


def _hyena_kernel(u_ref, x1_ref, x2_ref, fwd_ref, inv_ref, spec_ref, bias_ref, *rest):
    out_ref = rest[-1]
    L = fwd_ref.shape[1]
    n_sub = u_ref.shape[0] // L

    def side_by_side(ref):
        return jnp.concatenate([ref[s * L:(s + 1) * L, :] for s in range(n_sub)], axis=1)

    def tiled(x):
        return jnp.concatenate([x] * n_sub, axis=1)

    def long_conv(u, order):
        spec = _dot(fwd_ref[...], u)
        ur, ui = spec[:L], spec[L:]
        kr, ki, kr2 = tiled(spec_ref[order, 0]), tiled(spec_ref[order, 1]), tiled(spec_ref[order, 2])
        yr = ur * kr - ui * ki
        yi = ur * ki + ui * kr2
        y = _dot(inv_ref[...], jnp.concatenate([yr, yi], axis=0))
        return y + u * tiled(bias_ref[order:order + 1, :])

    z = side_by_side(x1_ref) * long_conv(side_by_side(u_ref), 0)
    out = (side_by_side(x2_ref) * long_conv(z, 1)).astype(BF16)
    tc = out_ref.shape[1]
    for s in range(n_sub):
        out_ref[s * L:(s + 1) * L, :] = out[:, s * tc:(s + 1) * tc]


def _hyena_call(proj, spec, bias, l, out_prev, row0, n_seq, L, tc, n_sub):
    fwd, inv = _dft_mats(L)
    nct = D_HYENA // tc
    rows = n_sub * L
    blk0 = row0 // rows
    cb0 = 3 * D_RWKV // tc

    def col(which):
        return lambda s, j: (blk0 + s, cb0 + which * nct + j)

    in_specs = [
        pl.BlockSpec((rows, tc), col(0)),
        pl.BlockSpec((rows, tc), col(1)),
        pl.BlockSpec((rows, tc), col(2)),
        pl.BlockSpec((2 * L, L), lambda s, j: (0, 0), pipeline_mode=pl.Buffered(1)),
        pl.BlockSpec((L, 2 * L), lambda s, j: (0, 0), pipeline_mode=pl.Buffered(1)),
        pl.BlockSpec((2, 3, L, tc), lambda s, j: (0, 0, 0, j)),
        pl.BlockSpec((None, 2, tc), lambda s, j: (l, 0, j)),
    ]
    args = [proj, proj, proj, jnp.asarray(fwd).astype(BF16), jnp.asarray(inv).astype(BF16), spec, bias]
    aliases = {}
    if out_prev is not None:
        aliases = {len(args): 0}
        in_specs.append(pl.BlockSpec(memory_space=pl.ANY))
        args.append(out_prev)
    return pl.pallas_call(
        _hyena_kernel,
        out_shape=jax.ShapeDtypeStruct((proj.shape[0], D_HYENA), BF16),
        grid=(n_seq // n_sub, nct),
        in_specs=in_specs,
        out_specs=pl.BlockSpec((rows, tc), lambda s, j: (blk0 + s, j)),
        input_output_aliases=aliases,
        compiler_params=_cparams(("parallel", "parallel")),
        name="hyena",
    )(*args)


def _outproj_kernel(a_ref, b_ref, x_ref, g_ref, w_ref, lg_ref, lb_ref, o_ref):
    mix = (jnp.dot(a_ref[...], w_ref[0:D_RWKV, :], preferred_element_type=F32)
           + jnp.dot(b_ref[...], w_ref[D_RWKV:, :], preferred_element_type=F32))
    y = ALPHA * x_ref[...] + g_ref[0, 0] * mix
    o_ref[...] = _layer_norm_rows(y) * lg_ref[...] + lb_ref[...]


def _outproj_call(a_out, b_out, x, mod_all, w_out, ln_g, ln_b, l, n_ctx_units):
    grp = lambda i: _tile_group(i, TM_OUT, n_ctx_units)
    vec = pl.BlockSpec((None, 1, D_MODEL), lambda i: (l, 0, 0))
    return pl.pallas_call(
        _outproj_kernel,
        out_shape=jax.ShapeDtypeStruct(x.shape, F32),
        grid=(x.shape[0] // TM_OUT,),
        in_specs=[
            pl.BlockSpec((TM_OUT, D_RWKV), lambda i: (i, 0)),
            pl.BlockSpec((TM_OUT, D_HYENA), lambda i: (i, 0)),
            pl.BlockSpec((TM_OUT, D_MODEL), lambda i: (i, 0)),
            _mod_spec(l, 2, grp),
            pl.BlockSpec((None, D_MODEL, D_MODEL), lambda i: (l, 0, 0)),
            vec, vec,
        ],
        out_specs=pl.BlockSpec((TM_OUT, D_MODEL), lambda i: (i, 0)),
        compiler_params=_cparams(("parallel",)),
        name="outproj",
    )(a_out, b_out, x, mod_all, w_out, ln_g, ln_b)


def _mlp_kernel(x_ref, sh_ref, sc_ref, g_ref, w1_ref, w2_ref, lg_ref, lb_ref, o_ref, h_ref):
    f = pl.program_id(1)
    row_blocks = o_ref.shape[0] // 128

    @pl.when(f == 0)
    def _():
        sc = 1.0 + sc_ref[0, 0]
        sh = sh_ref[0, 0]

        def body(rb, carry):
            rows = pl.ds(pl.multiple_of(rb * 128, 128), 128)
            h_ref[rows, :] = (_layer_norm_rows(x_ref[rows, :]) * sc + sh).astype(BF16)
            return carry

        lax.fori_loop(0, row_blocks, body, 0)
        o_ref[...] = jnp.zeros(o_ref.shape, F32)

    hid = jnp.maximum(jnp.dot(h_ref[...], w1_ref[...].astype(BF16), preferred_element_type=F32), 0.0)
    hid = (hid * hid).astype(BF16)
    o_ref[...] += jnp.dot(hid, w2_ref[...].astype(BF16), preferred_element_type=F32)

    @pl.when(f == pl.num_programs(1) - 1)
    def _():
        gate = g_ref[0, 0]
        lg = lg_ref[...]
        lb = lb_ref[...]

        def body(rb, carry):
            rows = pl.ds(pl.multiple_of(rb * 128, 128), 128)
            y = ALPHA * x_ref[rows, :] + gate * o_ref[rows, :]
            o_ref[rows, :] = _layer_norm_rows(y) * lg + lb
            return carry

        lax.fori_loop(0, row_blocks, body, 0)


def _mlp_call(x, mod_all, w1, w2, ln_g, ln_b, l, n_ctx_units):
    grp = lambda i: _tile_group(i, TM_MLP, n_ctx_units)
    vec = pl.BlockSpec((None, 1, D_MODEL), lambda i, f: (l, 0, 0))
    return pl.pallas_call(
        _mlp_kernel,
        out_shape=jax.ShapeDtypeStruct(x.shape, F32),
        grid=(x.shape[0] // TM_MLP, D_FF // TF_MLP),
        in_specs=[
            pl.BlockSpec((TM_MLP, D_MODEL), lambda i, f: (i, 0), pipeline_mode=pl.Buffered(1)),
            _mod_spec(l, 3, grp),
            _mod_spec(l, 4, grp),
            _mod_spec(l, 5, grp),
            pl.BlockSpec((None, D_MODEL, TF_MLP), lambda i, f: (l, 0, f)),
            pl.BlockSpec((None, TF_MLP, D_MODEL), lambda i, f: (l, f, 0)),
            vec, vec,
        ],
        out_specs=pl.BlockSpec((TM_MLP, D_MODEL), lambda i, f: (i, 0)),
        scratch_shapes=[pltpu.VMEM((TM_MLP, D_MODEL), BF16)],
        compiler_params=_cparams(("parallel", "arbitrary")),
        name="mlp",
    )(x, mod_all, mod_all, mod_all, w1, w2, ln_g, ln_b)


def kernel(x_prompt, x_sample, c, state_rwkv, c_ctx, w_ada, b_ada, w_in, conv_w, lora_w_up, lora_w0, lora_a_up, lora_a0, lora_g_up, rwkv_k_k, rwkv_k_a, rwkv_r_k, rwkv_ln_g, rwkv_ln_b, filt_w1, filt_b1, filt_w2, filt_b2, filt_w3, filt_freq, hyena_bias, w_out, ln1_g, ln1_b, ln2_g, ln2_b, mlp_w1, mlp_w2):
    n_ctx, ctx_seq, _ = x_prompt.shape
    n_lat, lat_seq, _ = x_sample.shape
    depth = w_ada.shape[0]
    assert ctx_seq == CTX_SEQ and lat_seq == LAT_SEQ and (n_ctx * ctx_seq) % TM == 0
    assert 1 + n_lat <= N_COND
    ctx_rows = n_ctx * ctx_seq
    lat_rows = n_lat * lat_seq
    n_ctx_units = ctx_rows // TM

    x = jnp.concatenate([x_prompt.reshape(ctx_rows, D_MODEL), x_sample.reshape(lat_rows, D_MODEL)], axis=0)
    cond = jnp.concatenate([c_ctx[None, :], c, jnp.zeros((N_COND - 1 - n_lat, D_MODEL), F32)], axis=0)
    mod_all = _mod_call(cond, w_ada, b_ada).reshape(depth, N_COND, 6, 1, D_MODEL)

    w_lo_bf = jnp.pad(w_in[:, :, N_CONV:].astype(BF16), ((0, 0), (0, 0), (0, TN_IN - (D_IN - N_CONV))))
    conv_w9 = conv_w.reshape(depth, 9, N_CONV)
    w_out_bf = w_out.astype(BF16)
    row = lambda a: a.reshape(depth, 1, a.shape[-1])
    lp = {
        "w_up": lora_w_up, "w0": lora_w0[:, :, None, :], "a_up": lora_a_up, "a0": lora_a0[:, :, None, :],
        "k_k": row(rwkv_k_k), "k_a": row(rwkv_k_a), "r_k": rwkv_r_k.reshape(depth, 1, D_RWKV),
        "ln_g": row(rwkv_ln_g), "ln_b": row(rwkv_ln_b),
        "g_up": jnp.pad(lora_g_up, ((0, 0), (0, LORA_G_PAD - LORA_G), (0, 0))),
    }
    fp = {
        "w1": jnp.pad(filt_w1, ((0, 0), (0, FILT_EMB_PAD - FILT_EMB), (0, 0))), "b1": row(filt_b1),
        "w2": filt_w2, "b2": row(filt_b2), "w3": filt_w3, "freq": filt_freq,
    }
    ln1_g, ln1_b, ln2_g, ln2_b = row(ln1_g), row(ln1_b), row(ln2_g), row(ln2_b)
    state_in = state_rwkv.astype(F32)

    ctx_states = []
    for l in range(depth):
        proj = _inproj_call(x, mod_all, w_in, w_lo_bf, conv_w9, l, n_ctx_units)

        o, s_ctx = _wkv_call(proj, lp, l, None, None, 0, n_ctx, ctx_seq, True)
        (o,) = _wkv_call(proj, lp, l, state_in, o, ctx_rows, n_lat, lat_seq, False)
        a_out = _rwkvpost_call(o, proj, lp, l)
        ctx_states.append(s_ctx)

        spec_ctx = _hyfilt_call(ctx_seq, fp, l)
        spec_lat = _hyfilt_call(lat_seq, fp, l)
        b_out = _hyena_call(proj, spec_ctx, hyena_bias, l, None, 0, n_ctx, ctx_seq, D_HYENA, 1)
        lat_sub = 2 if n_lat % 2 == 0 and ctx_rows % (2 * lat_seq) == 0 else 1
        b_out = _hyena_call(proj, spec_lat, hyena_bias, l, b_out, ctx_rows, n_lat, lat_seq, 256, lat_sub)

        x = _outproj_call(a_out, b_out, x, mod_all, w_out_bf, ln1_g, ln1_b, l, n_ctx_units)
        x = _mlp_call(x, mod_all, mlp_w1, mlp_w2, ln2_g, ln2_b, l, n_ctx_units)

    y_prompt = x[:ctx_rows].reshape(n_ctx, ctx_seq, D_MODEL)
    y_sample = x[ctx_rows:].reshape(n_lat, lat_seq, D_MODEL)
    new_state = jnp.stack(ctx_states, axis=1).astype(x_prompt.dtype)
    return (y_prompt, y_sample, new_state)
```

```python
import functools
import math

import jax
import jax.numpy as jnp
import numpy as np
from jax import lax
from jax.experimental import pallas as pl
from jax.experimental.pallas import tpu as pltpu

F32 = jnp.float32
BF16 = jnp.bfloat16

D_MODEL = 2048
D_RWKV = 1024
D_HYENA = 1024
HEAD_DIM = 64
N_HEADS = D_RWKV // HEAD_DIM
N_PAIRS = N_HEADS // 2
PAIR_W = 2 * HEAD_DIM
LORA_W = 64
LORA_A = 64
LORA_G = 160
LORA_G_PAD = 256
N_CONV = 3 * D_RWKV + 3 * D_HYENA
D_IN = N_CONV + LORA_W + LORA_A + LORA_G
D_FF = 4 * D_MODEL
GRID_W = 64
CTX_SEQ = 256
LAT_SEQ = 1024
FILT_BANDS = 16
FILT_EMB = 1 + 2 * FILT_BANDS
FILT_EMB_PAD = 128
FILT_HIDDEN = 64
DEPTH = 2
ALPHA = (2 * DEPTH) ** 0.25
LN_EPS = 1e-5
GN_EPS = 64e-5
FILT_TARGET = 1e-2
FAST_DECAY_PCT = 0.3
SLOW_DECAY_PCT = 1.5
DECAY_SCALE = math.exp(-0.5)

TM = 1024
TN_IN = 512
NJ_CONV = N_CONV // TN_IN
D_IN_PAD = (NJ_CONV + 1) * TN_IN
LO_BLOCK = N_CONV // TN_IN
CONV_PAD = 72
CHUNK = 64
WKV_GROUP = 4
WKV_ROWS = WKV_GROUP * CHUNK
N_COND = 8
TN_MOD = 1024
TM_OUT = 512
TM_MLP = 1024
TF_MLP = 512
TM_POST = 256
TC_FILT = 256
VMEM_LIMIT = 56 * 1024 * 1024


def _cparams(sem):
    return pltpu.CompilerParams(dimension_semantics=sem, vmem_limit_bytes=VMEM_LIMIT)


def _dot(a, b):
    return jnp.dot(a.astype(BF16), b.astype(BF16), preferred_element_type=F32)


def _dot_nt(a, b):
    return lax.dot_general(a.astype(BF16), b.astype(BF16), (((1,), (1,)), ((), ())),
                           preferred_element_type=F32)


def _dot_tn(a, b):
    return lax.dot_general(a.astype(BF16), b.astype(BF16), (((0,), (0,)), ((), ())),
                           preferred_element_type=F32)


def _split(x):
    hi = x.astype(BF16)
    lo = (x - hi.astype(F32)).astype(BF16)
    return hi, lo


def _dot3(a, b):
    ah, al = _split(a)
    bh, bl = _split(b)
    return _dot(ah, bh) + (_dot(ah, bl) + _dot(al, bh))


def _dot2(a, b_exact):
    ah, al = _split(a)
    return _dot(ah, b_exact) + _dot(al, b_exact)


def _dot2_left(a_exact, b):
    bh, bl = _split(b)
    return _dot(a_exact, bh) + _dot(a_exact, bl)


def _sigmoid(x):
    return 1.0 / (1.0 + jnp.exp(-x))


def _layer_norm_rows(x):
    mu = jnp.mean(x, axis=-1, keepdims=True)
    xc = x - mu
    var = jnp.mean(xc * xc, axis=-1, keepdims=True)
    return xc * lax.rsqrt(var + LN_EPS)


def _unit_group(i, n_ctx_units):
    return jnp.maximum(i - n_ctx_units + 1, 0)


def _tile_group(i, tm, n_ctx_units):
    return _unit_group(i // (TM // tm), n_ctx_units)


def _mod_spec(l, which, grp):
    return pl.BlockSpec((None, 1, 1, 1, D_MODEL), lambda i, *_: (l, grp(i), which, 0, 0))


def _mod_kernel(c_ref, w_ref, b_ref, o_ref):
    c = c_ref[...]
    s = c * _sigmoid(c)
    o_ref[0] = _dot(s, w_ref[0]) + b_ref[0]


def _mod_call(cond, w_ada, b_ada):
    depth = w_ada.shape[0]
    n_out = w_ada.shape[2]
    return pl.pallas_call(
        _mod_kernel,
        out_shape=jax.ShapeDtypeStruct((depth, N_COND, n_out), F32),
        grid=(depth, n_out // TN_MOD),
        in_specs=[
            pl.BlockSpec((N_COND, D_MODEL), lambda l, j: (0, 0)),
            pl.BlockSpec((1, D_MODEL, TN_MOD), lambda l, j: (l, 0, j)),
            pl.BlockSpec((1, 1, TN_MOD), lambda l, j: (l, 0, j)),
        ],
        out_specs=pl.BlockSpec((1, N_COND, TN_MOD), lambda l, j: (l, 0, j)),
        compiler_params=_cparams(("parallel", "parallel")),
        name="mod",
    )(cond, w_ada, b_ada.reshape(depth, 1, n_out))


def _inproj_kernel(x_ref, sh_ref, sc_ref, w_ref, wlo_ref, cw_ref, o_ref, h_ref, wbf_ref, acc_a, acc_b, *,
                   n_ctx_units):
    i = pl.program_id(0)
    j = pl.program_id(1)
    tn = o_ref.shape[1]
    is_grid = i >= n_ctx_units
    mid = slice(CONV_PAD, CONV_PAD + TM)

    n_blocks = TM // GRID_W
    n_parts = 4
    lane_w = 128
    n_chunk = 128

    def round_weights():
        for nc in range(tn // n_chunk):
            rows = slice(nc * n_chunk, (nc + 1) * n_chunk)
            wbf_ref[rows, :] = w_ref[rows, :].astype(BF16)

    def matmul_into(dst_ref, part=None):
        if part is None:
            dst_ref[mid, :] = _dot_nt(h_ref[...], wbf_ref[...])
        else:
            r0, r1 = part * (TM // n_parts), (part + 1) * (TM // n_parts)
            dst_ref[CONV_PAD + r0:CONV_PAD + r1, :] = _dot_nt(h_ref[r0:r1, :], wbf_ref[...])

    def conv_from(src_ref, g0=0, g1=n_blocks):
        row = lax.broadcasted_iota(jnp.int32, (GRID_W, lane_w), 0)
        first = row == 0
        last = row == GRID_W - 1
        first_grid = jnp.logical_and(first, is_grid)
        last_grid = jnp.logical_and(last, is_grid)
        tap_row = lax.broadcasted_iota(jnp.int32, (9, lane_w), 0)
        centre = jnp.logical_and(tap_row >= 3, tap_row < 6)
        keep_tap = jnp.logical_or(centre, is_grid)
        per_seq = CTX_SEQ // GRID_W
        for c in range(tn // lane_w):
            lanes = slice(c * lane_w, (c + 1) * lane_w)
            cw = jnp.where(keep_tap, cw_ref[:, lanes], 0.0)
            partial = {}
            for s in range(max(g0 - 1, 0), min(g1 + 1, n_blocks)):
                start = CONV_PAD + s * GRID_W
                uc = src_ref[start:start + GRID_W, lanes]
                ul = src_ref[start - 1:start - 1 + GRID_W, lanes]
                ur = src_ref[start + 1:start + 1 + GRID_W, lanes]
                ul = jnp.where(first if s % per_seq == 0 else first_grid, 0.0, ul)
                ur = jnp.where(last if s % per_seq == per_seq - 1 else last_grid, 0.0, ur)
                for a in range(3):
                    gi = s - (a - 1)
                    if g0 <= gi < g1:
                        t = (ul * cw[3 * a:3 * a + 1, :] + uc * cw[3 * a + 1:3 * a + 2, :]
                             + ur * cw[3 * a + 2:3 * a + 3, :])
                        partial[gi] = t if gi not in partial else partial[gi] + t
                if s - 1 in partial:
                    o_ref[(s - 1) * GRID_W:s * GRID_W, lanes] = partial.pop(s - 1)
            for gi in sorted(partial):
                o_ref[gi * GRID_W:(gi + 1) * GRID_W, lanes] = partial.pop(gi)

    @pl.when(j == 0)
    def _():
        sc = 1.0 + sc_ref[0, 0]
        sh = sh_ref[0, 0]

        def body(rb, carry):
            rows = pl.ds(pl.multiple_of(rb * 128, 128), 128)
            h = _layer_norm_rows(x_ref[rows, :]) * sc + sh
            h_ref[rows, :] = h.astype(BF16)
            return carry

        lax.fori_loop(0, TM // 128, body, 0)
        for acc in (acc_a, acc_b):
            acc[0:CONV_PAD, :] = jnp.zeros((CONV_PAD, tn), F32)
            acc[CONV_PAD + TM:, :] = jnp.zeros((CONV_PAD, tn), F32)
        o_ref[...] = jnp.dot(h_ref[...], wlo_ref[...], preferred_element_type=F32)

    @pl.when(j == 1)
    def _():
        round_weights()
        matmul_into(acc_a)

    main = jnp.logical_and(j >= 2, j <= NJ_CONV)
    odd = jnp.bitwise_and(j, 1) == 1

    def overlapped(src_ref, dst_ref):
        round_weights()
        per_part = n_blocks // n_parts
        for part in range(n_parts):
            matmul_into(dst_ref, part)
            conv_from(src_ref, part * per_part, (part + 1) * per_part)

    @pl.when(jnp.logical_and(main, jnp.logical_not(odd)))
    def _():
        overlapped(acc_a, acc_b)

    @pl.when(jnp.logical_and(main, odd))
    def _():
        overlapped(acc_b, acc_a)

    @pl.when(j == NJ_CONV + 1)
    def _():
        conv_from(acc_b if NJ_CONV % 2 == 0 else acc_a)


def _inproj_call(x, mod_all, w_in, w_lo, conv_w, l, n_ctx_units):
    n_units = x.shape[0] // TM
    kern = functools.partial(_inproj_kernel, n_ctx_units=n_ctx_units)
    grp = lambda i: _unit_group(i, n_ctx_units)
    tile = lambda t: jnp.clip(t, 0, NJ_CONV - 1)
    return pl.pallas_call(
        kern,
        out_shape=jax.ShapeDtypeStruct((x.shape[0], D_IN_PAD), F32),
        grid=(n_units, NJ_CONV + 2),
        in_specs=[
            pl.BlockSpec((TM, D_MODEL), lambda i, j: (i, 0)),
            _mod_spec(l, 0, grp),
            _mod_spec(l, 1, grp),
            pl.BlockSpec((None, TN_IN, D_MODEL), lambda i, j: (l, tile(j - 1), 0)),
            pl.BlockSpec((None, D_MODEL, TN_IN), lambda i, j: (l, 0, 0)),
            pl.BlockSpec((None, 9, TN_IN), lambda i, j: (l, 0, tile(j - 2))),
        ],
        out_specs=pl.BlockSpec((TM, TN_IN), lambda i, j: (i, jnp.where(j == 0, LO_BLOCK, tile(j - 2)))),
        scratch_shapes=[
            pltpu.VMEM((TM, D_MODEL), BF16),
            pltpu.VMEM((TN_IN, D_MODEL), BF16),
            pltpu.VMEM((TM + 2 * CONV_PAD, TN_IN), F32),
            pltpu.VMEM((TM + 2 * CONV_PAD, TN_IN), F32),
        ],
        compiler_params=_cparams(("parallel", "arbitrary")),
        name="inproj",
    )(x, mod_all, mod_all, w_in, w_lo, conv_w)


def _stack_heads(z, head0):
    return jnp.concatenate([jnp.where(head0, z, 0.0), jnp.where(head0, 0.0, z)], axis=0)


def _unstack_heads(z):
    c = z.shape[0] // 2
    return z[:c] + z[c:]


def _wkv_kernel(*refs, has_init, has_state_out, has_alias):
    (r_ref, k_ref, v_ref, lo_ref, wup_ref, w0_ref, aup_ref, a0_ref, kk_ref, ka_ref) = refs[:10]
    pos = 10
    s0_ref = None
    if has_init:
        s0_ref = refs[pos]
        pos += 1
    if has_alias:
        pos += 1
    o_ref = refs[pos]
    pos += 1
    sout_ref = None
    if has_state_out:
        sout_ref = refs[pos]
        pos += 1
    (s_ref, lw_ref, cum_ref, kkn_ref, b_ref, kd_ref, msk_ref,
     ahat_ref, yhat_ref, mrb_ref, r0_ref, b2s_ref, kv_ref, mv_ref, dec_ref) = refs[pos:]

    d = pl.program_id(0)
    g = pl.program_id(2)
    C = CHUNK
    pairs = range(N_PAIRS)
    cols = [slice(p * PAIR_W, (p + 1) * PAIR_W) for p in pairs]

    @pl.when(g == 0)
    def _():
        if has_init:
            zero = jnp.zeros((HEAD_DIM, HEAD_DIM), F32)
            for p in pairs:
                top = jnp.concatenate([s0_ref[2 * p], zero], axis=1)
                bottom = jnp.concatenate([zero, s0_ref[2 * p + 1]], axis=1)
                s_ref[p] = jnp.concatenate([top, bottom], axis=0)
        else:
            s_ref[...] = jnp.zeros(s_ref.shape, F32)

    sgn = 1 - 2 * d
    si = lax.broadcasted_iota(jnp.int32, (2 * C, 2 * C), 0)
    sj = lax.broadcasted_iota(jnp.int32, (2 * C, 2 * C), 1)
    same_head = (si >= C) == (sj >= C)
    dlt = (jnp.bitwise_and(si, C - 1) - jnp.bitwise_and(sj, C - 1)) * sgn
    msk_ref[0] = jnp.where(jnp.logical_and(same_head, dlt < 0), 1.0, 0.0)
    msk_ref[1] = jnp.where(jnp.logical_and(same_head, dlt >= 0), 1.0, 0.0)
    msk_ref[2] = jnp.where(si == sj, 1.0, 0.0)
    blockdiag = (si >= HEAD_DIM) == (sj >= HEAD_DIM)
    msk_ref[3] = jnp.where(blockdiag, 1.0, 0.0)
    head_sum = jnp.where(blockdiag, 1.0, 0.0).astype(BF16)

    lo = lo_ref[...]
    xw = lo[:, 0:LORA_W]
    xa = lo[:, LORA_W:LORA_W + LORA_A]
    logw = -DECAY_SCALE * _sigmoid(w0_ref[...] + _dot3(jnp.tanh(xw), wup_ref[...]))
    lw_ref[...] = logw
    gi = lax.broadcasted_iota(jnp.int32, (WKV_ROWS, WKV_ROWS), 0)
    gj = lax.broadcasted_iota(jnp.int32, (WKV_ROWS, WKV_ROWS), 1)
    same_chunk = jnp.bitwise_and(gi, -C) == jnp.bitwise_and(gj, -C)
    cum_mask = jnp.where(jnp.logical_and(same_chunk, (gi - gj) * sgn >= 0), 1.0, 0.0).astype(BF16)
    cum_ref[...] = _dot2_left(cum_mask, logw)
    iclr = _sigmoid(a0_ref[...] + _dot(xa, aup_ref[...]))
    k = k_ref[...]
    kd_ref[...] = k * (1.0 + (iclr - 1.0) * ka_ref[...])
    kkr = k * kk_ref[...]
    for p in pairs:
        slab = kkr[:, cols[p]]
        kkn = slab * lax.rsqrt(_dot(slab * slab, head_sum) + 1e-12)
        kkn_ref[:, cols[p]] = kkn
        b_ref[:, cols[p]] = kkn * iclr[:, cols[p]]

    def chunk_rows(i):
        cc = jnp.where(d == 0, i, WKV_GROUP - 1 - i)
        return pl.ds(pl.multiple_of(cc * C, C), C)

    def phase_a(i, carry):
        rows = chunk_rows(i)
        head0 = lax.broadcasted_iota(jnp.int32, (C, PAIR_W), 1) < HEAD_DIM
        strict_t = msk_ref[0]
        incl = msk_ref[1]
        eye = msk_ref[2]
        bd_f = msk_ref[3]

        lw = [lw_ref[rows, c] for c in cols]
        cum = [cum_ref[rows, c] for c in cols]
        kkn = [kkn_ref[rows, c] for c in cols]
        bb = [b_ref[rows, c] for c in cols]
        kd = [kd_ref[rows, c] for c in cols]
        rr = [r_ref[rows, c] for c in cols]
        vv = [v_ref[rows, c] for c in cols]
        cum_prev = [cm - x for cm, x in zip(cum, lw)]
        mid = [cm[C // 2:C // 2 + 1, :] for cm in cum]
        total = [jnp.where(d == 0, cm[C - 1:C, :], cm[0:1, :]) for cm in cum]
        e_in = [jnp.exp(cm - m) for cm, m in zip(cum, mid)]
        e_in_prev = [jnp.exp(cm - m) for cm, m in zip(cum_prev, mid)]
        e_out = [jnp.exp(m - cm) for cm, m in zip(cum, mid)]
        e_tail = [jnp.exp(t - cm) for cm, t in zip(cum, total)]
        a1s = [_stack_heads(x * e, head0) for x, e in zip(kkn, e_in_prev)]
        r1s = [_stack_heads(x * e, head0) for x, e in zip(rr, e_in)]
        b1 = [x * e for x, e in zip(bb, e_out)]
        k1 = [x * e for x, e in zip(kd, e_out)]
        a0s = [_stack_heads(x * jnp.exp(cm), head0) for x, cm in zip(kkn, cum_prev)]
        vs = [_stack_heads(x, head0) for x in vv]
        g_t = [_dot_nt(jnp.concatenate([b, k], axis=0), a) for a, b, k in zip(a1s, b1, k1)]
        strict_s = strict_t[:C] + strict_t[C:]
        eye_s = eye[:C] + eye[C:]
        stack = lambda z: _stack_heads(z, head0)
        l_ab_s = [g_[:C] * strict_s for g_ in g_t]
        l_ak_t = [stack(g_[C:] * strict_s) for g_ in g_t]
        g_r = [_dot_nt(a, jnp.concatenate([b, b, k, k], axis=0)) for a, b, k in zip(r1s, b1, k1)]
        m_rb = [g_[:, :PAIR_W] * incl for g_ in g_r]
        m_rk = [g_[:, PAIR_W:] * incl for g_ in g_r]
        lakv_t = [_dot_tn(a, b) for a, b in zip(vs, l_ak_t)]
        mv = [_unstack_heads(_dot(a, b)) for a, b in zip(m_rk, vs)]
        kv = [_dot_tn(a, b * e) * bd_f for a, b, e in zip(vv, kd, e_tail)]
        x = [eye_s - l_ for l_ in l_ab_s]
        q = [_dot(l_, stack(l_)) for l_ in l_ab_s]
        step = 2
        while 2 * step < C:
            prod = [_dot(jnp.concatenate([xx, qq], axis=0), stack(qq)) for xx, qq in zip(x, q)]
            x = [xx + pr[:C] for xx, pr in zip(x, prod)]
            q = [pr[C:] for pr in prod]
            step *= 2
        x = [stack(xx + _dot(xx, stack(qq))) for xx, qq in zip(x, q)]
        a_hat_t = [_dot_tn(a, xx) for xx, a in zip(x, a0s)]
        y_hat_t = [_dot(a, xx) for xx, a in zip(x, lakv_t)]
        for p in pairs:
            ahat_ref[i, p] = a_hat_t[p].astype(BF16)
            yhat_ref[i, p] = y_hat_t[p]
            mrb_ref[i, p] = m_rb[p].astype(BF16)
            r0_ref[i, p] = (rr[p] * jnp.exp(cum[p])).astype(BF16)
            b2s_ref[i, p] = _stack_heads(bb[p] * e_tail[p], head0).astype(BF16)
            kv_ref[i, p] = kv[p]
            mv_ref[i, p] = mv[p]
            dec_ref[i, p] = jnp.broadcast_to(jnp.exp(total[p]), (8, PAIR_W))
        return carry

    lax.fori_loop(0, WKV_GROUP, phase_a, 0, unroll=2)

    def phase_b(i, carry):
        rows = chunk_rows(i)
        s_old = [s_ref[p] for p in pairs]
        s_bf = [s.astype(BF16) for s in s_old]
        yt = [(_dot(s_bf[p], ahat_ref[i, p]) + yhat_ref[i, p]).astype(BF16) for p in pairs]
        for p in pairs:
            s_ref[p] = dec_ref[i, p, 0:1, :] * s_old[p] + kv_ref[i, p] - _dot(yt[p], b2s_ref[i, p])
        o = [_dot_nt(r0_ref[i, p], s_bf[p]) + mv_ref[i, p] - _unstack_heads(_dot_nt(mrb_ref[i, p], yt[p]))
             for p in pairs]
        for p in pairs:
            o_ref[0, rows, cols[p]] = o[p]
        return carry

    lax.fori_loop(0, WKV_GROUP, phase_b, 0, unroll=True)

    if has_state_out:
        @pl.when(g == pl.num_programs(2) - 1)
        def _():
            for p in pairs:
                s = s_ref[p]
                sout_ref[2 * p] = s[:HEAD_DIM, :HEAD_DIM]
                sout_ref[2 * p + 1] = s[HEAD_DIM:, HEAD_DIM:]


def _wkv_call(proj, lp, l, s0, o_prev, row0, n_seq, seq_len, has_state_out):
    n_groups = seq_len // WKV_ROWS
    blk0 = row0 // WKV_ROWS
    has_init = s0 is not None
    has_alias = o_prev is not None

    def group(d, s, g):
        return s * n_groups + g + d * (n_groups - 1 - 2 * g)

    def col(cb):
        return lambda d, s, g: (blk0 + group(d, s, g), cb)

    per_dir_mat = pl.BlockSpec((None, None, LORA_W, D_RWKV), lambda d, s, g: (l, d, 0, 0))
    per_dir_vec = pl.BlockSpec((None, None, 1, D_RWKV), lambda d, s, g: (l, d, 0, 0))
    shared = pl.BlockSpec((None, 1, D_RWKV), lambda d, s, g: (l, 0, 0))
    in_specs = [
        pl.BlockSpec((WKV_ROWS, D_RWKV), col(0)),
        pl.BlockSpec((WKV_ROWS, D_RWKV), col(1)),
        pl.BlockSpec((WKV_ROWS, D_RWKV), col(2)),
        pl.BlockSpec((WKV_ROWS, TN_IN), col(LO_BLOCK)),
        per_dir_mat, per_dir_vec, per_dir_mat, per_dir_vec, shared, shared,
    ]
    args = [proj, proj, proj, proj, lp["w_up"], lp["w0"], lp["a_up"], lp["a0"], lp["k_k"], lp["k_a"]]
    if has_init:
        in_specs.append(pl.BlockSpec((None, None, None, N_HEADS, HEAD_DIM, HEAD_DIM),
                                     lambda d, s, g: (s, l, d, 0, 0, 0)))
        args.append(s0)
    aliases = {}
    if has_alias:
        aliases = {len(args): 0}
        in_specs.append(pl.BlockSpec(memory_space=pl.ANY))
        args.append(o_prev)
    out_shape = [jax.ShapeDtypeStruct((2, proj.shape[0], D_RWKV), F32)]
    out_specs = [pl.BlockSpec((1, WKV_ROWS, D_RWKV), lambda d, s, g: (d, blk0 + group(d, s, g), 0))]
    if has_state_out:
        out_shape.append(jax.ShapeDtypeStruct((n_seq, 2, N_HEADS, HEAD_DIM, HEAD_DIM), F32))
        out_specs.append(pl.BlockSpec((None, None, N_HEADS, HEAD_DIM, HEAD_DIM), lambda d, s, g: (s, d, 0, 0, 0)))
    kern = functools.partial(_wkv_kernel, has_init=has_init, has_state_out=has_state_out, has_alias=has_alias)
    per_chunk = (WKV_GROUP, N_PAIRS)
    return pl.pallas_call(
        kern,
        out_shape=out_shape,
        grid=(2, n_seq, n_groups),
        in_specs=in_specs,
        out_specs=out_specs,
        input_output_aliases=aliases,
        scratch_shapes=[
            pltpu.VMEM((N_PAIRS, PAIR_W, PAIR_W), F32),
            pltpu.VMEM((WKV_ROWS, D_RWKV), F32),
            pltpu.VMEM((WKV_ROWS, D_RWKV), F32),
            pltpu.VMEM((WKV_ROWS, D_RWKV), F32),
            pltpu.VMEM((WKV_ROWS, D_RWKV), F32),
            pltpu.VMEM((WKV_ROWS, D_RWKV), F32),
            pltpu.VMEM((4, PAIR_W, PAIR_W), F32),
            pltpu.VMEM(per_chunk + (PAIR_W, PAIR_W), BF16),
            pltpu.VMEM(per_chunk + (PAIR_W, PAIR_W), F32),
            pltpu.VMEM(per_chunk + (PAIR_W, PAIR_W), BF16),
            pltpu.VMEM(per_chunk + (CHUNK, PAIR_W), BF16),
            pltpu.VMEM(per_chunk + (PAIR_W, PAIR_W), BF16),
            pltpu.VMEM(per_chunk + (PAIR_W, PAIR_W), F32),
            pltpu.VMEM(per_chunk + (CHUNK, PAIR_W), F32),
            pltpu.VMEM(per_chunk + (8, PAIR_W), F32),
        ],
        compiler_params=_cparams(("parallel", "parallel", "arbitrary")),
        name="wkv",
    )(*args)


def _rwkvpost_kernel(o_ref, r_ref, k_ref, v_ref, lo_ref, aup_ref, a0_ref, ka_ref, rk_ref, lng_ref, lnb_ref,
                     gup_ref, out_ref):
    lo = lo_ref[...]
    xa = lo[:, LORA_W:LORA_W + LORA_A]
    sg = _sigmoid(lo[:, LORA_W + LORA_A:LORA_W + LORA_A + LORA_G_PAD])
    gi = lax.broadcasted_iota(jnp.int32, (PAIR_W, PAIR_W), 0)
    gj = lax.broadcasted_iota(jnp.int32, (PAIR_W, PAIR_W), 1)
    head_sum = jnp.where((gi >= HEAD_DIM) == (gj >= HEAD_DIM), 1.0, 0.0).astype(BF16)
    cols = [slice(p * PAIR_W, (p + 1) * PAIR_W) for p in range(N_PAIRS)]
    gate = _dot(sg, gup_ref[...])
    ic = _sigmoid(a0_ref[0] + _dot(xa, aup_ref[0])) + _sigmoid(a0_ref[1] + _dot(xa, aup_ref[1]))
    kd_sum = k_ref[...] * (2.0 + (ic - 2.0) * ka_ref[...])
    bterm = r_ref[...] * kd_sum * rk_ref[...]
    o = [o_ref[0, :, c] + o_ref[1, :, c] for c in cols]
    mu = [_dot2(x, head_sum) * (1.0 / HEAD_DIM) for x in o]
    oc = [x - m for x, m in zip(o, mu)]
    var = [_dot(x * x, head_sum) * (1.0 / HEAD_DIM) for x in oc]
    bonus = [_dot(bterm[:, c], head_sum) * v_ref[:, c] for c in cols]
    for c, x, vr, bn in zip(cols, oc, var, bonus):
        y = x * lax.rsqrt(vr + GN_EPS) * lng_ref[:, c] + lnb_ref[:, c]
        out_ref[:, c] = ((y + bn) * gate[:, c]).astype(BF16)


def _rwkvpost_call(o, proj, lp, l):
    n_rows = o.shape[1]

    def col(cb):
        return lambda i: (i, cb)

    shared = pl.BlockSpec((None, 1, D_RWKV), lambda i: (l, 0, 0))
    return pl.pallas_call(
        _rwkvpost_kernel,
        out_shape=jax.ShapeDtypeStruct((n_rows, D_RWKV), BF16),
        grid=(n_rows // TM_POST,),
        in_specs=[
            pl.BlockSpec((2, TM_POST, D_RWKV), lambda i: (0, i, 0)),
            pl.BlockSpec((TM_POST, D_RWKV), col(0)),
            pl.BlockSpec((TM_POST, D_RWKV), col(1)),
            pl.BlockSpec((TM_POST, D_RWKV), col(2)),
            pl.BlockSpec((TM_POST, TN_IN), col(LO_BLOCK)),
            pl.BlockSpec((None, 2, LORA_A, D_RWKV), lambda i: (l, 0, 0, 0)),
            pl.BlockSpec((None, 2, 1, D_RWKV), lambda i: (l, 0, 0, 0)),
            shared, shared, shared, shared,
            pl.BlockSpec((None, LORA_G_PAD, D_RWKV), lambda i: (l, 0, 0)),
        ],
        out_specs=pl.BlockSpec((TM_POST, D_RWKV), lambda i: (i, 0)),
        compiler_params=_cparams(("parallel",)),
        name="rwkvpost",
    )(o, proj, proj, proj, proj, lp["a_up"], lp["a0"], lp["k_a"], lp["r_k"], lp["ln_g"], lp["ln_b"], lp["g_up"])


def _dft_mats(L):
    idx = (np.arange(L)[:, None] * np.arange(L)[None, :]) % (2 * L)
    ang = np.pi * idx.astype(np.float64) / L
    alt = np.cos(np.pi * np.arange(L))
    fc = np.cos(ang)
    fs = -np.sin(ang)
    fs[0, :] = alt
    fwd = np.concatenate([fc, fs], axis=0)
    ic = 2.0 * np.cos(ang.T)
    ic[:, 0] = 1.0
    isn = -2.0 * np.sin(ang.T)
    isn[:, 0] = alt
    inv = np.concatenate([ic, isn], axis=1) / (2 * L)
    return fwd.astype(np.float32), inv.astype(np.float32)


def _filter_features(L):
    t = np.linspace(0.0, 1.0, L, dtype=np.float32)[:, None]
    w = (2.0 * math.pi / L) * np.arange(L, dtype=np.float32)[:, None]
    f = np.linspace(1e-4, FILT_BANDS - 1, FILT_BANDS, dtype=np.float32)[None, :]
    z = np.concatenate([t, np.cos(f * w), -np.sin(f * w)], axis=-1)
    zp = np.zeros((L, FILT_EMB_PAD), np.float32)
    zp[:, :FILT_EMB] = z
    return zp, t


def _filter_deltas():
    max_decay = math.log(FILT_TARGET) / FAST_DECAY_PCT
    min_decay = math.log(FILT_TARGET) / SLOW_DECAY_PCT
    return np.abs(np.linspace(min_decay, max_decay, D_HYENA, dtype=np.float32))[None, :]


def _hyfilt_kernel(z_ref, t_ref, dl_ref, fwd_ref, w1_ref, b1_ref, w2_ref, b2_ref,
                   w3f0_ref, w3b0_ref, w3f1_ref, w3b1_ref, fr_ref, o_ref):
    L = z_ref.shape[0]
    tc = o_ref.shape[3]
    h = jnp.sin(fr_ref[0:1, :] * (_dot3(z_ref[...], w1_ref[...]) + b1_ref[...]))
    h = jnp.sin(fr_ref[1:2, :] * (_dot3(h, w2_ref[...]) + b2_ref[...]))
    decay = jnp.exp(-t_ref[...] * dl_ref[...])
    first = lax.broadcasted_iota(jnp.int32, (L, tc), 0) == 0
    fwd = fwd_ref[...]
    for order, (wf_ref, wb_ref) in enumerate(((w3f0_ref, w3b0_ref), (w3f1_ref, w3b1_ref))):
        hf = _dot3(h, wf_ref[...]) * decay
        hb = _dot3(h, wb_ref[...]) * decay
        norm = jnp.sum(jnp.abs(hf) + jnp.abs(hb), axis=0, keepdims=True)
        hf = hf / norm
        hb = jnp.where(first, 0.0, hb / norm)
        ks = _dot(fwd, hf + hb)
        kd = _dot(fwd, hf - hb)
        kr = ks[:L]
        o_ref[order, 0] = kr
        o_ref[order, 1] = jnp.where(first, 0.0, kd[L:])
        o_ref[order, 2] = jnp.where(first, ks[L:L + 1], kr)


def _hyfilt_call(L, fp, l):
    z, t = _filter_features(L)
    fwd, _ = _dft_mats(L)
    nct = D_HYENA // TC_FILT
    full = lambda shape: pl.BlockSpec(shape, lambda j: tuple(0 for _ in shape))
    layer = lambda shape: pl.BlockSpec((None,) + shape, lambda j: (l,) + tuple(0 for _ in shape))
    w3_spec = lambda grp: pl.BlockSpec((None, FILT_HIDDEN, TC_FILT), lambda j: (l, 0, grp * nct + j))
    w3 = fp["w3"]
    return pl.pallas_call(
        _hyfilt_kernel,
        out_shape=jax.ShapeDtypeStruct((2, 3, L, D_HYENA), F32),
        grid=(nct,),
        in_specs=[
            full((L, FILT_EMB_PAD)), full((L, 1)), pl.BlockSpec((1, TC_FILT), lambda j: (0, j)), full((2 * L, L)),
            layer((FILT_EMB_PAD, FILT_HIDDEN)), layer((1, FILT_HIDDEN)),
            layer((FILT_HIDDEN, FILT_HIDDEN)), layer((1, FILT_HIDDEN)),
            w3_spec(0), w3_spec(1), w3_spec(2), w3_spec(3),
            layer((2, FILT_HIDDEN)),
        ],
        out_specs=pl.BlockSpec((2, 3, L, TC_FILT), lambda j: (0, 0, 0, j)),
        compiler_params=_cparams(("parallel",)),
        name="hyfilt",
    )(jnp.asarray(z), jnp.asarray(t), jnp.asarray(_filter_deltas()), jnp.asarray(fwd).astype(BF16),
      fp["w1"], fp["b1"], fp["w2"], fp["b2"], w3, w3, w3, w3, fp["freq"])


def _hyena_kernel(u_ref, x1_ref, x2_ref, fwd_ref, inv_ref, spec_ref, bias_ref, *rest):
    out_ref = rest[-1]
    L = fwd_ref.shape[1]
    n_sub = u_ref.shape[0] // L

    def side_by_side(ref):
        return jnp.concatenate([ref[s * L:(s + 1) * L, :] for s in range(n_sub)], axis=1)

    def tiled(x):
        return jnp.concatenate([x] * n_sub, axis=1)

    def long_conv(u, order):
        spec = _dot(fwd_ref[...], u)
        ur, ui = spec[:L], spec[L:]
        kr, ki, kr2 = tiled(spec_ref[order, 0]), tiled(spec_ref[order, 1]), tiled(spec_ref[order, 2])
        yr = ur * kr - ui * ki
        yi = ur * ki + ui * kr2
        y = _dot(inv_ref[...], jnp.concatenate([yr, yi], axis=0))
        return y + u * tiled(bias_ref[order:order + 1, :])

    z = side_by_side(x1_ref) * long_conv(side_by_side(u_ref), 0)
    out = (side_by_side(x2_ref) * long_conv(z, 1)).astype(BF16)
    tc = out_ref.shape[1]
    for s in range(n_sub):
        out_ref[s * L:(s + 1) * L, :] = out[:, s * tc:(s + 1) * tc]


def _hyena_call(proj, spec, bias, l, out_prev, row0, n_seq, L, tc, n_sub):
    fwd, inv = _dft_mats(L)
    nct = D_HYENA // tc
    rows = n_sub * L
    blk0 = row0 // rows
    cb0 = 3 * D_RWKV // tc

    def col(which):
        return lambda s, j: (blk0 + s, cb0 + which * nct + j)

    in_specs = [
        pl.BlockSpec((rows, tc), col(0)),
        pl.BlockSpec((rows, tc), col(1)),
        pl.BlockSpec((rows, tc), col(2)),
        pl.BlockSpec((2 * L, L), lambda s, j: (0, 0), pipeline_mode=pl.Buffered(1)),
        pl.BlockSpec((L, 2 * L), lambda s, j: (0, 0), pipeline_mode=pl.Buffered(1)),
        pl.BlockSpec((2, 3, L, tc), lambda s, j: (0, 0, 0, j)),
        pl.BlockSpec((None, 2, tc), lambda s, j: (l, 0, j)),
    ]
    args = [proj, proj, proj, jnp.asarray(fwd).astype(BF16), jnp.asarray(inv).astype(BF16), spec, bias]
    aliases = {}
    if out_prev is not None:
        aliases = {len(args): 0}
        in_specs.append(pl.BlockSpec(memory_space=pl.ANY))
        args.append(out_prev)
    return pl.pallas_call(
        _hyena_kernel,
        out_shape=jax.ShapeDtypeStruct((proj.shape[0], D_HYENA), BF16),
        grid=(n_seq // n_sub, nct),
        in_specs=in_specs,
        out_specs=pl.BlockSpec((rows, tc), lambda s, j: (blk0 + s, j)),
        input_output_aliases=aliases,
        compiler_params=_cparams(("parallel", "parallel")),
        name="hyena",
    )(*args)


def _outproj_kernel(a_ref, b_ref, x_ref, g_ref, w_ref, lg_ref, lb_ref, o_ref):
    mix = (jnp.dot(a_ref[...], w_ref[0:D_RWKV, :], preferred_element_type=F32)
           + jnp.dot(b_ref[...], w_ref[D_RWKV:, :], preferred_element_type=F32))
    y = ALPHA * x_ref[...] + g_ref[0, 0] * mix
    o_ref[...] = _layer_norm_rows(y) * lg_ref[...] + lb_ref[...]


def _outproj_call(a_out, b_out, x, mod_all, w_out, ln_g, ln_b, l, n_ctx_units):
    grp = lambda i: _tile_group(i, TM_OUT, n_ctx_units)
    vec = pl.BlockSpec((None, 1, D_MODEL), lambda i: (l, 0, 0))
    return pl.pallas_call(
        _outproj_kernel,
        out_shape=jax.ShapeDtypeStruct(x.shape, F32),
        grid=(x.shape[0] // TM_OUT,),
        in_specs=[
            pl.BlockSpec((TM_OUT, D_RWKV), lambda i: (i, 0)),
            pl.BlockSpec((TM_OUT, D_HYENA), lambda i: (i, 0)),
            pl.BlockSpec((TM_OUT, D_MODEL), lambda i: (i, 0)),
            _mod_spec(l, 2, grp),
            pl.BlockSpec((None, D_MODEL, D_MODEL), lambda i: (l, 0, 0)),
            vec, vec,
        ],
        out_specs=pl.BlockSpec((TM_OUT, D_MODEL), lambda i: (i, 0)),
        compiler_params=_cparams(("parallel",)),
        name="outproj",
    )(a_out, b_out, x, mod_all, w_out, ln_g, ln_b)


def _mlp_kernel(x_ref, sh_ref, sc_ref, g_ref, w1_ref, w2_ref, lg_ref, lb_ref, o_ref, h_ref):
    f = pl.program_id(1)
    row_blocks = o_ref.shape[0] // 128

    @pl.when(f == 0)
    def _():
        sc = 1.0 + sc_ref[0, 0]
        sh = sh_ref[0, 0]

        def body(rb, carry):
            rows = pl.ds(pl.multiple_of(rb * 128, 128), 128)
            h_ref[rows, :] = (_layer_norm_rows(x_ref[rows, :]) * sc + sh).astype(BF16)
            return carry

        lax.fori_loop(0, row_blocks, body, 0)
        o_ref[...] = jnp.zeros(o_ref.shape, F32)

    hid = jnp.maximum(jnp.dot(h_ref[...], w1_ref[...].astype(BF16), preferred_element_type=F32), 0.0)
    hid = (hid * hid).astype(BF16)
    o_ref[...] += jnp.dot(hid, w2_ref[...].astype(BF16), preferred_element_type=F32)

    @pl.when(f == pl.num_programs(1) - 1)
    def _():
        gate = g_ref[0, 0]
        lg = lg_ref[...]
        lb = lb_ref[...]

        def body(rb, carry):
            rows = pl.ds(pl.multiple_of(rb * 128, 128), 128)
            y = ALPHA * x_ref[rows, :] + gate * o_ref[rows, :]
            o_ref[rows, :] = _layer_norm_rows(y) * lg + lb
            return carry

        lax.fori_loop(0, row_blocks, body, 0)


def _mlp_call(x, mod_all, w1, w2, ln_g, ln_b, l, n_ctx_units):
    grp = lambda i: _tile_group(i, TM_MLP, n_ctx_units)
    vec = pl.BlockSpec((None, 1, D_MODEL), lambda i, f: (l, 0, 0))
    return pl.pallas_call(
        _mlp_kernel,
        out_shape=jax.ShapeDtypeStruct(x.shape, F32),
        grid=(x.shape[0] // TM_MLP, D_FF // TF_MLP),
        in_specs=[
            pl.BlockSpec((TM_MLP, D_MODEL), lambda i, f: (i, 0), pipeline_mode=pl.Buffered(1)),
            _mod_spec(l, 3, grp),
            _mod_spec(l, 4, grp),
            _mod_spec(l, 5, grp),
            pl.BlockSpec((None, D_MODEL, TF_MLP), lambda i, f: (l, 0, f)),
            pl.BlockSpec((None, TF_MLP, D_MODEL), lambda i, f: (l, f, 0)),
            vec, vec,
        ],
        out_specs=pl.BlockSpec((TM_MLP, D_MODEL), lambda i, f: (i, 0)),
        scratch_shapes=[pltpu.VMEM((TM_MLP, D_MODEL), BF16)],
        compiler_params=_cparams(("parallel", "arbitrary")),
        name="mlp",
    )(x, mod_all, mod_all, mod_all, w1, w2, ln_g, ln_b)


def kernel(x_prompt, x_sample, c, state_rwkv, c_ctx, w_ada, b_ada, w_in, conv_w, lora_w_up, lora_w0, lora_a_up, lora_a0, lora_g_up, rwkv_k_k, rwkv_k_a, rwkv_r_k, rwkv_ln_g, rwkv_ln_b, filt_w1, filt_b1, filt_w2, filt_b2, filt_w3, filt_freq, hyena_bias, w_out, ln1_g, ln1_b, ln2_g, ln2_b, mlp_w1, mlp_w2):
    n_ctx, ctx_seq, _ = x_prompt.shape
    n_lat, lat_seq, _ = x_sample.shape
    depth = w_ada.shape[0]
    assert ctx_seq == CTX_SEQ and lat_seq == LAT_SEQ and (n_ctx * ctx_seq) % TM == 0
    assert 1 + n_lat <= N_COND
    ctx_rows = n_ctx * ctx_seq
    lat_rows = n_lat * lat_seq
    n_ctx_units = ctx_rows // TM

    x = jnp.concatenate([x_prompt.reshape(ctx_rows, D_MODEL), x_sample.reshape(lat_rows, D_MODEL)], axis=0)
    cond = jnp.concatenate([c_ctx[None, :], c, jnp.zeros((N_COND - 1 - n_lat, D_MODEL), F32)], axis=0)
    mod_all = _mod_call(cond, w_ada, b_ada).reshape(depth, N_COND, 6, 1, D_MODEL)

    w_in_t = jnp.swapaxes(w_in, 1, 2)
    w_lo_bf = jnp.pad(w_in[:, :, N_CONV:].astype(BF16), ((0, 0), (0, 0), (0, TN_IN - (D_IN - N_CONV))))
    conv_w9 = conv_w.reshape(depth, 9, N_CONV)
    w_out_bf = w_out.astype(BF16)
    row = lambda a: a.reshape(depth, 1, a.shape[-1])
    lp = {
        "w_up": lora_w_up, "w0": lora_w0[:, :, None, :], "a_up": lora_a_up, "a0": lora_a0[:, :, None, :],
        "k_k": row(rwkv_k_k), "k_a": row(rwkv_k_a), "r_k": rwkv_r_k.reshape(depth, 1, D_RWKV),
        "ln_g": row(rwkv_ln_g), "ln_b": row(rwkv_ln_b),
        "g_up": jnp.pad(lora_g_up, ((0, 0), (0, LORA_G_PAD - LORA_G), (0, 0))),
    }
    fp = {
        "w1": jnp.pad(filt_w1, ((0, 0), (0, FILT_EMB_PAD - FILT_EMB), (0, 0))), "b1": row(filt_b1),
        "w2": filt_w2, "b2": row(filt_b2), "w3": filt_w3, "freq": filt_freq,
    }
    ln1_g, ln1_b, ln2_g, ln2_b = row(ln1_g), row(ln1_b), row(ln2_g), row(ln2_b)
    state_in = state_rwkv.astype(F32)

    ctx_states = []
    for l in range(depth):
        proj = _inproj_call(x, mod_all, w_in_t, w_lo_bf, conv_w9, l, n_ctx_units)

        o, s_ctx = _wkv_call(proj, lp, l, None, None, 0, n_ctx, ctx_seq, True)
        (o,) = _wkv_call(proj, lp, l, state_in, o, ctx_rows, n_lat, lat_seq, False)
        a_out = _rwkvpost_call(o, proj, lp, l)
        ctx_states.append(s_ctx)

        spec_ctx = _hyfilt_call(ctx_seq, fp, l)
        spec_lat = _hyfilt_call(lat_seq, fp, l)
        b_out = _hyena_call(proj, spec_ctx, hyena_bias, l, None, 0, n_ctx, ctx_seq, D_HYENA, 1)
        lat_sub = 2 if n_lat % 2 == 0 and ctx_rows % (2 * lat_seq) == 0 else 1
        b_out = _hyena_call(proj, spec_lat, hyena_bias, l, b_out, ctx_rows, n_lat, lat_seq, 256, lat_sub)

        x = _outproj_call(a_out, b_out, x, mod_all, w_out_bf, ln1_g, ln1_b, l, n_ctx_units)
        x = _mlp_call(x, mod_all, mlp_w1, mlp_w2, ln2_g, ln2_b, l, n_ctx_units)

    y_prompt = x[:ctx_rows].reshape(n_ctx, ctx_seq, D_MODEL)
    y_sample = x[ctx_rows:].reshape(n_lat, lat_seq, D_MODEL)
    new_state = jnp.stack(ctx_states, axis=1).astype(x_prompt.dtype)
    return (y_prompt, y_sample, new_state)
```

```python
import functools
import math

import jax
import jax.numpy as jnp
import numpy as np
from jax import lax
from jax.experimental import pallas as pl
from jax.experimental.pallas import tpu as pltpu

F32 = jnp.float32
BF16 = jnp.bfloat16

D_MODEL = 2048
D_RWKV = 1024
D_HYENA = 1024
HEAD_DIM = 64
N_HEADS = D_RWKV // HEAD_DIM
N_PAIRS = N_HEADS // 2
PAIR_W = 2 * HEAD_DIM
LORA_W = 64
LORA_A = 64
LORA_G = 160
LORA_G_PAD = 256
N_CONV = 3 * D_RWKV + 3 * D_HYENA
D_IN = N_CONV + LORA_W + LORA_A + LORA_G
D_FF = 4 * D_MODEL
GRID_W = 64
CTX_SEQ = 256
LAT_SEQ = 1024
FILT_BANDS = 16
FILT_EMB = 1 + 2 * FILT_BANDS
FILT_EMB_PAD = 128
FILT_HIDDEN = 64
DEPTH = 2
ALPHA = (2 * DEPTH) ** 0.25
LN_EPS = 1e-5
GN_EPS = 64e-5
FILT_TARGET = 1e-2
FAST_DECAY_PCT = 0.3
SLOW_DECAY_PCT = 1.5
DECAY_SCALE = math.exp(-0.5)

TM = 1024
TN_IN = 512
NJ_CONV = N_CONV // TN_IN
D_IN_PAD = (NJ_CONV + 1) * TN_IN
LO_BLOCK = N_CONV // TN_IN
CONV_PAD = 72
CHUNK = 64
WKV_GROUP = 4
WKV_ROWS = WKV_GROUP * CHUNK
N_COND = 8
TN_MOD = 1024
TM_OUT = 512
TM_MLP = 1024
TF_MLP = 512
TM_POST = 256
TC_FILT = 256
VMEM_LIMIT = 56 * 1024 * 1024


def _cparams(sem):
    return pltpu.CompilerParams(dimension_semantics=sem, vmem_limit_bytes=VMEM_LIMIT)


def _dot(a, b):
    return jnp.dot(a.astype(BF16), b.astype(BF16), preferred_element_type=F32)


def _dot_nt(a, b):
    return lax.dot_general(a.astype(BF16), b.astype(BF16), (((1,), (1,)), ((), ())),
                           preferred_element_type=F32)


def _dot_tn(a, b):
    return lax.dot_general(a.astype(BF16), b.astype(BF16), (((0,), (0,)), ((), ())),
                           preferred_element_type=F32)


def _split(x):
    hi = x.astype(BF16)
    lo = (x - hi.astype(F32)).astype(BF16)
    return hi, lo


def _dot3(a, b):
    ah, al = _split(a)
    bh, bl = _split(b)
    return _dot(ah, bh) + (_dot(ah, bl) + _dot(al, bh))


def _dot2(a, b_exact):
    ah, al = _split(a)
    return _dot(ah, b_exact) + _dot(al, b_exact)


def _dot2_left(a_exact, b):
    bh, bl = _split(b)
    return _dot(a_exact, bh) + _dot(a_exact, bl)


def _sigmoid(x):
    return 1.0 / (1.0 + jnp.exp(-x))


def _layer_norm_rows(x):
    mu = jnp.mean(x, axis=-1, keepdims=True)
    xc = x - mu
    var = jnp.mean(xc * xc, axis=-1, keepdims=True)
    return xc * lax.rsqrt(var + LN_EPS)


def _cond_row(i, tm, is_grid):
    return 1 + (i * tm) // LAT_SEQ if is_grid else 0


def _mod_spec(l, which, grp):
    return pl.BlockSpec((None, 1, 1, 1, D_MODEL), lambda i, *_: (l, grp(i), which, 0, 0))


def _mod_kernel(c_ref, w_ref, b_ref, o_ref):
    c = c_ref[...]
    s = c * _sigmoid(c)
    o_ref[0] = _dot(s, w_ref[0]) + b_ref[0]


def _mod_call(cond, w_ada, b_ada):
    depth = w_ada.shape[0]
    n_out = w_ada.shape[2]
    return pl.pallas_call(
        _mod_kernel,
        out_shape=jax.ShapeDtypeStruct((depth, N_COND, n_out), F32),
        grid=(depth, n_out // TN_MOD),
        in_specs=[
            pl.BlockSpec((N_COND, D_MODEL), lambda l, j: (0, 0)),
            pl.BlockSpec((1, D_MODEL, TN_MOD), lambda l, j: (l, 0, j)),
            pl.BlockSpec((1, 1, TN_MOD), lambda l, j: (l, 0, j)),
        ],
        out_specs=pl.BlockSpec((1, N_COND, TN_MOD), lambda l, j: (l, 0, j)),
        compiler_params=_cparams(("parallel", "parallel")),
        name="mod",
    )(cond, w_ada, b_ada.reshape(depth, 1, n_out))


def _inproj_kernel(x_ref, sh_ref, sc_ref, w_ref, wlo_ref, cw_ref, o_ref, h_ref, wbf_ref, acc_a, acc_b, *,
                   is_grid):
    j = pl.program_id(1)
    tn = o_ref.shape[1]
    mid = slice(CONV_PAD, CONV_PAD + TM)

    n_blocks = TM // GRID_W
    n_parts = 4
    lane_w = 128
    n_chunk = 128

    def round_weights():
        for nc in range(tn // n_chunk):
            rows = slice(nc * n_chunk, (nc + 1) * n_chunk)
            wbf_ref[rows, :] = w_ref[rows, :].astype(BF16)

    def matmul_into(dst_ref, part=None):
        if part is None:
            dst_ref[mid, :] = _dot_nt(h_ref[...], wbf_ref[...])
        else:
            r0, r1 = part * (TM // n_parts), (part + 1) * (TM // n_parts)
            dst_ref[CONV_PAD + r0:CONV_PAD + r1, :] = _dot_nt(h_ref[r0:r1, :], wbf_ref[...])

    def conv_from(src_ref, g0=0, g1=n_blocks):
        row = lax.broadcasted_iota(jnp.int32, (GRID_W, lane_w), 0)
        first = row == 0
        last = row == GRID_W - 1
        per_seq = CTX_SEQ // GRID_W

        def shifted(s, lanes, mask_first, mask_last):
            start = CONV_PAD + s * GRID_W
            uc = src_ref[start:start + GRID_W, lanes]
            ul = src_ref[start - 1:start - 1 + GRID_W, lanes]
            ur = src_ref[start + 1:start + 1 + GRID_W, lanes]
            if mask_first:
                ul = jnp.where(first, 0.0, ul)
            if mask_last:
                ur = jnp.where(last, 0.0, ur)
            return ul, uc, ur

        for c in range(tn // lane_w):
            lanes = slice(c * lane_w, (c + 1) * lane_w)
            cw = cw_ref[:, lanes]
            if not is_grid:
                for s in range(g0, g1):
                    ul, uc, ur = shifted(s, lanes, s % per_seq == 0, s % per_seq == per_seq - 1)
                    o_ref[s * GRID_W:(s + 1) * GRID_W, lanes] = ul * cw[3:4, :] + uc * cw[4:5, :] + ur * cw[5:6, :]
                continue
            partial = {}
            for s in range(max(g0 - 1, 0), min(g1 + 1, n_blocks)):
                ul, uc, ur = shifted(s, lanes, True, True)
                for a in range(3):
                    gi = s - (a - 1)
                    if g0 <= gi < g1:
                        t = (ul * cw[3 * a:3 * a + 1, :] + uc * cw[3 * a + 1:3 * a + 2, :]
                             + ur * cw[3 * a + 2:3 * a + 3, :])
                        partial[gi] = t if gi not in partial else partial[gi] + t
                if s - 1 in partial:
                    o_ref[(s - 1) * GRID_W:s * GRID_W, lanes] = partial.pop(s - 1)
            for gi in sorted(partial):
                o_ref[gi * GRID_W:(gi + 1) * GRID_W, lanes] = partial.pop(gi)

    @pl.when(j == 0)
    def _():
        sc = 1.0 + sc_ref[0, 0]
        sh = sh_ref[0, 0]

        def body(rb, carry):
            rows = pl.ds(pl.multiple_of(rb * 128, 128), 128)
            h = _layer_norm_rows(x_ref[rows, :]) * sc + sh
            h_ref[rows, :] = h.astype(BF16)
            return carry

        lax.fori_loop(0, TM // 128, body, 0)
        for acc in (acc_a, acc_b):
            acc[0:CONV_PAD, :] = jnp.zeros((CONV_PAD, tn), F32)
            acc[CONV_PAD + TM:, :] = jnp.zeros((CONV_PAD, tn), F32)
        o_ref[...] = jnp.dot(h_ref[...], wlo_ref[...], preferred_element_type=F32)

    @pl.when(j == 1)
    def _():
        round_weights()
        matmul_into(acc_a)

    main = jnp.logical_and(j >= 2, j <= NJ_CONV)
    odd = jnp.bitwise_and(j, 1) == 1

    def overlapped(src_ref, dst_ref):
        round_weights()
        per_part = n_blocks // n_parts
        for part in range(n_parts):
            matmul_into(dst_ref, part)
            conv_from(src_ref, part * per_part, (part + 1) * per_part)

    @pl.when(jnp.logical_and(main, jnp.logical_not(odd)))
    def _():
        overlapped(acc_a, acc_b)

    @pl.when(jnp.logical_and(main, odd))
    def _():
        overlapped(acc_b, acc_a)

    @pl.when(j == NJ_CONV + 1)
    def _():
        conv_from(acc_b if NJ_CONV % 2 == 0 else acc_a)


def _inproj_call(x, mod_all, w_in, w_lo, conv_w, l, is_grid):
    n_units = x.shape[0] // TM
    kern = functools.partial(_inproj_kernel, is_grid=is_grid)
    grp = lambda i: _cond_row(i, TM, is_grid)
    tile = lambda t: jnp.clip(t, 0, NJ_CONV - 1)
    return pl.pallas_call(
        kern,
        out_shape=jax.ShapeDtypeStruct((x.shape[0], D_IN_PAD), F32),
        grid=(n_units, NJ_CONV + 2),
        in_specs=[
            pl.BlockSpec((TM, D_MODEL), lambda i, j: (i, 0)),
            _mod_spec(l, 0, grp),
            _mod_spec(l, 1, grp),
            pl.BlockSpec((None, TN_IN, D_MODEL), lambda i, j: (l, tile(j - 1), 0)),
            pl.BlockSpec((None, D_MODEL, TN_IN), lambda i, j: (l, 0, 0)),
            pl.BlockSpec((None, 9, TN_IN), lambda i, j: (l, 0, tile(j - 2))),
        ],
        out_specs=pl.BlockSpec((TM, TN_IN), lambda i, j: (i, jnp.where(j == 0, LO_BLOCK, tile(j - 2)))),
        scratch_shapes=[
            pltpu.VMEM((TM, D_MODEL), BF16),
            pltpu.VMEM((TN_IN, D_MODEL), BF16),
            pltpu.VMEM((TM + 2 * CONV_PAD, TN_IN), F32),
            pltpu.VMEM((TM + 2 * CONV_PAD, TN_IN), F32),
        ],
        compiler_params=_cparams(("parallel", "arbitrary")),
        name="inproj",
    )(x, mod_all, mod_all, w_in, w_lo, conv_w)


def _stack_heads(z, head0):
    return jnp.concatenate([jnp.where(head0, z, 0.0), jnp.where(head0, 0.0, z)], axis=0)


def _unstack_heads(z):
    c = z.shape[0] // 2
    return z[:c] + z[c:]


def _wkv_kernel(*refs, has_init, has_state_out):
    (r_ref, k_ref, v_ref, lo_ref, wup_ref, w0_ref, aup_ref, a0_ref, kk_ref, ka_ref) = refs[:10]
    pos = 10
    s0_ref = None
    if has_init:
        s0_ref = refs[pos]
        pos += 1
    o_ref = refs[pos]
    pos += 1
    sout_ref = None
    if has_state_out:
        sout_ref = refs[pos]
        pos += 1
    (s_ref, lw_ref, cum_ref, kkn_ref, b_ref, kd_ref, msk_ref,
     ahat_ref, yhat_ref, mrb_ref, r0_ref, b2s_ref, kv_ref, mv_ref, dec_ref) = refs[pos:]

    d = pl.program_id(0)
    g = pl.program_id(2)
    C = CHUNK
    pairs = range(N_PAIRS)
    cols = [slice(p * PAIR_W, (p + 1) * PAIR_W) for p in pairs]

    @pl.when(g == 0)
    def _():
        if has_init:
            zero = jnp.zeros((HEAD_DIM, HEAD_DIM), F32)
            for p in pairs:
                top = jnp.concatenate([s0_ref[2 * p], zero], axis=1)
                bottom = jnp.concatenate([zero, s0_ref[2 * p + 1]], axis=1)
                s_ref[p] = jnp.concatenate([top, bottom], axis=0)
        else:
            s_ref[...] = jnp.zeros(s_ref.shape, F32)

    sgn = 1 - 2 * d
    si = lax.broadcasted_iota(jnp.int32, (2 * C, 2 * C), 0)
    sj = lax.broadcasted_iota(jnp.int32, (2 * C, 2 * C), 1)
    same_head = (si >= C) == (sj >= C)
    dlt = (jnp.bitwise_and(si, C - 1) - jnp.bitwise_and(sj, C - 1)) * sgn
    msk_ref[0] = jnp.where(jnp.logical_and(same_head, dlt < 0), 1.0, 0.0)
    msk_ref[1] = jnp.where(jnp.logical_and(same_head, dlt >= 0), 1.0, 0.0)
    msk_ref[2] = jnp.where(si == sj, 1.0, 0.0)
    blockdiag = (si >= HEAD_DIM) == (sj >= HEAD_DIM)
    msk_ref[3] = jnp.where(blockdiag, 1.0, 0.0)
    head_sum = jnp.where(blockdiag, 1.0, 0.0).astype(BF16)

    lo = lo_ref[...]
    xw = lo[:, 0:LORA_W]
    xa = lo[:, LORA_W:LORA_W + LORA_A]
    logw = -DECAY_SCALE * _sigmoid(w0_ref[...] + _dot3(jnp.tanh(xw), wup_ref[...]))
    lw_ref[...] = logw
    gi = lax.broadcasted_iota(jnp.int32, (WKV_ROWS, WKV_ROWS), 0)
    gj = lax.broadcasted_iota(jnp.int32, (WKV_ROWS, WKV_ROWS), 1)
    same_chunk = jnp.bitwise_and(gi, -C) == jnp.bitwise_and(gj, -C)
    cum_mask = jnp.where(jnp.logical_and(same_chunk, (gi - gj) * sgn >= 0), 1.0, 0.0).astype(BF16)
    cum_ref[...] = _dot2_left(cum_mask, logw)
    iclr = _sigmoid(a0_ref[...] + _dot(xa, aup_ref[...]))
    k = k_ref[...]
    kd_ref[...] = k * (1.0 + (iclr - 1.0) * ka_ref[...])
    kkr = k * kk_ref[...]
    for p in pairs:
        slab = kkr[:, cols[p]]
        kkn = slab * lax.rsqrt(_dot(slab * slab, head_sum) + 1e-12)
        kkn_ref[:, cols[p]] = kkn
        b_ref[:, cols[p]] = kkn * iclr[:, cols[p]]

    def chunk_rows(i):
        cc = jnp.where(d == 0, i, WKV_GROUP - 1 - i)
        return pl.ds(pl.multiple_of(cc * C, C), C)

    def phase_a(i, carry):
        rows = chunk_rows(i)
        head0 = lax.broadcasted_iota(jnp.int32, (C, PAIR_W), 1) < HEAD_DIM
        strict_t = msk_ref[0]
        incl = msk_ref[1]
        eye = msk_ref[2]
        bd_f = msk_ref[3]

        lw = [lw_ref[rows, c] for c in cols]
        cum = [cum_ref[rows, c] for c in cols]
        kkn = [kkn_ref[rows, c] for c in cols]
        bb = [b_ref[rows, c] for c in cols]
        kd = [kd_ref[rows, c] for c in cols]
        rr = [r_ref[rows, c] for c in cols]
        vv = [v_ref[rows, c] for c in cols]
        cum_prev = [cm - x for cm, x in zip(cum, lw)]
        mid = [cm[C // 2:C // 2 + 1, :] for cm in cum]
        total = [jnp.where(d == 0, cm[C - 1:C, :], cm[0:1, :]) for cm in cum]
        e_in = [jnp.exp(cm - m) for cm, m in zip(cum, mid)]
        e_in_prev = [jnp.exp(cm - m) for cm, m in zip(cum_prev, mid)]
        e_out = [jnp.exp(m - cm) for cm, m in zip(cum, mid)]
        e_tail = [jnp.exp(t - cm) for cm, t in zip(cum, total)]
        a1s = [_stack_heads(x * e, head0) for x, e in zip(kkn, e_in_prev)]
        r1s = [_stack_heads(x * e, head0) for x, e in zip(rr, e_in)]
        b1 = [x * e for x, e in zip(bb, e_out)]
        k1 = [x * e for x, e in zip(kd, e_out)]
        a0s = [_stack_heads(x * jnp.exp(cm), head0) for x, cm in zip(kkn, cum_prev)]
        vs = [_stack_heads(x, head0) for x in vv]
        g_t = [_dot_nt(jnp.concatenate([b, k], axis=0), a) for a, b, k in zip(a1s, b1, k1)]
        strict_s = strict_t[:C] + strict_t[C:]
        eye_s = eye[:C] + eye[C:]
        stack = lambda z: _stack_heads(z, head0)
        l_ab_s = [g_[:C] * strict_s for g_ in g_t]
        l_ak_t = [stack(g_[C:] * strict_s) for g_ in g_t]
        g_r = [_dot_nt(a, jnp.concatenate([b, b, k, k], axis=0)) for a, b, k in zip(r1s, b1, k1)]
        m_rb = [g_[:, :PAIR_W] * incl for g_ in g_r]
        m_rk = [g_[:, PAIR_W:] * incl for g_ in g_r]
        lakv_t = [_dot_tn(a, b) for a, b in zip(vs, l_ak_t)]
        mv = [_unstack_heads(_dot(a, b)) for a, b in zip(m_rk, vs)]
        kv = [_dot_tn(a, b * e) * bd_f for a, b, e in zip(vv, kd, e_tail)]
        x = [eye_s - l_ for l_ in l_ab_s]
        q = [_dot(l_, stack(l_)) for l_ in l_ab_s]
        step = 2
        while 2 * step < C:
            prod = [_dot(jnp.concatenate([xx, qq], axis=0), stack(qq)) for xx, qq in zip(x, q)]
            x = [xx + pr[:C] for xx, pr in zip(x, prod)]
            q = [pr[C:] for pr in prod]
            step *= 2
        x = [stack(xx + _dot(xx, stack(qq))) for xx, qq in zip(x, q)]
        a_hat_t = [_dot_tn(a, xx) for xx, a in zip(x, a0s)]
        y_hat_t = [_dot(a, xx) for xx, a in zip(x, lakv_t)]
        for p in pairs:
            ahat_ref[i, p] = a_hat_t[p].astype(BF16)
            yhat_ref[i, p] = y_hat_t[p]
            mrb_ref[i, p] = m_rb[p].astype(BF16)
            r0_ref[i, p] = (rr[p] * jnp.exp(cum[p])).astype(BF16)
            b2s_ref[i, p] = _stack_heads(bb[p] * e_tail[p], head0).astype(BF16)
            kv_ref[i, p] = kv[p]
            mv_ref[i, p] = mv[p]
            dec_ref[i, p] = jnp.broadcast_to(jnp.exp(total[p]), (8, PAIR_W))
        return carry

    lax.fori_loop(0, WKV_GROUP, phase_a, 0, unroll=2)

    def phase_b(i, carry):
        rows = chunk_rows(i)
        s_old = [s_ref[p] for p in pairs]
        s_bf = [s.astype(BF16) for s in s_old]
        yt = [(_dot(s_bf[p], ahat_ref[i, p]) + yhat_ref[i, p]).astype(BF16) for p in pairs]
        for p in pairs:
            s_ref[p] = dec_ref[i, p, 0:1, :] * s_old[p] + kv_ref[i, p] - _dot(yt[p], b2s_ref[i, p])
        o = [_dot_nt(r0_ref[i, p], s_bf[p]) + mv_ref[i, p] - _unstack_heads(_dot_nt(mrb_ref[i, p], yt[p]))
             for p in pairs]
        for p in pairs:
            o_ref[0, rows, cols[p]] = o[p]
        return carry

    lax.fori_loop(0, WKV_GROUP, phase_b, 0, unroll=True)

    if has_state_out:
        @pl.when(g == pl.num_programs(2) - 1)
        def _():
            for p in pairs:
                s = s_ref[p]
                sout_ref[2 * p] = s[:HEAD_DIM, :HEAD_DIM]
                sout_ref[2 * p + 1] = s[HEAD_DIM:, HEAD_DIM:]


def _wkv_call(proj, lp, l, s0, n_seq, seq_len, has_state_out):
    n_groups = seq_len // WKV_ROWS
    has_init = s0 is not None

    def group(d, s, g):
        return s * n_groups + g + d * (n_groups - 1 - 2 * g)

    def col(cb):
        return lambda d, s, g: (group(d, s, g), cb)

    per_dir_mat = pl.BlockSpec((None, None, LORA_W, D_RWKV), lambda d, s, g: (l, d, 0, 0))
    per_dir_vec = pl.BlockSpec((None, None, 1, D_RWKV), lambda d, s, g: (l, d, 0, 0))
    shared = pl.BlockSpec((None, 1, D_RWKV), lambda d, s, g: (l, 0, 0))
    in_specs = [
        pl.BlockSpec((WKV_ROWS, D_RWKV), col(0)),
        pl.BlockSpec((WKV_ROWS, D_RWKV), col(1)),
        pl.BlockSpec((WKV_ROWS, D_RWKV), col(2)),
        pl.BlockSpec((WKV_ROWS, TN_IN), col(LO_BLOCK)),
        per_dir_mat, per_dir_vec, per_dir_mat, per_dir_vec, shared, shared,
    ]
    args = [proj, proj, proj, proj, lp["w_up"], lp["w0"], lp["a_up"], lp["a0"], lp["k_k"], lp["k_a"]]
    if has_init:
        in_specs.append(pl.BlockSpec((None, None, None, N_HEADS, HEAD_DIM, HEAD_DIM),
                                     lambda d, s, g: (s, l, d, 0, 0, 0)))
        args.append(s0)
    out_shape = [jax.ShapeDtypeStruct((2, n_seq * seq_len, D_RWKV), F32)]
    out_specs = [pl.BlockSpec((1, WKV_ROWS, D_RWKV), lambda d, s, g: (d, group(d, s, g), 0))]
    if has_state_out:
        out_shape.append(jax.ShapeDtypeStruct((n_seq, 2, N_HEADS, HEAD_DIM, HEAD_DIM), F32))
        out_specs.append(pl.BlockSpec((None, None, N_HEADS, HEAD_DIM, HEAD_DIM), lambda d, s, g: (s, d, 0, 0, 0)))
    kern = functools.partial(_wkv_kernel, has_init=has_init, has_state_out=has_state_out)
    per_chunk = (WKV_GROUP, N_PAIRS)
    return pl.pallas_call(
        kern,
        out_shape=out_shape,
        grid=(2, n_seq, n_groups),
        in_specs=in_specs,
        out_specs=out_specs,
        scratch_shapes=[
            pltpu.VMEM((N_PAIRS, PAIR_W, PAIR_W), F32),
            pltpu.VMEM((WKV_ROWS, D_RWKV), F32),
            pltpu.VMEM((WKV_ROWS, D_RWKV), F32),
            pltpu.VMEM((WKV_ROWS, D_RWKV), F32),
            pltpu.VMEM((WKV_ROWS, D_RWKV), F32),
            pltpu.VMEM((WKV_ROWS, D_RWKV), F32),
            pltpu.VMEM((4, PAIR_W, PAIR_W), F32),
            pltpu.VMEM(per_chunk + (PAIR_W, PAIR_W), BF16),
            pltpu.VMEM(per_chunk + (PAIR_W, PAIR_W), F32),
            pltpu.VMEM(per_chunk + (PAIR_W, PAIR_W), BF16),
            pltpu.VMEM(per_chunk + (CHUNK, PAIR_W), BF16),
            pltpu.VMEM(per_chunk + (PAIR_W, PAIR_W), BF16),
            pltpu.VMEM(per_chunk + (PAIR_W, PAIR_W), F32),
            pltpu.VMEM(per_chunk + (CHUNK, PAIR_W), F32),
            pltpu.VMEM(per_chunk + (8, PAIR_W), F32),
        ],
        compiler_params=_cparams(("parallel", "parallel", "arbitrary")),
        name="wkv",
    )(*args)


def _rwkvpost_kernel(o_ref, r_ref, k_ref, v_ref, lo_ref, aup_ref, a0_ref, ka_ref, rk_ref, lng_ref, lnb_ref,
                     gup_ref, out_ref):
    lo = lo_ref[...]
    xa = lo[:, LORA_W:LORA_W + LORA_A]
    sg = _sigmoid(lo[:, LORA_W + LORA_A:LORA_W + LORA_A + LORA_G_PAD])
    gi = lax.broadcasted_iota(jnp.int32, (PAIR_W, PAIR_W), 0)
    gj = lax.broadcasted_iota(jnp.int32, (PAIR_W, PAIR_W), 1)
    head_sum = jnp.where((gi >= HEAD_DIM) == (gj >= HEAD_DIM), 1.0, 0.0).astype(BF16)
    cols = [slice(p * PAIR_W, (p + 1) * PAIR_W) for p in range(N_PAIRS)]
    gate = _dot(sg, gup_ref[...])
    ic = _sigmoid(a0_ref[0] + _dot(xa, aup_ref[0])) + _sigmoid(a0_ref[1] + _dot(xa, aup_ref[1]))
    kd_sum = k_ref[...] * (2.0 + (ic - 2.0) * ka_ref[...])
    bterm = r_ref[...] * kd_sum * rk_ref[...]
    o = [o_ref[0, :, c] + o_ref[1, :, c] for c in cols]
    mu = [_dot2(x, head_sum) * (1.0 / HEAD_DIM) for x in o]
    oc = [x - m for x, m in zip(o, mu)]
    var = [_dot(x * x, head_sum) * (1.0 / HEAD_DIM) for x in oc]
    bonus = [_dot(bterm[:, c], head_sum) * v_ref[:, c] for c in cols]
    for c, x, vr, bn in zip(cols, oc, var, bonus):
        y = x * lax.rsqrt(vr + GN_EPS) * lng_ref[:, c] + lnb_ref[:, c]
        out_ref[:, c] = ((y + bn) * gate[:, c]).astype(BF16)


def _rwkvpost_call(o, proj, lp, l):
    n_rows = o.shape[1]

    def col(cb):
        return lambda i: (i, cb)

    shared = pl.BlockSpec((None, 1, D_RWKV), lambda i: (l, 0, 0))
    return pl.pallas_call(
        _rwkvpost_kernel,
        out_shape=jax.ShapeDtypeStruct((n_rows, D_RWKV), BF16),
        grid=(n_rows // TM_POST,),
        in_specs=[
            pl.BlockSpec((2, TM_POST, D_RWKV), lambda i: (0, i, 0)),
            pl.BlockSpec((TM_POST, D_RWKV), col(0)),
            pl.BlockSpec((TM_POST, D_RWKV), col(1)),
            pl.BlockSpec((TM_POST, D_RWKV), col(2)),
            pl.BlockSpec((TM_POST, TN_IN), col(LO_BLOCK)),
            pl.BlockSpec((None, 2, LORA_A, D_RWKV), lambda i: (l, 0, 0, 0)),
            pl.BlockSpec((None, 2, 1, D_RWKV), lambda i: (l, 0, 0, 0)),
            shared, shared, shared, shared,
            pl.BlockSpec((None, LORA_G_PAD, D_RWKV), lambda i: (l, 0, 0)),
        ],
        out_specs=pl.BlockSpec((TM_POST, D_RWKV), lambda i: (i, 0)),
        compiler_params=_cparams(("parallel",)),
        name="rwkvpost",
    )(o, proj, proj, proj, proj, lp["a_up"], lp["a0"], lp["k_a"], lp["r_k"], lp["ln_g"], lp["ln_b"], lp["g_up"])


def _dft_mats(L):
    idx = (np.arange(L)[:, None] * np.arange(L)[None, :]) % (2 * L)
    ang = np.pi * idx.astype(np.float64) / L
    alt = np.cos(np.pi * np.arange(L))
    fc = np.cos(ang)
    fs = -np.sin(ang)
    fs[0, :] = alt
    fwd = np.concatenate([fc, fs], axis=0)
    ic = 2.0 * np.cos(ang.T)
    ic[:, 0] = 1.0
    isn = -2.0 * np.sin(ang.T)
    isn[:, 0] = alt
    inv = np.concatenate([ic, isn], axis=1) / (2 * L)
    return fwd.astype(np.float32), inv.astype(np.float32)


def _filter_features(L):
    t = np.linspace(0.0, 1.0, L, dtype=np.float32)[:, None]
    w = (2.0 * math.pi / L) * np.arange(L, dtype=np.float32)[:, None]
    f = np.linspace(1e-4, FILT_BANDS - 1, FILT_BANDS, dtype=np.float32)[None, :]
    z = np.concatenate([t, np.cos(f * w), -np.sin(f * w)], axis=-1)
    zp = np.zeros((L, FILT_EMB_PAD), np.float32)
    zp[:, :FILT_EMB] = z
    return zp, t


def _filter_deltas():
    max_decay = math.log(FILT_TARGET) / FAST_DECAY_PCT
    min_decay = math.log(FILT_TARGET) / SLOW_DECAY_PCT
    return np.abs(np.linspace(min_decay, max_decay, D_HYENA, dtype=np.float32))[None, :]


def _hyfilt_kernel(z_ref, t_ref, dl_ref, fwd_ref, w1_ref, b1_ref, w2_ref, b2_ref,
                   w3f0_ref, w3b0_ref, w3f1_ref, w3b1_ref, fr_ref, o_ref):
    L = z_ref.shape[0]
    tc = o_ref.shape[3]
    h = jnp.sin(fr_ref[0:1, :] * (_dot3(z_ref[...], w1_ref[...]) + b1_ref[...]))
    h = jnp.sin(fr_ref[1:2, :] * (_dot3(h, w2_ref[...]) + b2_ref[...]))
    decay = jnp.exp(-t_ref[...] * dl_ref[...])
    first = lax.broadcasted_iota(jnp.int32, (L, tc), 0) == 0
    fwd = fwd_ref[...]
    for order, (wf_ref, wb_ref) in enumerate(((w3f0_ref, w3b0_ref), (w3f1_ref, w3b1_ref))):
        hf = _dot3(h, wf_ref[...]) * decay
        hb = _dot3(h, wb_ref[...]) * decay
        norm = jnp.sum(jnp.abs(hf) + jnp.abs(hb), axis=0, keepdims=True)
        hf = hf / norm
        hb = jnp.where(first, 0.0, hb / norm)
        ks = _dot(fwd, hf + hb)
        kd = _dot(fwd, hf - hb)
        kr = ks[:L]
        o_ref[order, 0] = kr
        o_ref[order, 1] = jnp.where(first, 0.0, kd[L:])
        o_ref[order, 2] = jnp.where(first, ks[L:L + 1], kr)


def _hyfilt_call(L, fp, l):
    z, t = _filter_features(L)
    fwd, _ = _dft_mats(L)
    nct = D_HYENA // TC_FILT
    full = lambda shape: pl.BlockSpec(shape, lambda j: tuple(0 for _ in shape))
    layer = lambda shape: pl.BlockSpec((None,) + shape, lambda j: (l,) + tuple(0 for _ in shape))
    w3_spec = lambda grp: pl.BlockSpec((None, FILT_HIDDEN, TC_FILT), lambda j: (l, 0, grp * nct + j))
    w3 = fp["w3"]
    return pl.pallas_call(
        _hyfilt_kernel,
        out_shape=jax.ShapeDtypeStruct((2, 3, L, D_HYENA), F32),
        grid=(nct,),
        in_specs=[
            full((L, FILT_EMB_PAD)), full((L, 1)), pl.BlockSpec((1, TC_FILT), lambda j: (0, j)), full((2 * L, L)),
            layer((FILT_EMB_PAD, FILT_HIDDEN)), layer((1, FILT_HIDDEN)),
            layer((FILT_HIDDEN, FILT_HIDDEN)), layer((1, FILT_HIDDEN)),
            w3_spec(0), w3_spec(1), w3_spec(2), w3_spec(3),
            layer((2, FILT_HIDDEN)),
        ],
        out_specs=pl.BlockSpec((2, 3, L, TC_FILT), lambda j: (0, 0, 0, j)),
        compiler_params=_cparams(("parallel",)),
        name="hyfilt",
    )(jnp.asarray(z), jnp.asarray(t), jnp.asarray(_filter_deltas()), jnp.asarray(fwd).astype(BF16),
      fp["w1"], fp["b1"], fp["w2"], fp["b2"], w3, w3, w3, w3, fp["freq"])


def _hyena_kernel(u_ref, x1_ref, x2_ref, fwd_ref, inv_ref, spec_ref, bias_ref, out_ref):
    L = fwd_ref.shape[1]
    n_sub = u_ref.shape[0] // L

    def side_by_side(ref):
        return jnp.concatenate([ref[s * L:(s + 1) * L, :] for s in range(n_sub)], axis=1)

    def tiled(x):
        return jnp.concatenate([x] * n_sub, axis=1)

    def long_conv(u, order):
        spec = _dot(fwd_ref[...], u)
        ur, ui = spec[:L], spec[L:]
        kr, ki, kr2 = tiled(spec_ref[order, 0]), tiled(spec_ref[order, 1]), tiled(spec_ref[order, 2])
        yr = ur * kr - ui * ki
        yi = ur * ki + ui * kr2
        y = _dot(inv_ref[...], jnp.concatenate([yr, yi], axis=0))
        return y + u * tiled(bias_ref[order:order + 1, :])

    z = side_by_side(x1_ref) * long_conv(side_by_side(u_ref), 0)
    out = (side_by_side(x2_ref) * long_conv(z, 1)).astype(BF16)
    tc = out_ref.shape[1]
    for s in range(n_sub):
        out_ref[s * L:(s + 1) * L, :] = out[:, s * tc:(s + 1) * tc]


def _hyena_call(proj, spec, bias, l, n_seq, L, tc, n_sub):
    fwd, inv = _dft_mats(L)
    nct = D_HYENA // tc
    rows = n_sub * L
    cb0 = 3 * D_RWKV // tc

    def col(which):
        return lambda s, j: (s, cb0 + which * nct + j)

    in_specs = [
        pl.BlockSpec((rows, tc), col(0)),
        pl.BlockSpec((rows, tc), col(1)),
        pl.BlockSpec((rows, tc), col(2)),
        pl.BlockSpec((2 * L, L), lambda s, j: (0, 0), pipeline_mode=pl.Buffered(1)),
        pl.BlockSpec((L, 2 * L), lambda s, j: (0, 0), pipeline_mode=pl.Buffered(1)),
        pl.BlockSpec((2, 3, L, tc), lambda s, j: (0, 0, 0, j)),
        pl.BlockSpec((None, 2, tc), lambda s, j: (l, 0, j)),
    ]
    return pl.pallas_call(
        _hyena_kernel,
        out_shape=jax.ShapeDtypeStruct((n_seq * L, D_HYENA), BF16),
        grid=(n_seq // n_sub, nct),
        in_specs=in_specs,
        out_specs=pl.BlockSpec((rows, tc), lambda s, j: (s, j)),
        compiler_params=_cparams(("parallel", "parallel")),
        name="hyena",
    )(proj, proj, proj, jnp.asarray(fwd).astype(BF16), jnp.asarray(inv).astype(BF16), spec, bias)


def _outproj_kernel(a_ref, b_ref, x_ref, g_ref, w_ref, lg_ref, lb_ref, o_ref):
    mix = (jnp.dot(a_ref[...], w_ref[0:D_RWKV, :], preferred_element_type=F32)
           + jnp.dot(b_ref[...], w_ref[D_RWKV:, :], preferred_element_type=F32))
    y = ALPHA * x_ref[...] + g_ref[0, 0] * mix
    o_ref[...] = _layer_norm_rows(y) * lg_ref[...] + lb_ref[...]


def _outproj_call(a_out, b_out, x, mod_all, w_out, ln_g, ln_b, l, is_grid):
    grp = lambda i: _cond_row(i, TM_OUT, is_grid)
    vec = pl.BlockSpec((None, 1, D_MODEL), lambda i: (l, 0, 0))
    return pl.pallas_call(
        _outproj_kernel,
        out_shape=jax.ShapeDtypeStruct(x.shape, F32),
        grid=(x.shape[0] // TM_OUT,),
        in_specs=[
            pl.BlockSpec((TM_OUT, D_RWKV), lambda i: (i, 0)),
            pl.BlockSpec((TM_OUT, D_HYENA), lambda i: (i, 0)),
            pl.BlockSpec((TM_OUT, D_MODEL), lambda i: (i, 0)),
            _mod_spec(l, 2, grp),
            pl.BlockSpec((None, D_MODEL, D_MODEL), lambda i: (l, 0, 0)),
            vec, vec,
        ],
        out_specs=pl.BlockSpec((TM_OUT, D_MODEL), lambda i: (i, 0)),
        compiler_params=_cparams(("parallel",)),
        name="outproj",
    )(a_out, b_out, x, mod_all, w_out, ln_g, ln_b)


def _mlp_kernel(x_ref, sh_ref, sc_ref, g_ref, w1_ref, w2_ref, lg_ref, lb_ref, o_ref, h_ref):
    f = pl.program_id(1)
    row_blocks = o_ref.shape[0] // 128

    @pl.when(f == 0)
    def _():
        sc = 1.0 + sc_ref[0, 0]
        sh = sh_ref[0, 0]

        def body(rb, carry):
            rows = pl.ds(pl.multiple_of(rb * 128, 128), 128)
            h_ref[rows, :] = (_layer_norm_rows(x_ref[rows, :]) * sc + sh).astype(BF16)
            return carry

        lax.fori_loop(0, row_blocks, body, 0)
        o_ref[...] = jnp.zeros(o_ref.shape, F32)

    hid = jnp.maximum(jnp.dot(h_ref[...], w1_ref[...].astype(BF16), preferred_element_type=F32), 0.0)
    hid = (hid * hid).astype(BF16)
    o_ref[...] += jnp.dot(hid, w2_ref[...].astype(BF16), preferred_element_type=F32)

    @pl.when(f == pl.num_programs(1) - 1)
    def _():
        gate = g_ref[0, 0]
        lg = lg_ref[...]
        lb = lb_ref[...]

        def body(rb, carry):
            rows = pl.ds(pl.multiple_of(rb * 128, 128), 128)
            y = ALPHA * x_ref[rows, :] + gate * o_ref[rows, :]
            o_ref[rows, :] = _layer_norm_rows(y) * lg + lb
            return carry

        lax.fori_loop(0, row_blocks, body, 0)


def _mlp_call(x, mod_all, w1, w2, ln_g, ln_b, l, is_grid):
    grp = lambda i: _cond_row(i, TM_MLP, is_grid)
    vec = pl.BlockSpec((None, 1, D_MODEL), lambda i, f: (l, 0, 0))
    return pl.pallas_call(
        _mlp_kernel,
        out_shape=jax.ShapeDtypeStruct(x.shape, F32),
        grid=(x.shape[0] // TM_MLP, D_FF // TF_MLP),
        in_specs=[
            pl.BlockSpec((TM_MLP, D_MODEL), lambda i, f: (i, 0), pipeline_mode=pl.Buffered(1)),
            _mod_spec(l, 3, grp),
            _mod_spec(l, 4, grp),
            _mod_spec(l, 5, grp),
            pl.BlockSpec((None, D_MODEL, TF_MLP), lambda i, f: (l, 0, f)),
            pl.BlockSpec((None, TF_MLP, D_MODEL), lambda i, f: (l, f, 0)),
            vec, vec,
        ],
        out_specs=pl.BlockSpec((TM_MLP, D_MODEL), lambda i, f: (i, 0)),
        scratch_shapes=[pltpu.VMEM((TM_MLP, D_MODEL), BF16)],
        compiler_params=_cparams(("parallel", "arbitrary")),
        name="mlp",
    )(x, mod_all, mod_all, mod_all, w1, w2, ln_g, ln_b)


def kernel(x_prompt, x_sample, c, state_rwkv, c_ctx, w_ada, b_ada, w_in, conv_w, lora_w_up, lora_w0, lora_a_up, lora_a0, lora_g_up, rwkv_k_k, rwkv_k_a, rwkv_r_k, rwkv_ln_g, rwkv_ln_b, filt_w1, filt_b1, filt_w2, filt_b2, filt_w3, filt_freq, hyena_bias, w_out, ln1_g, ln1_b, ln2_g, ln2_b, mlp_w1, mlp_w2):
    n_ctx, ctx_seq, _ = x_prompt.shape
    n_lat, lat_seq, _ = x_sample.shape
    depth = w_ada.shape[0]
    assert ctx_seq == CTX_SEQ and lat_seq == LAT_SEQ and (n_ctx * ctx_seq) % TM == 0
    assert 1 + n_lat <= N_COND
    ctx_rows = n_ctx * ctx_seq
    lat_rows = n_lat * lat_seq

    xc = x_prompt.reshape(ctx_rows, D_MODEL)
    xl = x_sample.reshape(lat_rows, D_MODEL)
    cond = jnp.concatenate([c_ctx[None, :], c, jnp.zeros((N_COND - 1 - n_lat, D_MODEL), F32)], axis=0)
    mod_all = _mod_call(cond, w_ada, b_ada).reshape(depth, N_COND, 6, 1, D_MODEL)

    w_in_t = jnp.swapaxes(w_in, 1, 2)
    w_lo_bf = jnp.pad(w_in[:, :, N_CONV:].astype(BF16), ((0, 0), (0, 0), (0, TN_IN - (D_IN - N_CONV))))
    conv_w9 = conv_w.reshape(depth, 9, N_CONV)
    w_out_bf = w_out.astype(BF16)
    row = lambda a: a.reshape(depth, 1, a.shape[-1])
    lp = {
        "w_up": lora_w_up, "w0": lora_w0[:, :, None, :], "a_up": lora_a_up, "a0": lora_a0[:, :, None, :],
        "k_k": row(rwkv_k_k), "k_a": row(rwkv_k_a), "r_k": rwkv_r_k.reshape(depth, 1, D_RWKV),
        "ln_g": row(rwkv_ln_g), "ln_b": row(rwkv_ln_b),
        "g_up": jnp.pad(lora_g_up, ((0, 0), (0, LORA_G_PAD - LORA_G), (0, 0))),
    }
    fp = {
        "w1": jnp.pad(filt_w1, ((0, 0), (0, FILT_EMB_PAD - FILT_EMB), (0, 0))), "b1": row(filt_b1),
        "w2": filt_w2, "b2": row(filt_b2), "w3": filt_w3, "freq": filt_freq,
    }
    ln1_g, ln1_b, ln2_g, ln2_b = row(ln1_g), row(ln1_b), row(ln2_g), row(ln2_b)
    state_in = state_rwkv.astype(F32)

    def trunk_layer(x, l, is_grid, n_seq, seq_len, s0, hyena_tc, hyena_sub):
        proj = _inproj_call(x, mod_all, w_in_t, w_lo_bf, conv_w9, l, is_grid)
        scan = _wkv_call(proj, lp, l, s0, n_seq, seq_len, s0 is None)
        a_out = _rwkvpost_call(scan[0], proj, lp, l)
        spec = _hyfilt_call(seq_len, fp, l)
        b_out = _hyena_call(proj, spec, hyena_bias, l, n_seq, seq_len, hyena_tc, hyena_sub)
        x = _outproj_call(a_out, b_out, x, mod_all, w_out_bf, ln1_g, ln1_b, l, is_grid)
        x = _mlp_call(x, mod_all, mlp_w1, mlp_w2, ln2_g, ln2_b, l, is_grid)
        return x, (scan[1] if s0 is None else None)

    ctx_states = []
    lat_sub = 2 if n_lat % 2 == 0 else 1
    for l in range(depth):
        xc, s_ctx = trunk_layer(xc, l, False, n_ctx, ctx_seq, None, D_HYENA, 1)
        ctx_states.append(s_ctx)
        xl, _ = trunk_layer(xl, l, True, n_lat, lat_seq, state_in, 256, lat_sub)

    y_prompt = xc.reshape(n_ctx, ctx_seq, D_MODEL)
    y_sample = xl.reshape(n_lat, lat_seq, D_MODEL)
    new_state = jnp.stack(ctx_states, axis=1).astype(x_prompt.dtype)
    return (y_prompt, y_sample, new_state)
```

```python
import functools
import math

import jax
import jax.numpy as jnp
import numpy as np
from jax import lax
from jax.experimental import pallas as pl
from jax.experimental.pallas import tpu as pltpu

F32 = jnp.float32
BF16 = jnp.bfloat16

D_MODEL = 2048
D_RWKV = 1024
D_HYENA = 1024
HEAD_DIM = 64
N_HEADS = D_RWKV // HEAD_DIM
N_PAIRS = N_HEADS // 2
PAIR_W = 2 * HEAD_DIM
LORA_W = 64
LORA_A = 64
LORA_G = 160
LORA_G_PAD = 256
N_CONV = 3 * D_RWKV + 3 * D_HYENA
D_IN = N_CONV + LORA_W + LORA_A + LORA_G
D_FF = 4 * D_MODEL
GRID_W = 64
CTX_SEQ = 256
LAT_SEQ = 1024
FILT_BANDS = 16
FILT_EMB = 1 + 2 * FILT_BANDS
FILT_EMB_PAD = 128
FILT_HIDDEN = 64
DEPTH = 2
ALPHA = (2 * DEPTH) ** 0.25
LN_EPS = 1e-5
GN_EPS = 64e-5
FILT_TARGET = 1e-2
FAST_DECAY_PCT = 0.3
SLOW_DECAY_PCT = 1.5
DECAY_SCALE = math.exp(-0.5)

TM = 1024
TN_IN = 512
NJ_CONV = N_CONV // TN_IN
D_IN_PAD = (NJ_CONV + 1) * TN_IN
LO_BLOCK = N_CONV // TN_IN
CONV_PAD = 72
CHUNK = 64
WKV_GROUP = 4
WKV_ROWS = WKV_GROUP * CHUNK
WKV_PAR = 2
N_COND = 8
TN_MOD = 1024
TM_OUT = 512
TM_MLP = 1024
TF_MLP = 512
TM_POST = 256
TC_FILT = 512
VMEM_LIMIT = 56 * 1024 * 1024


def _cparams(sem):
    return pltpu.CompilerParams(dimension_semantics=sem, vmem_limit_bytes=VMEM_LIMIT)


def _dot(a, b):
    return jnp.dot(a.astype(BF16), b.astype(BF16), preferred_element_type=F32)


def _dot_nt(a, b):
    return lax.dot_general(a.astype(BF16), b.astype(BF16), (((1,), (1,)), ((), ())),
                           preferred_element_type=F32)


def _dot_tn(a, b):
    return lax.dot_general(a.astype(BF16), b.astype(BF16), (((0,), (0,)), ((), ())),
                           preferred_element_type=F32)


def _split(x):
    hi = x.astype(BF16)
    lo = (x - hi.astype(F32)).astype(BF16)
    return hi, lo


def _dot3(a, b):
    ah, al = _split(a)
    bh, bl = _split(b)
    return _dot(ah, bh) + (_dot(ah, bl) + _dot(al, bh))


def _dot2(a, b_exact):
    ah, al = _split(a)
    return _dot(ah, b_exact) + _dot(al, b_exact)


def _dot2_left(a_exact, b):
    bh, bl = _split(b)
    return _dot(a_exact, bh) + _dot(a_exact, bl)


def _sigmoid(x):
    return 1.0 / (1.0 + jnp.exp(-x))


def _layer_norm_rows(x):
    mu = jnp.mean(x, axis=-1, keepdims=True)
    xc = x - mu
    var = jnp.mean(xc * xc, axis=-1, keepdims=True)
    return xc * lax.rsqrt(var + LN_EPS)


def _cond_row(i, tm, is_grid):
    return 1 + (i * tm) // LAT_SEQ if is_grid else 0


def _mod_spec(l, which, grp):
    return pl.BlockSpec((None, 1, 1, 1, D_MODEL), lambda i, *_: (l, grp(i), which, 0, 0))


def _mod_kernel(c_ref, w_ref, b_ref, o_ref):
    c = c_ref[...]
    s = c * _sigmoid(c)
    o_ref[0] = _dot(s, w_ref[0]) + b_ref[0]


def _mod_call(cond, w_ada, b_ada):
    depth = w_ada.shape[0]
    n_out = w_ada.shape[2]
    return pl.pallas_call(
        _mod_kernel,
        out_shape=jax.ShapeDtypeStruct((depth, N_COND, n_out), F32),
        grid=(depth, n_out // TN_MOD),
        in_specs=[
            pl.BlockSpec((N_COND, D_MODEL), lambda l, j: (0, 0)),
            pl.BlockSpec((1, D_MODEL, TN_MOD), lambda l, j: (l, 0, j)),
            pl.BlockSpec((1, 1, TN_MOD), lambda l, j: (l, 0, j)),
        ],
        out_specs=pl.BlockSpec((1, N_COND, TN_MOD), lambda l, j: (l, 0, j)),
        compiler_params=_cparams(("parallel", "parallel")),
        name="mod",
    )(cond, w_ada, b_ada.reshape(depth, 1, n_out))


def _inproj_kernel(x_ref, sh_ref, sc_ref, w_ref, wlo_ref, cw_ref, o_ref, h_ref, wbf_ref, acc_a, acc_b, *,
                   is_grid):
    j = pl.program_id(1)
    tn = o_ref.shape[1]
    mid = slice(CONV_PAD, CONV_PAD + TM)

    n_blocks = TM // GRID_W
    n_parts = 4
    lane_w = 128
    n_chunk = 128

    def round_weights():
        for nc in range(tn // n_chunk):
            rows = slice(nc * n_chunk, (nc + 1) * n_chunk)
            wbf_ref[rows, :] = w_ref[rows, :].astype(BF16)

    def matmul_into(dst_ref, part=None):
        if part is None:
            dst_ref[mid, :] = _dot_nt(h_ref[...], wbf_ref[...])
        else:
            r0, r1 = part * (TM // n_parts), (part + 1) * (TM // n_parts)
            dst_ref[CONV_PAD + r0:CONV_PAD + r1, :] = _dot_nt(h_ref[r0:r1, :], wbf_ref[...])

    def conv_from(src_ref, g0=0, g1=n_blocks):
        row = lax.broadcasted_iota(jnp.int32, (GRID_W, lane_w), 0)
        first = row == 0
        last = row == GRID_W - 1
        per_seq = CTX_SEQ // GRID_W

        def shifted(s, lanes, mask_first, mask_last):
            start = CONV_PAD + s * GRID_W
            uc = src_ref[start:start + GRID_W, lanes]
            ul = src_ref[start - 1:start - 1 + GRID_W, lanes]
            ur = src_ref[start + 1:start + 1 + GRID_W, lanes]
            if mask_first:
                ul = jnp.where(first, 0.0, ul)
            if mask_last:
                ur = jnp.where(last, 0.0, ur)
            return ul, uc, ur

        for c in range(tn // lane_w):
            lanes = slice(c * lane_w, (c + 1) * lane_w)
            cw = cw_ref[:, lanes]
            if not is_grid:
                for s in range(g0, g1):
                    ul, uc, ur = shifted(s, lanes, s % per_seq == 0, s % per_seq == per_seq - 1)
                    o_ref[s * GRID_W:(s + 1) * GRID_W, lanes] = ul * cw[3:4, :] + uc * cw[4:5, :] + ur * cw[5:6, :]
                continue
            partial = {}
            for s in range(max(g0 - 1, 0), min(g1 + 1, n_blocks)):
                ul, uc, ur = shifted(s, lanes, True, True)
                for a in range(3):
                    gi = s - (a - 1)
                    if g0 <= gi < g1:
                        t = (ul * cw[3 * a:3 * a + 1, :] + uc * cw[3 * a + 1:3 * a + 2, :]
                             + ur * cw[3 * a + 2:3 * a + 3, :])
                        partial[gi] = t if gi not in partial else partial[gi] + t
                if s - 1 in partial:
                    o_ref[(s - 1) * GRID_W:s * GRID_W, lanes] = partial.pop(s - 1)
            for gi in sorted(partial):
                o_ref[gi * GRID_W:(gi + 1) * GRID_W, lanes] = partial.pop(gi)

    @pl.when(j == 0)
    def _():
        sc = 1.0 + sc_ref[0, 0]
        sh = sh_ref[0, 0]

        def body(rb, carry):
            rows = pl.ds(pl.multiple_of(rb * 128, 128), 128)
            h = _layer_norm_rows(x_ref[rows, :]) * sc + sh
            h_ref[rows, :] = h.astype(BF16)
            return carry

        lax.fori_loop(0, TM // 128, body, 0)
        for acc in (acc_a, acc_b):
            acc[0:CONV_PAD, :] = jnp.zeros((CONV_PAD, tn), F32)
            acc[CONV_PAD + TM:, :] = jnp.zeros((CONV_PAD, tn), F32)
        o_ref[...] = jnp.dot(h_ref[...], wlo_ref[...], preferred_element_type=F32)

    @pl.when(j == 1)
    def _():
        round_weights()
        matmul_into(acc_a)

    main = jnp.logical_and(j >= 2, j <= NJ_CONV)
    odd = jnp.bitwise_and(j, 1) == 1

    def overlapped(src_ref, dst_ref):
        round_weights()
        per_part = n_blocks // n_parts
        for part in range(n_parts):
            matmul_into(dst_ref, part)
            conv_from(src_ref, part * per_part, (part + 1) * per_part)

    @pl.when(jnp.logical_and(main, jnp.logical_not(odd)))
    def _():
        overlapped(acc_a, acc_b)

    @pl.when(jnp.logical_and(main, odd))
    def _():
        overlapped(acc_b, acc_a)

    @pl.when(j == NJ_CONV + 1)
    def _():
        conv_from(acc_b if NJ_CONV % 2 == 0 else acc_a)


def _inproj_call(x, mod_all, w_in, w_lo, conv_w, l, is_grid):
    n_units = x.shape[0] // TM
    kern = functools.partial(_inproj_kernel, is_grid=is_grid)
    grp = lambda i: _cond_row(i, TM, is_grid)
    tile = lambda t: jnp.clip(t, 0, NJ_CONV - 1)
    return pl.pallas_call(
        kern,
        out_shape=jax.ShapeDtypeStruct((x.shape[0], D_IN_PAD), F32),
        grid=(n_units, NJ_CONV + 2),
        in_specs=[
            pl.BlockSpec((TM, D_MODEL), lambda i, j: (i, 0)),
            _mod_spec(l, 0, grp),
            _mod_spec(l, 1, grp),
            pl.BlockSpec((None, TN_IN, D_MODEL), lambda i, j: (l, tile(j - 1), 0)),
            pl.BlockSpec((None, D_MODEL, TN_IN), lambda i, j: (l, 0, 0)),
            pl.BlockSpec((None, 9, TN_IN), lambda i, j: (l, 0, tile(j - 2))),
        ],
        out_specs=pl.BlockSpec((TM, TN_IN), lambda i, j: (i, jnp.where(j == 0, LO_BLOCK, tile(j - 2)))),
        scratch_shapes=[
            pltpu.VMEM((TM, D_MODEL), BF16),
            pltpu.VMEM((TN_IN, D_MODEL), BF16),
            pltpu.VMEM((TM + 2 * CONV_PAD, TN_IN), F32),
            pltpu.VMEM((TM + 2 * CONV_PAD, TN_IN), F32),
        ],
        compiler_params=_cparams(("parallel", "arbitrary")),
        name="inproj",
    )(x, mod_all, mod_all, w_in, w_lo, conv_w)


def _stack_heads(z, head0):
    return jnp.concatenate([jnp.where(head0, z, 0.0), jnp.where(head0, 0.0, z)], axis=0)


def _unstack_heads(z):
    c = z.shape[0] // 2
    return z[:c] + z[c:]


def _wkv_kernel(*refs, has_init, has_state_out):
    (r_ref, k_ref, v_ref, lo_ref, wup_ref, w0_ref, aup_ref, a0_ref, kk_ref, ka_ref) = refs[:10]
    pos = 10
    s0_ref = None
    if has_init:
        s0_ref = refs[pos]
        pos += 1
    o_ref = refs[pos]
    pos += 1
    sout_ref = None
    if has_state_out:
        sout_ref = refs[pos]
        pos += 1
    (s_ref, lw_ref, cum_ref, kkn_ref, b_ref, kd_ref, msk_ref,
     ahat_ref, yhat_ref, mrb_ref, r0_ref, b2s_ref, kv_ref, mv_ref, dec_ref) = refs[pos:]

    d = pl.program_id(0)
    g = pl.program_id(2)
    C = CHUNK
    pairs = range(N_PAIRS)
    cols = [slice(p * PAIR_W, (p + 1) * PAIR_W) for p in pairs]

    @pl.when(g == 0)
    def _():
        if has_init:
            zero = jnp.zeros((HEAD_DIM, HEAD_DIM), F32)
            for p in pairs:
                top = jnp.concatenate([s0_ref[2 * p], zero], axis=1)
                bottom = jnp.concatenate([zero, s0_ref[2 * p + 1]], axis=1)
                s_ref[p] = jnp.concatenate([top, bottom], axis=0)
        else:
            s_ref[...] = jnp.zeros(s_ref.shape, F32)

    sgn = 1 - 2 * d
    si = lax.broadcasted_iota(jnp.int32, (2 * C, 2 * C), 0)
    sj = lax.broadcasted_iota(jnp.int32, (2 * C, 2 * C), 1)
    same_head = (si >= C) == (sj >= C)
    dlt = (jnp.bitwise_and(si, C - 1) - jnp.bitwise_and(sj, C - 1)) * sgn
    msk_ref[0] = jnp.where(jnp.logical_and(same_head, dlt < 0), 1.0, 0.0)
    msk_ref[1] = jnp.where(jnp.logical_and(same_head, dlt >= 0), 1.0, 0.0)
    msk_ref[2] = jnp.where(si == sj, 1.0, 0.0)
    blockdiag = (si >= HEAD_DIM) == (sj >= HEAD_DIM)
    msk_ref[3] = jnp.where(blockdiag, 1.0, 0.0)
    head_sum = jnp.where(blockdiag, 1.0, 0.0).astype(BF16)

    lo = lo_ref[...]
    xw = lo[:, 0:LORA_W]
    xa = lo[:, LORA_W:LORA_W + LORA_A]
    logw = -DECAY_SCALE * _sigmoid(w0_ref[...] + _dot3(jnp.tanh(xw), wup_ref[...]))
    lw_ref[...] = logw
    gi = lax.broadcasted_iota(jnp.int32, (WKV_ROWS, WKV_ROWS), 0)
    gj = lax.broadcasted_iota(jnp.int32, (WKV_ROWS, WKV_ROWS), 1)
    same_chunk = jnp.bitwise_and(gi, -C) == jnp.bitwise_and(gj, -C)
    cum_mask = jnp.where(jnp.logical_and(same_chunk, (gi - gj) * sgn >= 0), 1.0, 0.0).astype(BF16)
    cum_ref[...] = _dot2_left(cum_mask, logw)
    iclr = _sigmoid(a0_ref[...] + _dot(xa, aup_ref[...]))
    k = k_ref[...]
    kd_ref[...] = k * (1.0 + (iclr - 1.0) * ka_ref[...])
    kkr = k * kk_ref[...]
    for p in pairs:
        slab = kkr[:, cols[p]]
        kkn = slab * lax.rsqrt(_dot(slab * slab, head_sum) + 1e-12)
        kkn_ref[:, cols[p]] = kkn
        b_ref[:, cols[p]] = kkn * iclr[:, cols[p]]

    def chunk_rows(i):
        cc = jnp.where(d == 0, i, WKV_GROUP - 1 - i)
        return pl.ds(pl.multiple_of(cc * C, C), C)

    def phase_a(it, carry):
        chunk_ids = [it * WKV_PAR + ci for ci in range(WKV_PAR)]
        units = [(ci, c) for ci in chunk_ids for c in cols]
        chunk_of = [ci for ci in chunk_ids for _ in cols]
        pair_of = [p for _ in chunk_ids for p in pairs]
        head0 = lax.broadcasted_iota(jnp.int32, (C, PAIR_W), 1) < HEAD_DIM
        strict_t = msk_ref[0]
        incl = msk_ref[1]
        eye = msk_ref[2]
        bd_f = msk_ref[3]

        lw = [lw_ref[chunk_rows(ci), c] for ci, c in units]
        cum = [cum_ref[chunk_rows(ci), c] for ci, c in units]
        kkn = [kkn_ref[chunk_rows(ci), c] for ci, c in units]
        bb = [b_ref[chunk_rows(ci), c] for ci, c in units]
        kd = [kd_ref[chunk_rows(ci), c] for ci, c in units]
        rr = [r_ref[chunk_rows(ci), c] for ci, c in units]
        vv = [v_ref[chunk_rows(ci), c] for ci, c in units]
        cum_prev = [cm - x for cm, x in zip(cum, lw)]
        mid = [cm[C // 2:C // 2 + 1, :] for cm in cum]
        total = [jnp.where(d == 0, cm[C - 1:C, :], cm[0:1, :]) for cm in cum]
        e_in = [jnp.exp(cm - m) for cm, m in zip(cum, mid)]
        e_in_prev = [jnp.exp(cm - m) for cm, m in zip(cum_prev, mid)]
        e_out = [jnp.exp(m - cm) for cm, m in zip(cum, mid)]
        e_tail = [jnp.exp(t - cm) for cm, t in zip(cum, total)]
        a1s = [_stack_heads(x * e, head0) for x, e in zip(kkn, e_in_prev)]
        r1s = [_stack_heads(x * e, head0) for x, e in zip(rr, e_in)]
        b1 = [x * e for x, e in zip(bb, e_out)]
        k1 = [x * e for x, e in zip(kd, e_out)]
        a0s = [_stack_heads(x * jnp.exp(cm), head0) for x, cm in zip(kkn, cum_prev)]
        vs = [_stack_heads(x, head0) for x in vv]
        g_t = [_dot_nt(jnp.concatenate([b, k], axis=0), a) for a, b, k in zip(a1s, b1, k1)]
        strict_s = strict_t[:C] + strict_t[C:]
        eye_s = eye[:C] + eye[C:]
        stack = lambda z: _stack_heads(z, head0)
        l_ab_s = [g_[:C] * strict_s for g_ in g_t]
        l_ak_t = [stack(g_[C:] * strict_s) for g_ in g_t]
        g_r = [_dot_nt(a, jnp.concatenate([b, b, k, k], axis=0)) for a, b, k in zip(r1s, b1, k1)]
        m_rb = [g_[:, :PAIR_W] * incl for g_ in g_r]
        m_rk = [g_[:, PAIR_W:] * incl for g_ in g_r]
        lakv_t = [_dot_tn(a, b) for a, b in zip(vs, l_ak_t)]
        mv = [_unstack_heads(_dot(a, b)) for a, b in zip(m_rk, vs)]
        kv = [_dot_tn(a, b * e) * bd_f for a, b, e in zip(vv, kd, e_tail)]
        x = [eye_s - l_ for l_ in l_ab_s]
        q = [_dot(l_, stack(l_)) for l_ in l_ab_s]
        step = 2
        while 2 * step < C:
            prod = [_dot(jnp.concatenate([xx, qq], axis=0), stack(qq)) for xx, qq in zip(x, q)]
            x = [xx + pr[:C] for xx, pr in zip(x, prod)]
            q = [pr[C:] for pr in prod]
            step *= 2
        x = [stack(xx + _dot(xx, stack(qq))) for xx, qq in zip(x, q)]
        a_hat_t = [_dot_tn(a, xx) for xx, a in zip(x, a0s)]
        y_hat_t = [_dot(a, xx) for xx, a in zip(x, lakv_t)]
        for u, (i, p) in enumerate(zip(chunk_of, pair_of)):
            ahat_ref[i, p] = a_hat_t[u].astype(BF16)
            yhat_ref[i, p] = y_hat_t[u]
            mrb_ref[i, p] = m_rb[u].astype(BF16)
            r0_ref[i, p] = (rr[u] * jnp.exp(cum[u])).astype(BF16)
            b2s_ref[i, p] = _stack_heads(bb[u] * e_tail[u], head0).astype(BF16)
            kv_ref[i, p] = kv[u]
            mv_ref[i, p] = mv[u]
            dec_ref[i, p] = jnp.broadcast_to(jnp.exp(total[u]), (8, PAIR_W))
        return carry

    def phase_b(i, s_old):
        rows = chunk_rows(i)
        s_bf = [s.astype(BF16) for s in s_old]
        yt = [(_dot(s_bf[p], ahat_ref[i, p]) + yhat_ref[i, p]).astype(BF16) for p in pairs]
        s_new = [dec_ref[i, p, 0:1, :] * s_old[p] + kv_ref[i, p] - _dot(yt[p], b2s_ref[i, p]) for p in pairs]
        o = [_dot_nt(r0_ref[i, p], s_bf[p]) + mv_ref[i, p] - _unstack_heads(_dot_nt(mrb_ref[i, p], yt[p]))
             for p in pairs]
        for p in pairs:
            o_ref[0, rows, cols[p]] = o[p]
        return s_new

    lax.fori_loop(0, WKV_GROUP // WKV_PAR, phase_a, 0)
    state = [s_ref[p] for p in pairs]
    for i in range(WKV_GROUP):
        state = phase_b(i, state)
    for p in pairs:
        s_ref[p] = state[p]

    if has_state_out:
        @pl.when(g == pl.num_programs(2) - 1)
        def _():
            for p in pairs:
                s = s_ref[p]
                sout_ref[2 * p] = s[:HEAD_DIM, :HEAD_DIM]
                sout_ref[2 * p + 1] = s[HEAD_DIM:, HEAD_DIM:]


def _wkv_call(proj, lp, l, s0, n_seq, seq_len, has_state_out):
    n_groups = seq_len // WKV_ROWS
    has_init = s0 is not None

    def group(d, s, g):
        return s * n_groups + g + d * (n_groups - 1 - 2 * g)

    def col(cb):
        return lambda d, s, g: (group(d, s, g), cb)

    per_dir_mat = pl.BlockSpec((None, None, LORA_W, D_RWKV), lambda d, s, g: (l, d, 0, 0))
    per_dir_vec = pl.BlockSpec((None, None, 1, D_RWKV), lambda d, s, g: (l, d, 0, 0))
    shared = pl.BlockSpec((None, 1, D_RWKV), lambda d, s, g: (l, 0, 0))
    in_specs = [
        pl.BlockSpec((WKV_ROWS, D_RWKV), col(0)),
        pl.BlockSpec((WKV_ROWS, D_RWKV), col(1)),
        pl.BlockSpec((WKV_ROWS, D_RWKV), col(2)),
        pl.BlockSpec((WKV_ROWS, TN_IN), col(LO_BLOCK)),
        per_dir_mat, per_dir_vec, per_dir_mat, per_dir_vec, shared, shared,
    ]
    args = [proj, proj, proj, proj, lp["w_up"], lp["w0"], lp["a_up"], lp["a0"], lp["k_k"], lp["k_a"]]
    if has_init:
        in_specs.append(pl.BlockSpec((None, None, None, N_HEADS, HEAD_DIM, HEAD_DIM),
                                     lambda d, s, g: (s, l, d, 0, 0, 0)))
        args.append(s0)
    out_shape = [jax.ShapeDtypeStruct((2, n_seq * seq_len, D_RWKV), F32)]
    out_specs = [pl.BlockSpec((1, WKV_ROWS, D_RWKV), lambda d, s, g: (d, group(d, s, g), 0))]
    if has_state_out:
        out_shape.append(jax.ShapeDtypeStruct((n_seq, 2, N_HEADS, HEAD_DIM, HEAD_DIM), F32))
        out_specs.append(pl.BlockSpec((None, None, N_HEADS, HEAD_DIM, HEAD_DIM), lambda d, s, g: (s, d, 0, 0, 0)))
    kern = functools.partial(_wkv_kernel, has_init=has_init, has_state_out=has_state_out)
    per_chunk = (WKV_GROUP, N_PAIRS)
    return pl.pallas_call(
        kern,
        out_shape=out_shape,
        grid=(2, n_seq, n_groups),
        in_specs=in_specs,
        out_specs=out_specs,
        scratch_shapes=[
            pltpu.VMEM((N_PAIRS, PAIR_W, PAIR_W), F32),
            pltpu.VMEM((WKV_ROWS, D_RWKV), F32),
            pltpu.VMEM((WKV_ROWS, D_RWKV), F32),
            pltpu.VMEM((WKV_ROWS, D_RWKV), F32),
            pltpu.VMEM((WKV_ROWS, D_RWKV), F32),
            pltpu.VMEM((WKV_ROWS, D_RWKV), F32),
            pltpu.VMEM((4, PAIR_W, PAIR_W), F32),
            pltpu.VMEM(per_chunk + (PAIR_W, PAIR_W), BF16),
            pltpu.VMEM(per_chunk + (PAIR_W, PAIR_W), F32),
            pltpu.VMEM(per_chunk + (PAIR_W, PAIR_W), BF16),
            pltpu.VMEM(per_chunk + (CHUNK, PAIR_W), BF16),
            pltpu.VMEM(per_chunk + (PAIR_W, PAIR_W), BF16),
            pltpu.VMEM(per_chunk + (PAIR_W, PAIR_W), F32),
            pltpu.VMEM(per_chunk + (CHUNK, PAIR_W), F32),
            pltpu.VMEM(per_chunk + (8, PAIR_W), F32),
        ],
        compiler_params=_cparams(("parallel", "parallel", "arbitrary")),
        name="wkv",
    )(*args)


def _rwkvpost_kernel(o_ref, r_ref, k_ref, v_ref, lo_ref, aup_ref, a0_ref, ka_ref, rk_ref, lng_ref, lnb_ref,
                     gup_ref, out_ref):
    lo = lo_ref[...]
    xa = lo[:, LORA_W:LORA_W + LORA_A]
    sg = _sigmoid(lo[:, LORA_W + LORA_A:LORA_W + LORA_A + LORA_G_PAD])
    gi = lax.broadcasted_iota(jnp.int32, (PAIR_W, PAIR_W), 0)
    gj = lax.broadcasted_iota(jnp.int32, (PAIR_W, PAIR_W), 1)
    head_sum = jnp.where((gi >= HEAD_DIM) == (gj >= HEAD_DIM), 1.0, 0.0).astype(BF16)
    cols = [slice(p * PAIR_W, (p + 1) * PAIR_W) for p in range(N_PAIRS)]
    gate = _dot(sg, gup_ref[...])
    ic = _sigmoid(a0_ref[0] + _dot(xa, aup_ref[0])) + _sigmoid(a0_ref[1] + _dot(xa, aup_ref[1]))
    kd_sum = k_ref[...] * (2.0 + (ic - 2.0) * ka_ref[...])
    bterm = r_ref[...] * kd_sum * rk_ref[...]
    o = [o_ref[0, :, c] + o_ref[1, :, c] for c in cols]
    mu = [_dot2(x, head_sum) * (1.0 / HEAD_DIM) for x in o]
    oc = [x - m for x, m in zip(o, mu)]
    var = [_dot(x * x, head_sum) * (1.0 / HEAD_DIM) for x in oc]
    bonus = [_dot(bterm[:, c], head_sum) * v_ref[:, c] for c in cols]
    for c, x, vr, bn in zip(cols, oc, var, bonus):
        y = x * lax.rsqrt(vr + GN_EPS) * lng_ref[:, c] + lnb_ref[:, c]
        out_ref[:, c] = ((y + bn) * gate[:, c]).astype(BF16)


def _rwkvpost_call(o, proj, lp, l):
    n_rows = o.shape[1]

    def col(cb):
        return lambda i: (i, cb)

    shared = pl.BlockSpec((None, 1, D_RWKV), lambda i: (l, 0, 0))
    return pl.pallas_call(
        _rwkvpost_kernel,
        out_shape=jax.ShapeDtypeStruct((n_rows, D_RWKV), BF16),
        grid=(n_rows // TM_POST,),
        in_specs=[
            pl.BlockSpec((2, TM_POST, D_RWKV), lambda i: (0, i, 0)),
            pl.BlockSpec((TM_POST, D_RWKV), col(0)),
            pl.BlockSpec((TM_POST, D_RWKV), col(1)),
            pl.BlockSpec((TM_POST, D_RWKV), col(2)),
            pl.BlockSpec((TM_POST, TN_IN), col(LO_BLOCK)),
            pl.BlockSpec((None, 2, LORA_A, D_RWKV), lambda i: (l, 0, 0, 0)),
            pl.BlockSpec((None, 2, 1, D_RWKV), lambda i: (l, 0, 0, 0)),
            shared, shared, shared, shared,
            pl.BlockSpec((None, LORA_G_PAD, D_RWKV), lambda i: (l, 0, 0)),
        ],
        out_specs=pl.BlockSpec((TM_POST, D_RWKV), lambda i: (i, 0)),
        compiler_params=_cparams(("parallel",)),
        name="rwkvpost",
    )(o, proj, proj, proj, proj, lp["a_up"], lp["a0"], lp["k_a"], lp["r_k"], lp["ln_g"], lp["ln_b"], lp["g_up"])


def _dft_mats(L):
    idx = (np.arange(L)[:, None] * np.arange(L)[None, :]) % (2 * L)
    ang = np.pi * idx.astype(np.float64) / L
    alt = np.cos(np.pi * np.arange(L))
    fc = np.cos(ang)
    fs = -np.sin(ang)
    fs[0, :] = alt
    fwd = np.concatenate([fc, fs], axis=0)
    ic = 2.0 * np.cos(ang.T)
    ic[:, 0] = 1.0
    isn = -2.0 * np.sin(ang.T)
    isn[:, 0] = alt
    inv = np.concatenate([ic, isn], axis=1) / (2 * L)
    return fwd.astype(np.float32), inv.astype(np.float32)


def _filter_features(L):
    t = np.linspace(0.0, 1.0, L, dtype=np.float32)[:, None]
    w = (2.0 * math.pi / L) * np.arange(L, dtype=np.float32)[:, None]
    f = np.linspace(1e-4, FILT_BANDS - 1, FILT_BANDS, dtype=np.float32)[None, :]
    z = np.concatenate([t, np.cos(f * w), -np.sin(f * w)], axis=-1)
    zp = np.zeros((L, FILT_EMB_PAD), np.float32)
    zp[:, :FILT_EMB] = z
    return zp, t


def _filter_deltas():
    max_decay = math.log(FILT_TARGET) / FAST_DECAY_PCT
    min_decay = math.log(FILT_TARGET) / SLOW_DECAY_PCT
    return np.abs(np.linspace(min_decay, max_decay, D_HYENA, dtype=np.float32))[None, :]


def _hyfilt_kernel(z_ref, t_ref, dl_ref, fwd_ref, w1_ref, b1_ref, w2_ref, b2_ref,
                   w3f0_ref, w3b0_ref, w3f1_ref, w3b1_ref, fr_ref, o_ref):
    L = z_ref.shape[0]
    tc = o_ref.shape[3]
    h = jnp.sin(fr_ref[0:1, :] * (_dot3(z_ref[...], w1_ref[...]) + b1_ref[...]))
    h = jnp.sin(fr_ref[1:2, :] * (_dot3(h, w2_ref[...]) + b2_ref[...]))
    decay = jnp.exp(-t_ref[...] * dl_ref[...])
    first = lax.broadcasted_iota(jnp.int32, (L, tc), 0) == 0
    fwd = fwd_ref[...]
    for order, (wf_ref, wb_ref) in enumerate(((w3f0_ref, w3b0_ref), (w3f1_ref, w3b1_ref))):
        hf = _dot3(h, wf_ref[...]) * decay
        hb = _dot3(h, wb_ref[...]) * decay
        norm = jnp.sum(jnp.abs(hf) + jnp.abs(hb), axis=0, keepdims=True)
        hf = hf / norm
        hb = jnp.where(first, 0.0, hb / norm)
        ks = _dot(fwd, hf + hb)
        kd = _dot(fwd, hf - hb)
        kr = ks[:L]
        o_ref[order, 0] = kr
        o_ref[order, 1] = jnp.where(first, 0.0, kd[L:])
        o_ref[order, 2] = jnp.where(first, ks[L:L + 1], kr)


def _hyfilt_call(L, fp, l):
    z, t = _filter_features(L)
    fwd, _ = _dft_mats(L)
    nct = D_HYENA // TC_FILT
    full = lambda shape: pl.BlockSpec(shape, lambda j: tuple(0 for _ in shape))
    layer = lambda shape: pl.BlockSpec((None,) + shape, lambda j: (l,) + tuple(0 for _ in shape))
    w3_spec = lambda grp: pl.BlockSpec((None, FILT_HIDDEN, TC_FILT), lambda j: (l, 0, grp * nct + j))
    w3 = fp["w3"]
    return pl.pallas_call(
        _hyfilt_kernel,
        out_shape=jax.ShapeDtypeStruct((2, 3, L, D_HYENA), F32),
        grid=(nct,),
        in_specs=[
            full((L, FILT_EMB_PAD)), full((L, 1)), pl.BlockSpec((1, TC_FILT), lambda j: (0, j)), full((2 * L, L)),
            layer((FILT_EMB_PAD, FILT_HIDDEN)), layer((1, FILT_HIDDEN)),
            layer((FILT_HIDDEN, FILT_HIDDEN)), layer((1, FILT_HIDDEN)),
            w3_spec(0), w3_spec(1), w3_spec(2), w3_spec(3),
            layer((2, FILT_HIDDEN)),
        ],
        out_specs=pl.BlockSpec((2, 3, L, TC_FILT), lambda j: (0, 0, 0, j)),
        compiler_params=_cparams(("parallel",)),
        name="hyfilt",
    )(jnp.asarray(z), jnp.asarray(t), jnp.asarray(_filter_deltas()), jnp.asarray(fwd).astype(BF16),
      fp["w1"], fp["b1"], fp["w2"], fp["b2"], w3, w3, w3, w3, fp["freq"])


def _hyena_kernel(u_ref, x1_ref, x2_ref, fwd_ref, inv_ref, spec_ref, bias_ref, out_ref):
    L = fwd_ref.shape[1]
    n_sub = u_ref.shape[0] // L

    def side_by_side(ref):
        return jnp.concatenate([ref[s * L:(s + 1) * L, :] for s in range(n_sub)], axis=1)

    def tiled(x):
        return jnp.concatenate([x] * n_sub, axis=1)

    def long_conv(u, order):
        spec = _dot(fwd_ref[...], u)
        ur, ui = spec[:L], spec[L:]
        kr, ki, kr2 = tiled(spec_ref[order, 0]), tiled(spec_ref[order, 1]), tiled(spec_ref[order, 2])
        yr = ur * kr - ui * ki
        yi = ur * ki + ui * kr2
        y = _dot(inv_ref[...], jnp.concatenate([yr, yi], axis=0))
        return y + u * tiled(bias_ref[order:order + 1, :])

    z = side_by_side(x1_ref) * long_conv(side_by_side(u_ref), 0)
    out = (side_by_side(x2_ref) * long_conv(z, 1)).astype(BF16)
    tc = out_ref.shape[1]
    for s in range(n_sub):
        out_ref[s * L:(s + 1) * L, :] = out[:, s * tc:(s + 1) * tc]


def _hyena_call(proj, spec, bias, l, n_seq, L, tc, n_sub):
    fwd, inv = _dft_mats(L)
    nct = D_HYENA // tc
    rows = n_sub * L
    cb0 = 3 * D_RWKV // tc

    def col(which):
        return lambda s, j: (s, cb0 + which * nct + j)

    in_specs = [
        pl.BlockSpec((rows, tc), col(0)),
        pl.BlockSpec((rows, tc), col(1)),
        pl.BlockSpec((rows, tc), col(2)),
        pl.BlockSpec((2 * L, L), lambda s, j: (0, 0), pipeline_mode=pl.Buffered(1)),
        pl.BlockSpec((L, 2 * L), lambda s, j: (0, 0), pipeline_mode=pl.Buffered(1)),
        pl.BlockSpec((2, 3, L, tc), lambda s, j: (0, 0, 0, j)),
        pl.BlockSpec((None, 2, tc), lambda s, j: (l, 0, j)),
    ]
    return pl.pallas_call(
        _hyena_kernel,
        out_shape=jax.ShapeDtypeStruct((n_seq * L, D_HYENA), BF16),
        grid=(n_seq // n_sub, nct),
        in_specs=in_specs,
        out_specs=pl.BlockSpec((rows, tc), lambda s, j: (s, j)),
        compiler_params=_cparams(("parallel", "parallel")),
        name="hyena",
    )(proj, proj, proj, jnp.asarray(fwd).astype(BF16), jnp.asarray(inv).astype(BF16), spec, bias)


def _outproj_kernel(a_ref, b_ref, x_ref, g_ref, w_ref, lg_ref, lb_ref, o_ref):
    mix = (jnp.dot(a_ref[...], w_ref[0:D_RWKV, :], preferred_element_type=F32)
           + jnp.dot(b_ref[...], w_ref[D_RWKV:, :], preferred_element_type=F32))
    y = ALPHA * x_ref[...] + g_ref[0, 0] * mix
    o_ref[...] = _layer_norm_rows(y) * lg_ref[...] + lb_ref[...]


def _outproj_call(a_out, b_out, x, mod_all, w_out, ln_g, ln_b, l, is_grid):
    grp = lambda i: _cond_row(i, TM_OUT, is_grid)
    vec = pl.BlockSpec((None, 1, D_MODEL), lambda i: (l, 0, 0))
    return pl.pallas_call(
        _outproj_kernel,
        out_shape=jax.ShapeDtypeStruct(x.shape, F32),
        grid=(x.shape[0] // TM_OUT,),
        in_specs=[
            pl.BlockSpec((TM_OUT, D_RWKV), lambda i: (i, 0)),
            pl.BlockSpec((TM_OUT, D_HYENA), lambda i: (i, 0)),
            pl.BlockSpec((TM_OUT, D_MODEL), lambda i: (i, 0)),
            _mod_spec(l, 2, grp),
            pl.BlockSpec((None, D_MODEL, D_MODEL), lambda i: (l, 0, 0)),
            vec, vec,
        ],
        out_specs=pl.BlockSpec((TM_OUT, D_MODEL), lambda i: (i, 0)),
        compiler_params=_cparams(("parallel",)),
        name="outproj",
    )(a_out, b_out, x, mod_all, w_out, ln_g, ln_b)


def _mlp_kernel(x_ref, sh_ref, sc_ref, g_ref, w1_ref, w2_ref, lg_ref, lb_ref, o_ref, h_ref):
    f = pl.program_id(1)
    row_blocks = o_ref.shape[0] // 128

    @pl.when(f == 0)
    def _():
        sc = 1.0 + sc_ref[0, 0]
        sh = sh_ref[0, 0]

        def body(rb, carry):
            rows = pl.ds(pl.multiple_of(rb * 128, 128), 128)
            h_ref[rows, :] = (_layer_norm_rows(x_ref[rows, :]) * sc + sh).astype(BF16)
            return carry

        lax.fori_loop(0, row_blocks, body, 0)
        o_ref[...] = jnp.zeros(o_ref.shape, F32)

    hid = jnp.maximum(jnp.dot(h_ref[...], w1_ref[...].astype(BF16), preferred_element_type=F32), 0.0)
    hid = (hid * hid).astype(BF16)
    o_ref[...] += jnp.dot(hid, w2_ref[...].astype(BF16), preferred_element_type=F32)

    @pl.when(f == pl.num_programs(1) - 1)
    def _():
        gate = g_ref[0, 0]
        lg = lg_ref[...]
        lb = lb_ref[...]

        def body(rb, carry):
            rows = pl.ds(pl.multiple_of(rb * 128, 128), 128)
            y = ALPHA * x_ref[rows, :] + gate * o_ref[rows, :]
            o_ref[rows, :] = _layer_norm_rows(y) * lg + lb
            return carry

        lax.fori_loop(0, row_blocks, body, 0)


def _mlp_call(x, mod_all, w1, w2, ln_g, ln_b, l, is_grid):
    grp = lambda i: _cond_row(i, TM_MLP, is_grid)
    vec = pl.BlockSpec((None, 1, D_MODEL), lambda i, f: (l, 0, 0))
    return pl.pallas_call(
        _mlp_kernel,
        out_shape=jax.ShapeDtypeStruct(x.shape, F32),
        grid=(x.shape[0] // TM_MLP, D_FF // TF_MLP),
        in_specs=[
            pl.BlockSpec((TM_MLP, D_MODEL), lambda i, f: (i, 0), pipeline_mode=pl.Buffered(1)),
            _mod_spec(l, 3, grp),
            _mod_spec(l, 4, grp),
            _mod_spec(l, 5, grp),
            pl.BlockSpec((None, D_MODEL, TF_MLP), lambda i, f: (l, 0, f)),
            pl.BlockSpec((None, TF_MLP, D_MODEL), lambda i, f: (l, f, 0)),
            vec, vec,
        ],
        out_specs=pl.BlockSpec((TM_MLP, D_MODEL), lambda i, f: (i, 0)),
        scratch_shapes=[pltpu.VMEM((TM_MLP, D_MODEL), BF16)],
        compiler_params=_cparams(("parallel", "arbitrary")),
        name="mlp",
    )(x, mod_all, mod_all, mod_all, w1, w2, ln_g, ln_b)


def kernel(x_prompt, x_sample, c, state_rwkv, c_ctx, w_ada, b_ada, w_in, conv_w, lora_w_up, lora_w0, lora_a_up, lora_a0, lora_g_up, rwkv_k_k, rwkv_k_a, rwkv_r_k, rwkv_ln_g, rwkv_ln_b, filt_w1, filt_b1, filt_w2, filt_b2, filt_w3, filt_freq, hyena_bias, w_out, ln1_g, ln1_b, ln2_g, ln2_b, mlp_w1, mlp_w2):
    n_ctx, ctx_seq, _ = x_prompt.shape
    n_lat, lat_seq, _ = x_sample.shape
    depth = w_ada.shape[0]
    assert ctx_seq == CTX_SEQ and lat_seq == LAT_SEQ and (n_ctx * ctx_seq) % TM == 0
    assert 1 + n_lat <= N_COND
    ctx_rows = n_ctx * ctx_seq
    lat_rows = n_lat * lat_seq

    xc = x_prompt.reshape(ctx_rows, D_MODEL)
    xl = x_sample.reshape(lat_rows, D_MODEL)
    cond = jnp.concatenate([c_ctx[None, :], c, jnp.zeros((N_COND - 1 - n_lat, D_MODEL), F32)], axis=0)
    mod_all = _mod_call(cond, w_ada, b_ada).reshape(depth, N_COND, 6, 1, D_MODEL)

    w_in_t = jnp.swapaxes(w_in, 1, 2)
    w_lo_bf = jnp.pad(w_in[:, :, N_CONV:].astype(BF16), ((0, 0), (0, 0), (0, TN_IN - (D_IN - N_CONV))))
    conv_w9 = conv_w.reshape(depth, 9, N_CONV)
    w_out_bf = w_out.astype(BF16)
    row = lambda a: a.reshape(depth, 1, a.shape[-1])
    lp = {
        "w_up": lora_w_up, "w0": lora_w0[:, :, None, :], "a_up": lora_a_up, "a0": lora_a0[:, :, None, :],
        "k_k": row(rwkv_k_k), "k_a": row(rwkv_k_a), "r_k": rwkv_r_k.reshape(depth, 1, D_RWKV),
        "ln_g": row(rwkv_ln_g), "ln_b": row(rwkv_ln_b),
        "g_up": jnp.pad(lora_g_up, ((0, 0), (0, LORA_G_PAD - LORA_G), (0, 0))),
    }
    fp = {
        "w1": jnp.pad(filt_w1, ((0, 0), (0, FILT_EMB_PAD - FILT_EMB), (0, 0))), "b1": row(filt_b1),
        "w2": filt_w2, "b2": row(filt_b2), "w3": filt_w3, "freq": filt_freq,
    }
    ln1_g, ln1_b, ln2_g, ln2_b = row(ln1_g), row(ln1_b), row(ln2_g), row(ln2_b)
    state_in = state_rwkv.astype(F32)

    def trunk_layer(x, l, is_grid, n_seq, seq_len, s0, hyena_tc, hyena_sub):
        proj = _inproj_call(x, mod_all, w_in_t, w_lo_bf, conv_w9, l, is_grid)
        scan = _wkv_call(proj, lp, l, s0, n_seq, seq_len, s0 is None)
        a_out = _rwkvpost_call(scan[0], proj, lp, l)
        spec = _hyfilt_call(seq_len, fp, l)
        b_out = _hyena_call(proj, spec, hyena_bias, l, n_seq, seq_len, hyena_tc, hyena_sub)
        x = _outproj_call(a_out, b_out, x, mod_all, w_out_bf, ln1_g, ln1_b, l, is_grid)
        x = _mlp_call(x, mod_all, mlp_w1, mlp_w2, ln2_g, ln2_b, l, is_grid)
        return x, (scan[1] if s0 is None else None)

    ctx_states = []
    lat_sub = 2 if n_lat % 2 == 0 else 1
    for l in range(depth):
        xc, s_ctx = trunk_layer(xc, l, False, n_ctx, ctx_seq, None, D_HYENA, 1)
        ctx_states.append(s_ctx)
        xl, _ = trunk_layer(xl, l, True, n_lat, lat_seq, state_in, 256, lat_sub)

    y_prompt = xc.reshape(n_ctx, ctx_seq, D_MODEL)
    y_sample = xl.reshape(n_lat, lat_seq, D_MODEL)
    new_state = jnp.stack(ctx_states, axis=1).astype(x_prompt.dtype)
    return (y_prompt, y_sample, new_state)
```

```python
import functools
import math

import jax
import jax.numpy as jnp
import numpy as np
from jax import lax
from jax.experimental import pallas as pl
from jax.experimental.pallas import tpu as pltpu

F32 = jnp.float32
BF16 = jnp.bfloat16

D_MODEL = 2048
D_RWKV = 1024
D_HYENA = 1024
HEAD_DIM = 64
N_HEADS = D_RWKV // HEAD_DIM
N_PAIRS = N_HEADS // 2
PAIR_W = 2 * HEAD_DIM
LORA_W = 64
LORA_A = 64
LORA_G = 160
LORA_G_PAD = 256
N_CONV = 3 * D_RWKV + 3 * D_HYENA
D_IN = N_CONV + LORA_W + LORA_A + LORA_G
D_FF = 4 * D_MODEL
GRID_W = 64
CTX_SEQ = 256
LAT_SEQ = 1024
FILT_BANDS = 16
FILT_EMB = 1 + 2 * FILT_BANDS
FILT_EMB_PAD = 128
FILT_HIDDEN = 64
DEPTH = 2
ALPHA = (2 * DEPTH) ** 0.25
LN_EPS = 1e-5
GN_EPS = 64e-5
FILT_TARGET = 1e-2
FAST_DECAY_PCT = 0.3
SLOW_DECAY_PCT = 1.5
DECAY_SCALE = math.exp(-0.5)

TM = 1024
TN_IN = 512
NJ_CONV = N_CONV // TN_IN
D_IN_PAD = (NJ_CONV + 1) * TN_IN
LO_BLOCK = N_CONV // TN_IN
CONV_PAD = 72
CHUNK = 64
WKV_GROUP = 4
WKV_ROWS = WKV_GROUP * CHUNK
WKV_PAR = 2
N_COND = 8
TN_MOD = 1024
TM_OUT = 512
TM_MLP = 1024
TF_MLP = 512
TM_POST = 256
TC_FILT = 512
VMEM_LIMIT = 56 * 1024 * 1024


def _cparams(sem):
    return pltpu.CompilerParams(dimension_semantics=sem, vmem_limit_bytes=VMEM_LIMIT)


def _dot(a, b):
    return jnp.dot(a.astype(BF16), b.astype(BF16), preferred_element_type=F32)


def _dot_nt(a, b):
    return lax.dot_general(a.astype(BF16), b.astype(BF16), (((1,), (1,)), ((), ())),
                           preferred_element_type=F32)


def _dot_tn(a, b):
    return lax.dot_general(a.astype(BF16), b.astype(BF16), (((0,), (0,)), ((), ())),
                           preferred_element_type=F32)


def _split(x):
    hi = x.astype(BF16)
    lo = (x - hi.astype(F32)).astype(BF16)
    return hi, lo


def _dot3(a, b):
    ah, al = _split(a)
    bh, bl = _split(b)
    return _dot(ah, bh) + (_dot(ah, bl) + _dot(al, bh))


def _dot2(a, b_exact):
    ah, al = _split(a)
    return _dot(ah, b_exact) + _dot(al, b_exact)


def _dot2_left(a_exact, b):
    bh, bl = _split(b)
    return _dot(a_exact, bh) + _dot(a_exact, bl)


def _sigmoid(x):
    return 1.0 / (1.0 + jnp.exp(-x))


def _layer_norm_rows(x):
    mu = jnp.mean(x, axis=-1, keepdims=True)
    xc = x - mu
    var = jnp.mean(xc * xc, axis=-1, keepdims=True)
    return xc * lax.rsqrt(var + LN_EPS)


def _cond_row(i, tm, is_grid):
    return 1 + (i * tm) // LAT_SEQ if is_grid else 0


def _mod_spec(l, which, grp):
    return pl.BlockSpec((None, 1, 1, 1, D_MODEL), lambda i, *_: (l, grp(i), which, 0, 0))


def _mod_kernel(c_ref, w_ref, b_ref, o_ref):
    c = c_ref[...]
    s = c * _sigmoid(c)
    o_ref[0] = _dot(s, w_ref[0]) + b_ref[0]


def _mod_call(cond, w_ada, b_ada):
    depth = w_ada.shape[0]
    n_out = w_ada.shape[2]
    return pl.pallas_call(
        _mod_kernel,
        out_shape=jax.ShapeDtypeStruct((depth, N_COND, n_out), F32),
        grid=(depth, n_out // TN_MOD),
        in_specs=[
            pl.BlockSpec((N_COND, D_MODEL), lambda l, j: (0, 0)),
            pl.BlockSpec((1, D_MODEL, TN_MOD), lambda l, j: (l, 0, j)),
            pl.BlockSpec((1, 1, TN_MOD), lambda l, j: (l, 0, j)),
        ],
        out_specs=pl.BlockSpec((1, N_COND, TN_MOD), lambda l, j: (l, 0, j)),
        compiler_params=_cparams(("parallel", "parallel")),
        name="mod",
    )(cond, w_ada, b_ada.reshape(depth, 1, n_out))


def _inproj_kernel(x_ref, sh_ref, sc_ref, w_ref, wlo_ref, cw_ref, o_ref, h_ref, wbf_ref, acc_a, acc_b, *,
                   is_grid):
    j = pl.program_id(1)
    tn = o_ref.shape[1]
    mid = slice(CONV_PAD, CONV_PAD + TM)

    n_blocks = TM // GRID_W
    n_parts = 4
    lane_w = 128
    n_chunk = 128

    def round_weights():
        for nc in range(tn // n_chunk):
            rows = slice(nc * n_chunk, (nc + 1) * n_chunk)
            wbf_ref[rows, :] = w_ref[rows, :].astype(BF16)

    def matmul_into(dst_ref, part=None):
        if part is None:
            dst_ref[mid, :] = _dot_nt(h_ref[...], wbf_ref[...])
        else:
            r0, r1 = part * (TM // n_parts), (part + 1) * (TM // n_parts)
            dst_ref[CONV_PAD + r0:CONV_PAD + r1, :] = _dot_nt(h_ref[r0:r1, :], wbf_ref[...])

    def conv_from(src_ref, g0=0, g1=n_blocks):
        row = lax.broadcasted_iota(jnp.int32, (GRID_W, lane_w), 0)
        first = row == 0
        last = row == GRID_W - 1
        per_seq = CTX_SEQ // GRID_W

        def shifted(s, lanes, mask_first, mask_last):
            start = CONV_PAD + s * GRID_W
            uc = src_ref[start:start + GRID_W, lanes]
            ul = src_ref[start - 1:start - 1 + GRID_W, lanes]
            ur = src_ref[start + 1:start + 1 + GRID_W, lanes]
            if mask_first:
                ul = jnp.where(first, 0.0, ul)
            if mask_last:
                ur = jnp.where(last, 0.0, ur)
            return ul, uc, ur

        for c in range(tn // lane_w):
            lanes = slice(c * lane_w, (c + 1) * lane_w)
            cw = cw_ref[:, lanes]
            if not is_grid:
                for s in range(g0, g1):
                    ul, uc, ur = shifted(s, lanes, s % per_seq == 0, s % per_seq == per_seq - 1)
                    o_ref[s * GRID_W:(s + 1) * GRID_W, lanes] = ul * cw[3:4, :] + uc * cw[4:5, :] + ur * cw[5:6, :]
                continue
            partial = {}
            for s in range(max(g0 - 1, 0), min(g1 + 1, n_blocks)):
                ul, uc, ur = shifted(s, lanes, True, True)
                for a in range(3):
                    gi = s - (a - 1)
                    if g0 <= gi < g1:
                        t = (ul * cw[3 * a:3 * a + 1, :] + uc * cw[3 * a + 1:3 * a + 2, :]
                             + ur * cw[3 * a + 2:3 * a + 3, :])
                        partial[gi] = t if gi not in partial else partial[gi] + t
                if s - 1 in partial:
                    o_ref[(s - 1) * GRID_W:s * GRID_W, lanes] = partial.pop(s - 1)
            for gi in sorted(partial):
                o_ref[gi * GRID_W:(gi + 1) * GRID_W, lanes] = partial.pop(gi)

    @pl.when(j == 0)
    def _():
        sc = 1.0 + sc_ref[0, 0]
        sh = sh_ref[0, 0]

        def body(rb, carry):
            rows = pl.ds(pl.multiple_of(rb * 128, 128), 128)
            h = _layer_norm_rows(x_ref[rows, :]) * sc + sh
            h_ref[rows, :] = h.astype(BF16)
            return carry

        lax.fori_loop(0, TM // 128, body, 0)
        for acc in (acc_a, acc_b):
            acc[0:CONV_PAD, :] = jnp.zeros((CONV_PAD, tn), F32)
            acc[CONV_PAD + TM:, :] = jnp.zeros((CONV_PAD, tn), F32)
        o_ref[...] = jnp.dot(h_ref[...], wlo_ref[...], preferred_element_type=F32)

    @pl.when(j == 1)
    def _():
        round_weights()
        matmul_into(acc_a)

    main = jnp.logical_and(j >= 2, j <= NJ_CONV)
    odd = jnp.bitwise_and(j, 1) == 1

    def overlapped(src_ref, dst_ref):
        round_weights()
        per_part = n_blocks // n_parts
        for part in range(n_parts):
            matmul_into(dst_ref, part)
            conv_from(src_ref, part * per_part, (part + 1) * per_part)

    @pl.when(jnp.logical_and(main, jnp.logical_not(odd)))
    def _():
        overlapped(acc_a, acc_b)

    @pl.when(jnp.logical_and(main, odd))
    def _():
        overlapped(acc_b, acc_a)

    @pl.when(j == NJ_CONV + 1)
    def _():
        conv_from(acc_b if NJ_CONV % 2 == 0 else acc_a)


def _inproj_call(x, mod_all, w_in, w_lo, conv_w, l, is_grid):
    n_units = x.shape[0] // TM
    kern = functools.partial(_inproj_kernel, is_grid=is_grid)
    grp = lambda i: _cond_row(i, TM, is_grid)
    tile = lambda t: jnp.clip(t, 0, NJ_CONV - 1)
    return pl.pallas_call(
        kern,
        out_shape=jax.ShapeDtypeStruct((x.shape[0], D_IN_PAD), F32),
        grid=(n_units, NJ_CONV + 2),
        in_specs=[
            pl.BlockSpec((TM, D_MODEL), lambda i, j: (i, 0)),
            _mod_spec(l, 0, grp),
            _mod_spec(l, 1, grp),
            pl.BlockSpec((None, TN_IN, D_MODEL), lambda i, j: (l, tile(j - 1), 0)),
            pl.BlockSpec((None, D_MODEL, TN_IN), lambda i, j: (l, 0, 0)),
            pl.BlockSpec((None, 9, TN_IN), lambda i, j: (l, 0, tile(j - 2))),
        ],
        out_specs=pl.BlockSpec((TM, TN_IN), lambda i, j: (i, jnp.where(j == 0, LO_BLOCK, tile(j - 2)))),
        scratch_shapes=[
            pltpu.VMEM((TM, D_MODEL), BF16),
            pltpu.VMEM((TN_IN, D_MODEL), BF16),
            pltpu.VMEM((TM + 2 * CONV_PAD, TN_IN), F32),
            pltpu.VMEM((TM + 2 * CONV_PAD, TN_IN), F32),
        ],
        compiler_params=_cparams(("parallel", "arbitrary")),
        name="inproj",
    )(x, mod_all, mod_all, w_in, w_lo, conv_w)


def _stack_heads(z, head0):
    return jnp.concatenate([jnp.where(head0, z, 0.0), jnp.where(head0, 0.0, z)], axis=0)


def _unstack_heads(z):
    c = z.shape[0] // 2
    return z[:c] + z[c:]


def _wkv_kernel(*refs, has_init, has_state_out):
    (r_ref, k_ref, v_ref, lo_ref, wup_ref, w0_ref, aup_ref, a0_ref, kk_ref, ka_ref) = refs[:10]
    pos = 10
    s0_ref = None
    if has_init:
        s0_ref = refs[pos]
        pos += 1
    o_ref = refs[pos]
    pos += 1
    sout_ref = None
    if has_state_out:
        sout_ref = refs[pos]
        pos += 1
    (s_ref, lw_ref, cum_ref, kkn_ref, b_ref, kd_ref, msk_ref,
     w1_ref, w2_ref, yhat_ref, kv_ref, mv_ref, dec_ref) = refs[pos:]

    d = pl.program_id(0)
    g = pl.program_id(2)
    C = CHUNK
    pairs = range(N_PAIRS)
    cols = [slice(p * PAIR_W, (p + 1) * PAIR_W) for p in pairs]

    @pl.when(g == 0)
    def _():
        if has_init:
            zero = jnp.zeros((HEAD_DIM, HEAD_DIM), F32)
            for p in pairs:
                top = jnp.concatenate([s0_ref[2 * p], zero], axis=1)
                bottom = jnp.concatenate([zero, s0_ref[2 * p + 1]], axis=1)
                s_ref[p] = jnp.concatenate([top, bottom], axis=0)
        else:
            s_ref[...] = jnp.zeros(s_ref.shape, F32)

    sgn = 1 - 2 * d
    si = lax.broadcasted_iota(jnp.int32, (2 * C, 2 * C), 0)
    sj = lax.broadcasted_iota(jnp.int32, (2 * C, 2 * C), 1)
    same_head = (si >= C) == (sj >= C)
    dlt = (jnp.bitwise_and(si, C - 1) - jnp.bitwise_and(sj, C - 1)) * sgn
    msk_ref[0] = jnp.where(jnp.logical_and(same_head, dlt < 0), 1.0, 0.0)
    msk_ref[1] = jnp.where(jnp.logical_and(same_head, dlt >= 0), 1.0, 0.0)
    msk_ref[2] = jnp.where(si == sj, 1.0, 0.0)
    blockdiag = (si >= HEAD_DIM) == (sj >= HEAD_DIM)
    msk_ref[3] = jnp.where(blockdiag, 1.0, 0.0)
    head_sum = jnp.where(blockdiag, 1.0, 0.0).astype(BF16)

    lo = lo_ref[...]
    xw = lo[:, 0:LORA_W]
    xa = lo[:, LORA_W:LORA_W + LORA_A]
    logw = -DECAY_SCALE * _sigmoid(w0_ref[...] + _dot(jnp.tanh(xw), wup_ref[...]))
    lw_ref[...] = logw
    gi = lax.broadcasted_iota(jnp.int32, (WKV_ROWS, WKV_ROWS), 0)
    gj = lax.broadcasted_iota(jnp.int32, (WKV_ROWS, WKV_ROWS), 1)
    same_chunk = jnp.bitwise_and(gi, -C) == jnp.bitwise_and(gj, -C)
    cum_mask = jnp.where(jnp.logical_and(same_chunk, (gi - gj) * sgn >= 0), 1.0, 0.0).astype(BF16)
    cum_ref[...] = _dot2_left(cum_mask, logw)
    iclr = _sigmoid(a0_ref[...] + _dot(xa, aup_ref[...]))
    k = k_ref[...]
    kd_ref[...] = k * (1.0 + (iclr - 1.0) * ka_ref[...])
    kkr = k * kk_ref[...]
    for p in pairs:
        slab = kkr[:, cols[p]]
        kkn = slab * lax.rsqrt(_dot(slab * slab, head_sum) + 1e-12)
        kkn_ref[:, cols[p]] = kkn
        b_ref[:, cols[p]] = kkn * iclr[:, cols[p]]

    def chunk_rows(i):
        cc = jnp.where(d == 0, i, WKV_GROUP - 1 - i)
        return pl.ds(pl.multiple_of(cc * C, C), C)

    def phase_a(it, carry):
        chunk_ids = [it * WKV_PAR + ci for ci in range(WKV_PAR)]
        units = [(ci, c) for ci in chunk_ids for c in cols]
        chunk_of = [ci for ci in chunk_ids for _ in cols]
        pair_of = [p for _ in chunk_ids for p in pairs]
        head0 = lax.broadcasted_iota(jnp.int32, (C, PAIR_W), 1) < HEAD_DIM
        strict_t = msk_ref[0]
        incl = msk_ref[1]
        eye = msk_ref[2]
        bd_f = msk_ref[3]

        lw = [lw_ref[chunk_rows(ci), c] for ci, c in units]
        cum = [cum_ref[chunk_rows(ci), c] for ci, c in units]
        kkn = [kkn_ref[chunk_rows(ci), c] for ci, c in units]
        bb = [b_ref[chunk_rows(ci), c] for ci, c in units]
        kd = [kd_ref[chunk_rows(ci), c] for ci, c in units]
        rr = [r_ref[chunk_rows(ci), c] for ci, c in units]
        vv = [v_ref[chunk_rows(ci), c] for ci, c in units]
        cum_prev = [cm - x for cm, x in zip(cum, lw)]
        mid = [cm[C // 2:C // 2 + 1, :] for cm in cum]
        total = [jnp.where(d == 0, cm[C - 1:C, :], cm[0:1, :]) for cm in cum]
        e_in = [jnp.exp(cm - m) for cm, m in zip(cum, mid)]
        e_in_prev = [jnp.exp(cm - m) for cm, m in zip(cum_prev, mid)]
        e_out = [jnp.exp(m - cm) for cm, m in zip(cum, mid)]
        e_tail = [jnp.exp(t - cm) for cm, t in zip(cum, total)]
        a1s = [_stack_heads(x * e, head0) for x, e in zip(kkn, e_in_prev)]
        r1s = [_stack_heads(x * e, head0) for x, e in zip(rr, e_in)]
        b1 = [x * e for x, e in zip(bb, e_out)]
        k1 = [x * e for x, e in zip(kd, e_out)]
        a0s = [_stack_heads(x * jnp.exp(cm), head0) for x, cm in zip(kkn, cum_prev)]
        vs = [_stack_heads(x, head0) for x in vv]
        g_t = [_dot_nt(jnp.concatenate([b, k], axis=0), jnp.concatenate([a, r_], axis=0))
               for a, r_, b, k in zip(a1s, r1s, b1, k1)]
        strict_s = strict_t[:C] + strict_t[C:]
        eye_s = eye[:C] + eye[C:]
        incl_s = strict_s + eye_s
        stack = lambda z: _stack_heads(z, head0)
        l_ab_s = [g_[:C, :PAIR_W] * strict_s for g_ in g_t]
        l_ak_t = [stack(g_[C:, :PAIR_W] * strict_s) for g_ in g_t]
        m_rb_t = [stack(g_[:C, PAIR_W:] * incl_s) for g_ in g_t]
        m_rk = [_dot_nt(a, jnp.concatenate([k, k], axis=0)) * incl for a, k in zip(r1s, k1)]
        lakv_t = [_dot_tn(a, b) for a, b in zip(vs, l_ak_t)]
        mv = [_unstack_heads(_dot(a, b)) for a, b in zip(m_rk, vs)]
        kv = [_dot_tn(a, b * e) * bd_f for a, b, e in zip(vv, kd, e_tail)]
        x = [eye_s - l_ for l_ in l_ab_s]
        q = [_dot(l_, stack(l_)) for l_ in l_ab_s]
        step = 2
        while 2 * step < C:
            prod = [_dot(jnp.concatenate([xx, qq], axis=0), stack(qq)) for xx, qq in zip(x, q)]
            x = [xx + pr[:C] for xx, pr in zip(x, prod)]
            q = [pr[C:] for pr in prod]
            step *= 2
        x = [stack(xx + _dot(xx, stack(qq))) for xx, qq in zip(x, q)]
        a_hat_t = [_dot_tn(a, xx) for xx, a in zip(x, a0s)]
        y_hat_t = [_dot(a, xx) for xx, a in zip(x, lakv_t)]
        zero_rows = jnp.zeros((C, PAIR_W), F32)
        for u, (i, p) in enumerate(zip(chunk_of, pair_of)):
            r0_t = jnp.concatenate([rr[u] * jnp.exp(cum[u]), zero_rows], axis=0).T
            w1_ref[i, p] = jnp.concatenate([a_hat_t[u], r0_t], axis=1).astype(BF16)
            w2_ref[i, p] = jnp.concatenate([stack(bb[u] * e_tail[u]), m_rb_t[u]], axis=1).astype(BF16)
            yhat_ref[i, p] = y_hat_t[u]
            kv_ref[i, p] = kv[u]
            mv_ref[i, p] = mv[u]
            dec_ref[i, p] = jnp.broadcast_to(jnp.exp(total[u]), (8, PAIR_W))
        return carry

    def phase_b(i, s_old):
        rows = chunk_rows(i)
        s_bf = [s.astype(BF16) for s in s_old]
        sw = [_dot(s_bf[p], w1_ref[i, p]) for p in pairs]
        yt = [(sw[p][:, :PAIR_W] + yhat_ref[i, p]).astype(BF16) for p in pairs]
        yw = [_dot(yt[p], w2_ref[i, p]) for p in pairs]
        s_new = [dec_ref[i, p, 0:1, :] * s_old[p] + kv_ref[i, p] - yw[p][:, :PAIR_W] for p in pairs]
        o = [sw[p][:, PAIR_W:].T[:C] + mv_ref[i, p] - _unstack_heads(yw[p][:, PAIR_W:].T) for p in pairs]
        for p in pairs:
            o_ref[0, rows, cols[p]] = o[p]
        return s_new

    lax.fori_loop(0, WKV_GROUP // WKV_PAR, phase_a, 0)
    state = [s_ref[p] for p in pairs]
    for i in range(WKV_GROUP):
        state = phase_b(i, state)
    for p in pairs:
        s_ref[p] = state[p]

    if has_state_out:
        @pl.when(g == pl.num_programs(2) - 1)
        def _():
            for p in pairs:
                s = s_ref[p]
                sout_ref[2 * p] = s[:HEAD_DIM, :HEAD_DIM]
                sout_ref[2 * p + 1] = s[HEAD_DIM:, HEAD_DIM:]


def _wkv_call(proj, lp, l, s0, n_seq, seq_len, has_state_out):
    n_groups = seq_len // WKV_ROWS
    has_init = s0 is not None

    def group(d, s, g):
        return s * n_groups + g + d * (n_groups - 1 - 2 * g)

    def col(cb):
        return lambda d, s, g: (group(d, s, g), cb)

    per_dir_mat = pl.BlockSpec((None, None, LORA_W, D_RWKV), lambda d, s, g: (l, d, 0, 0))
    per_dir_vec = pl.BlockSpec((None, None, 1, D_RWKV), lambda d, s, g: (l, d, 0, 0))
    shared = pl.BlockSpec((None, 1, D_RWKV), lambda d, s, g: (l, 0, 0))
    in_specs = [
        pl.BlockSpec((WKV_ROWS, D_RWKV), col(0)),
        pl.BlockSpec((WKV_ROWS, D_RWKV), col(1)),
        pl.BlockSpec((WKV_ROWS, D_RWKV), col(2)),
        pl.BlockSpec((WKV_ROWS, TN_IN), col(LO_BLOCK)),
        per_dir_mat, per_dir_vec, per_dir_mat, per_dir_vec, shared, shared,
    ]
    args = [proj, proj, proj, proj, lp["w_up"], lp["w0"], lp["a_up"], lp["a0"], lp["k_k"], lp["k_a"]]
    if has_init:
        in_specs.append(pl.BlockSpec((None, None, None, N_HEADS, HEAD_DIM, HEAD_DIM),
                                     lambda d, s, g: (s, l, d, 0, 0, 0)))
        args.append(s0)
    out_shape = [jax.ShapeDtypeStruct((2, n_seq * seq_len, D_RWKV), F32)]
    out_specs = [pl.BlockSpec((1, WKV_ROWS, D_RWKV), lambda d, s, g: (d, group(d, s, g), 0))]
    if has_state_out:
        out_shape.append(jax.ShapeDtypeStruct((n_seq, 2, N_HEADS, HEAD_DIM, HEAD_DIM), F32))
        out_specs.append(pl.BlockSpec((None, None, N_HEADS, HEAD_DIM, HEAD_DIM), lambda d, s, g: (s, d, 0, 0, 0)))
    kern = functools.partial(_wkv_kernel, has_init=has_init, has_state_out=has_state_out)
    per_chunk = (WKV_GROUP, N_PAIRS)
    return pl.pallas_call(
        kern,
        out_shape=out_shape,
        grid=(2, n_seq, n_groups),
        in_specs=in_specs,
        out_specs=out_specs,
        scratch_shapes=[
            pltpu.VMEM((N_PAIRS, PAIR_W, PAIR_W), F32),
            pltpu.VMEM((WKV_ROWS, D_RWKV), F32),
            pltpu.VMEM((WKV_ROWS, D_RWKV), F32),
            pltpu.VMEM((WKV_ROWS, D_RWKV), F32),
            pltpu.VMEM((WKV_ROWS, D_RWKV), F32),
            pltpu.VMEM((WKV_ROWS, D_RWKV), F32),
            pltpu.VMEM((4, PAIR_W, PAIR_W), F32),
            pltpu.VMEM(per_chunk + (PAIR_W, 2 * PAIR_W), BF16),
            pltpu.VMEM(per_chunk + (PAIR_W, 2 * PAIR_W), BF16),
            pltpu.VMEM(per_chunk + (PAIR_W, PAIR_W), F32),
            pltpu.VMEM(per_chunk + (PAIR_W, PAIR_W), F32),
            pltpu.VMEM(per_chunk + (CHUNK, PAIR_W), F32),
            pltpu.VMEM(per_chunk + (8, PAIR_W), F32),
        ],
        compiler_params=_cparams(("parallel", "parallel", "arbitrary")),
        name="wkv",
    )(*args)


def _rwkvpost_kernel(o_ref, r_ref, k_ref, v_ref, lo_ref, aup_ref, a0_ref, ka_ref, rk_ref, lng_ref, lnb_ref,
                     gup_ref, out_ref):
    lo = lo_ref[...]
    xa = lo[:, LORA_W:LORA_W + LORA_A]
    sg = _sigmoid(lo[:, LORA_W + LORA_A:LORA_W + LORA_A + LORA_G_PAD])
    gi = lax.broadcasted_iota(jnp.int32, (PAIR_W, PAIR_W), 0)
    gj = lax.broadcasted_iota(jnp.int32, (PAIR_W, PAIR_W), 1)
    head_sum = jnp.where((gi >= HEAD_DIM) == (gj >= HEAD_DIM), 1.0, 0.0).astype(BF16)
    cols = [slice(p * PAIR_W, (p + 1) * PAIR_W) for p in range(N_PAIRS)]
    gate = _dot(sg, gup_ref[...])
    ic = _sigmoid(a0_ref[0] + _dot(xa, aup_ref[0])) + _sigmoid(a0_ref[1] + _dot(xa, aup_ref[1]))
    kd_sum = k_ref[...] * (2.0 + (ic - 2.0) * ka_ref[...])
    bterm = r_ref[...] * kd_sum * rk_ref[...]
    o = [o_ref[0, :, c] + o_ref[1, :, c] for c in cols]
    mu = [_dot2(x, head_sum) * (1.0 / HEAD_DIM) for x in o]
    oc = [x - m for x, m in zip(o, mu)]
    var = [_dot(x * x, head_sum) * (1.0 / HEAD_DIM) for x in oc]
    bonus = [_dot(bterm[:, c], head_sum) * v_ref[:, c] for c in cols]
    for c, x, vr, bn in zip(cols, oc, var, bonus):
        y = x * lax.rsqrt(vr + GN_EPS) * lng_ref[:, c] + lnb_ref[:, c]
        out_ref[:, c] = ((y + bn) * gate[:, c]).astype(BF16)


def _rwkvpost_call(o, proj, lp, l):
    n_rows = o.shape[1]

    def col(cb):
        return lambda i: (i, cb)

    shared = pl.BlockSpec((None, 1, D_RWKV), lambda i: (l, 0, 0))
    return pl.pallas_call(
        _rwkvpost_kernel,
        out_shape=jax.ShapeDtypeStruct((n_rows, D_RWKV), BF16),
        grid=(n_rows // TM_POST,),
        in_specs=[
            pl.BlockSpec((2, TM_POST, D_RWKV), lambda i: (0, i, 0)),
            pl.BlockSpec((TM_POST, D_RWKV), col(0)),
            pl.BlockSpec((TM_POST, D_RWKV), col(1)),
            pl.BlockSpec((TM_POST, D_RWKV), col(2)),
            pl.BlockSpec((TM_POST, TN_IN), col(LO_BLOCK)),
            pl.BlockSpec((None, 2, LORA_A, D_RWKV), lambda i: (l, 0, 0, 0)),
            pl.BlockSpec((None, 2, 1, D_RWKV), lambda i: (l, 0, 0, 0)),
            shared, shared, shared, shared,
            pl.BlockSpec((None, LORA_G_PAD, D_RWKV), lambda i: (l, 0, 0)),
        ],
        out_specs=pl.BlockSpec((TM_POST, D_RWKV), lambda i: (i, 0)),
        compiler_params=_cparams(("parallel",)),
        name="rwkvpost",
    )(o, proj, proj, proj, proj, lp["a_up"], lp["a0"], lp["k_a"], lp["r_k"], lp["ln_g"], lp["ln_b"], lp["g_up"])


def _dft_mats(L):
    idx = (np.arange(L)[:, None] * np.arange(L)[None, :]) % (2 * L)
    ang = np.pi * idx.astype(np.float64) / L
    alt = np.cos(np.pi * np.arange(L))
    fc = np.cos(ang)
    fs = -np.sin(ang)
    fs[0, :] = alt
    fwd = np.concatenate([fc, fs], axis=0)
    ic = 2.0 * np.cos(ang.T)
    ic[:, 0] = 1.0
    isn = -2.0 * np.sin(ang.T)
    isn[:, 0] = alt
    inv = np.concatenate([ic, isn], axis=1) / (2 * L)
    return fwd.astype(np.float32), inv.astype(np.float32)


def _filter_features(L):
    t = np.linspace(0.0, 1.0, L, dtype=np.float32)[:, None]
    w = (2.0 * math.pi / L) * np.arange(L, dtype=np.float32)[:, None]
    f = np.linspace(1e-4, FILT_BANDS - 1, FILT_BANDS, dtype=np.float32)[None, :]
    z = np.concatenate([t, np.cos(f * w), -np.sin(f * w)], axis=-1)
    zp = np.zeros((L, FILT_EMB_PAD), np.float32)
    zp[:, :FILT_EMB] = z
    return zp, t


def _filter_deltas():
    max_decay = math.log(FILT_TARGET) / FAST_DECAY_PCT
    min_decay = math.log(FILT_TARGET) / SLOW_DECAY_PCT
    return np.abs(np.linspace(min_decay, max_decay, D_HYENA, dtype=np.float32))[None, :]


def _hyfilt_kernel(z_ref, t_ref, dl_ref, fwd_ref, w1_ref, b1_ref, w2_ref, b2_ref,
                   w3f0_ref, w3b0_ref, w3f1_ref, w3b1_ref, fr_ref, o_ref):
    L = z_ref.shape[0]
    tc = o_ref.shape[3]
    h = jnp.sin(fr_ref[0:1, :] * (_dot3(z_ref[...], w1_ref[...]) + b1_ref[...]))
    h = jnp.sin(fr_ref[1:2, :] * (_dot3(h, w2_ref[...]) + b2_ref[...]))
    decay = jnp.exp(-t_ref[...] * dl_ref[...])
    first = lax.broadcasted_iota(jnp.int32, (L, tc), 0) == 0
    fwd = fwd_ref[...]
    for order, (wf_ref, wb_ref) in enumerate(((w3f0_ref, w3b0_ref), (w3f1_ref, w3b1_ref))):
        hf = _dot3(h, wf_ref[...]) * decay
        hb = _dot3(h, wb_ref[...]) * decay
        norm = jnp.sum(jnp.abs(hf) + jnp.abs(hb), axis=0, keepdims=True)
        hf = hf / norm
        hb = jnp.where(first, 0.0, hb / norm)
        ks = _dot(fwd, hf + hb)
        kd = _dot(fwd, hf - hb)
        kr = ks[:L]
        o_ref[order, 0] = kr
        o_ref[order, 1] = jnp.where(first, 0.0, kd[L:])
        o_ref[order, 2] = jnp.where(first, ks[L:L + 1], kr)


def _hyfilt_call(L, fp, l):
    z, t = _filter_features(L)
    fwd, _ = _dft_mats(L)
    nct = D_HYENA // TC_FILT
    full = lambda shape: pl.BlockSpec(shape, lambda j: tuple(0 for _ in shape))
    layer = lambda shape: pl.BlockSpec((None,) + shape, lambda j: (l,) + tuple(0 for _ in shape))
    w3_spec = lambda grp: pl.BlockSpec((None, FILT_HIDDEN, TC_FILT), lambda j: (l, 0, grp * nct + j))
    w3 = fp["w3"]
    return pl.pallas_call(
        _hyfilt_kernel,
        out_shape=jax.ShapeDtypeStruct((2, 3, L, D_HYENA), F32),
        grid=(nct,),
        in_specs=[
            full((L, FILT_EMB_PAD)), full((L, 1)), pl.BlockSpec((1, TC_FILT), lambda j: (0, j)), full((2 * L, L)),
            layer((FILT_EMB_PAD, FILT_HIDDEN)), layer((1, FILT_HIDDEN)),
            layer((FILT_HIDDEN, FILT_HIDDEN)), layer((1, FILT_HIDDEN)),
            w3_spec(0), w3_spec(1), w3_spec(2), w3_spec(3),
            layer((2, FILT_HIDDEN)),
        ],
        out_specs=pl.BlockSpec((2, 3, L, TC_FILT), lambda j: (0, 0, 0, j)),
        compiler_params=_cparams(("parallel",)),
        name="hyfilt",
    )(jnp.asarray(z), jnp.asarray(t), jnp.asarray(_filter_deltas()), jnp.asarray(fwd).astype(BF16),
      fp["w1"], fp["b1"], fp["w2"], fp["b2"], w3, w3, w3, w3, fp["freq"])


def _hyena_kernel(u_ref, x1_ref, x2_ref, fwd_ref, inv_ref, spec_ref, bias_ref, out_ref):
    L = fwd_ref.shape[1]
    n_sub = u_ref.shape[0] // L

    def side_by_side(ref):
        return jnp.concatenate([ref[s * L:(s + 1) * L, :] for s in range(n_sub)], axis=1)

    def tiled(x):
        return jnp.concatenate([x] * n_sub, axis=1)

    def long_conv(u, order):
        spec = _dot(fwd_ref[...], u)
        ur, ui = spec[:L], spec[L:]
        kr, ki, kr2 = tiled(spec_ref[order, 0]), tiled(spec_ref[order, 1]), tiled(spec_ref[order, 2])
        yr = ur * kr - ui * ki
        yi = ur * ki + ui * kr2
        y = _dot(inv_ref[...], jnp.concatenate([yr, yi], axis=0))
        return y + u * tiled(bias_ref[order:order + 1, :])

    z = side_by_side(x1_ref) * long_conv(side_by_side(u_ref), 0)
    out = (side_by_side(x2_ref) * long_conv(z, 1)).astype(BF16)
    tc = out_ref.shape[1]
    for s in range(n_sub):
        out_ref[s * L:(s + 1) * L, :] = out[:, s * tc:(s + 1) * tc]


def _hyena_call(proj, spec, bias, l, n_seq, L, tc, n_sub):
    fwd, inv = _dft_mats(L)
    nct = D_HYENA // tc
    rows = n_sub * L
    cb0 = 3 * D_RWKV // tc

    def col(which):
        return lambda s, j: (s, cb0 + which * nct + j)

    in_specs = [
        pl.BlockSpec((rows, tc), col(0)),
        pl.BlockSpec((rows, tc), col(1)),
        pl.BlockSpec((rows, tc), col(2)),
        pl.BlockSpec((2 * L, L), lambda s, j: (0, 0), pipeline_mode=pl.Buffered(1)),
        pl.BlockSpec((L, 2 * L), lambda s, j: (0, 0), pipeline_mode=pl.Buffered(1)),
        pl.BlockSpec((2, 3, L, tc), lambda s, j: (0, 0, 0, j)),
        pl.BlockSpec((None, 2, tc), lambda s, j: (l, 0, j)),
    ]
    return pl.pallas_call(
        _hyena_kernel,
        out_shape=jax.ShapeDtypeStruct((n_seq * L, D_HYENA), BF16),
        grid=(n_seq // n_sub, nct),
        in_specs=in_specs,
        out_specs=pl.BlockSpec((rows, tc), lambda s, j: (s, j)),
        compiler_params=_cparams(("parallel", "parallel")),
        name="hyena",
    )(proj, proj, proj, jnp.asarray(fwd).astype(BF16), jnp.asarray(inv).astype(BF16), spec, bias)


def _outproj_kernel(a_ref, b_ref, x_ref, g_ref, w_ref, lg_ref, lb_ref, o_ref):
    mix = (jnp.dot(a_ref[...], w_ref[0:D_RWKV, :], preferred_element_type=F32)
           + jnp.dot(b_ref[...], w_ref[D_RWKV:, :], preferred_element_type=F32))
    y = ALPHA * x_ref[...] + g_ref[0, 0] * mix
    o_ref[...] = _layer_norm_rows(y) * lg_ref[...] + lb_ref[...]


def _outproj_call(a_out, b_out, x, mod_all, w_out, ln_g, ln_b, l, is_grid):
    grp = lambda i: _cond_row(i, TM_OUT, is_grid)
    vec = pl.BlockSpec((None, 1, D_MODEL), lambda i: (l, 0, 0))
    return pl.pallas_call(
        _outproj_kernel,
        out_shape=jax.ShapeDtypeStruct(x.shape, F32),
        grid=(x.shape[0] // TM_OUT,),
        in_specs=[
            pl.BlockSpec((TM_OUT, D_RWKV), lambda i: (i, 0)),
            pl.BlockSpec((TM_OUT, D_HYENA), lambda i: (i, 0)),
            pl.BlockSpec((TM_OUT, D_MODEL), lambda i: (i, 0)),
            _mod_spec(l, 2, grp),
            pl.BlockSpec((None, D_MODEL, D_MODEL), lambda i: (l, 0, 0)),
            vec, vec,
        ],
        out_specs=pl.BlockSpec((TM_OUT, D_MODEL), lambda i: (i, 0)),
        compiler_params=_cparams(("parallel",)),
        name="outproj",
    )(a_out, b_out, x, mod_all, w_out, ln_g, ln_b)


def _mlp_kernel(x_ref, sh_ref, sc_ref, g_ref, w1_ref, w2_ref, lg_ref, lb_ref, o_ref, h_ref):
    f = pl.program_id(1)
    row_blocks = o_ref.shape[0] // 128

    @pl.when(f == 0)
    def _():
        sc = 1.0 + sc_ref[0, 0]
        sh = sh_ref[0, 0]

        def body(rb, carry):
            rows = pl.ds(pl.multiple_of(rb * 128, 128), 128)
            h_ref[rows, :] = (_layer_norm_rows(x_ref[rows, :]) * sc + sh).astype(BF16)
            return carry

        lax.fori_loop(0, row_blocks, body, 0)
        o_ref[...] = jnp.zeros(o_ref.shape, F32)

    hid = jnp.maximum(jnp.dot(h_ref[...], w1_ref[...].astype(BF16), preferred_element_type=F32), 0.0)
    hid = (hid * hid).astype(BF16)
    o_ref[...] += jnp.dot(hid, w2_ref[...].astype(BF16), preferred_element_type=F32)

    @pl.when(f == pl.num_programs(1) - 1)
    def _():
        gate = g_ref[0, 0]
        lg = lg_ref[...]
        lb = lb_ref[...]

        def body(rb, carry):
            rows = pl.ds(pl.multiple_of(rb * 128, 128), 128)
            y = ALPHA * x_ref[rows, :] + gate * o_ref[rows, :]
            o_ref[rows, :] = _layer_norm_rows(y) * lg + lb
            return carry

        lax.fori_loop(0, row_blocks, body, 0)


def _mlp_call(x, mod_all, w1, w2, ln_g, ln_b, l, is_grid):
    grp = lambda i: _cond_row(i, TM_MLP, is_grid)
    vec = pl.BlockSpec((None, 1, D_MODEL), lambda i, f: (l, 0, 0))
    return pl.pallas_call(
        _mlp_kernel,
        out_shape=jax.ShapeDtypeStruct(x.shape, F32),
        grid=(x.shape[0] // TM_MLP, D_FF // TF_MLP),
        in_specs=[
            pl.BlockSpec((TM_MLP, D_MODEL), lambda i, f: (i, 0), pipeline_mode=pl.Buffered(1)),
            _mod_spec(l, 3, grp),
            _mod_spec(l, 4, grp),
            _mod_spec(l, 5, grp),
            pl.BlockSpec((None, D_MODEL, TF_MLP), lambda i, f: (l, 0, f)),
            pl.BlockSpec((None, TF_MLP, D_MODEL), lambda i, f: (l, f, 0)),
            vec, vec,
        ],
        out_specs=pl.BlockSpec((TM_MLP, D_MODEL), lambda i, f: (i, 0)),
        scratch_shapes=[pltpu.VMEM((TM_MLP, D_MODEL), BF16)],
        compiler_params=_cparams(("parallel", "arbitrary")),
        name="mlp",
    )(x, mod_all, mod_all, mod_all, w1, w2, ln_g, ln_b)


def kernel(x_prompt, x_sample, c, state_rwkv, c_ctx, w_ada, b_ada, w_in, conv_w, lora_w_up, lora_w0, lora_a_up, lora_a0, lora_g_up, rwkv_k_k, rwkv_k_a, rwkv_r_k, rwkv_ln_g, rwkv_ln_b, filt_w1, filt_b1, filt_w2, filt_b2, filt_w3, filt_freq, hyena_bias, w_out, ln1_g, ln1_b, ln2_g, ln2_b, mlp_w1, mlp_w2):
    n_ctx, ctx_seq, _ = x_prompt.shape
    n_lat, lat_seq, _ = x_sample.shape
    depth = w_ada.shape[0]
    assert ctx_seq == CTX_SEQ and lat_seq == LAT_SEQ and (n_ctx * ctx_seq) % TM == 0
    assert 1 + n_lat <= N_COND
    ctx_rows = n_ctx * ctx_seq
    lat_rows = n_lat * lat_seq

    xc = x_prompt.reshape(ctx_rows, D_MODEL)
    xl = x_sample.reshape(lat_rows, D_MODEL)
    cond = jnp.concatenate([c_ctx[None, :], c, jnp.zeros((N_COND - 1 - n_lat, D_MODEL), F32)], axis=0)
    mod_all = _mod_call(cond, w_ada, b_ada).reshape(depth, N_COND, 6, 1, D_MODEL)

    w_in_t = jnp.swapaxes(w_in, 1, 2)
    w_lo_bf = jnp.pad(w_in[:, :, N_CONV:].astype(BF16), ((0, 0), (0, 0), (0, TN_IN - (D_IN - N_CONV))))
    conv_w9 = conv_w.reshape(depth, 9, N_CONV)
    w_out_bf = w_out.astype(BF16)
    row = lambda a: a.reshape(depth, 1, a.shape[-1])
    lp = {
        "w_up": lora_w_up, "w0": lora_w0[:, :, None, :], "a_up": lora_a_up, "a0": lora_a0[:, :, None, :],
        "k_k": row(rwkv_k_k), "k_a": row(rwkv_k_a), "r_k": rwkv_r_k.reshape(depth, 1, D_RWKV),
        "ln_g": row(rwkv_ln_g), "ln_b": row(rwkv_ln_b),
        "g_up": jnp.pad(lora_g_up, ((0, 0), (0, LORA_G_PAD - LORA_G), (0, 0))),
    }
    fp = {
        "w1": jnp.pad(filt_w1, ((0, 0), (0, FILT_EMB_PAD - FILT_EMB), (0, 0))), "b1": row(filt_b1),
        "w2": filt_w2, "b2": row(filt_b2), "w3": filt_w3, "freq": filt_freq,
    }
    ln1_g, ln1_b, ln2_g, ln2_b = row(ln1_g), row(ln1_b), row(ln2_g), row(ln2_b)
    state_in = state_rwkv.astype(F32)

    def trunk_layer(x, l, is_grid, n_seq, seq_len, s0, hyena_tc, hyena_sub):
        proj = _inproj_call(x, mod_all, w_in_t, w_lo_bf, conv_w9, l, is_grid)
        scan = _wkv_call(proj, lp, l, s0, n_seq, seq_len, s0 is None)
        a_out = _rwkvpost_call(scan[0], proj, lp, l)
        spec = _hyfilt_call(seq_len, fp, l)
        b_out = _hyena_call(proj, spec, hyena_bias, l, n_seq, seq_len, hyena_tc, hyena_sub)
        x = _outproj_call(a_out, b_out, x, mod_all, w_out_bf, ln1_g, ln1_b, l, is_grid)
        x = _mlp_call(x, mod_all, mlp_w1, mlp_w2, ln2_g, ln2_b, l, is_grid)
        return x, (scan[1] if s0 is None else None)

    ctx_states = []
    lat_sub = 2 if n_lat % 2 == 0 else 1
    for l in range(depth):
        xc, s_ctx = trunk_layer(xc, l, False, n_ctx, ctx_seq, None, D_HYENA, 1)
        ctx_states.append(s_ctx)
        xl, _ = trunk_layer(xl, l, True, n_lat, lat_seq, state_in, 256, lat_sub)

    y_prompt = xc.reshape(n_ctx, ctx_seq, D_MODEL)
    y_sample = xl.reshape(n_lat, lat_seq, D_MODEL)
    new_state = jnp.stack(ctx_states, axis=1).astype(x_prompt.dtype)
    return (y_prompt, y_sample, new_state)
```

```python
import functools
import math

import jax
import jax.numpy as jnp
import numpy as np
from jax import lax
from jax.experimental import pallas as pl
from jax.experimental.pallas import tpu as pltpu

F32 = jnp.float32
BF16 = jnp.bfloat16

D_MODEL = 2048
D_RWKV = 1024
D_HYENA = 1024
HEAD_DIM = 64
N_HEADS = D_RWKV // HEAD_DIM
N_PAIRS = N_HEADS // 2
PAIR_W = 2 * HEAD_DIM
LORA_W = 64
LORA_A = 64
LORA_G = 160
LORA_G_PAD = 256
N_CONV = 3 * D_RWKV + 3 * D_HYENA
D_IN = N_CONV + LORA_W + LORA_A + LORA_G
D_FF = 4 * D_MODEL
GRID_W = 64
CTX_SEQ = 256
LAT_SEQ = 1024
FILT_BANDS = 16
FILT_EMB = 1 + 2 * FILT_BANDS
FILT_EMB_PAD = 128
FILT_HIDDEN = 64
DEPTH = 2
ALPHA = (2 * DEPTH) ** 0.25
LN_EPS = 1e-5
GN_EPS = 64e-5
FILT_TARGET = 1e-2
FAST_DECAY_PCT = 0.3
SLOW_DECAY_PCT = 1.5
DECAY_SCALE = math.exp(-0.5)

TM = 1024
TN_IN = 512
NJ_CONV = N_CONV // TN_IN
D_IN_PAD = (NJ_CONV + 1) * TN_IN
LO_BLOCK = N_CONV // TN_IN
CONV_PAD = 72
CHUNK = 64
WKV_GROUP = 4
WKV_ROWS = WKV_GROUP * CHUNK
WKV_PAR = 2
N_COND = 8
TN_MOD = 1024
TM_OUT = 512
TM_MLP = 1024
TF_MLP = 512
TM_POST = 256
TC_FILT = 512
VMEM_LIMIT = 56 * 1024 * 1024


def _cparams(sem):
    return pltpu.CompilerParams(dimension_semantics=sem, vmem_limit_bytes=VMEM_LIMIT)


def _dot(a, b):
    return jnp.dot(a.astype(BF16), b.astype(BF16), preferred_element_type=F32)


def _dot_nt(a, b):
    return lax.dot_general(a.astype(BF16), b.astype(BF16), (((1,), (1,)), ((), ())),
                           preferred_element_type=F32)


def _dot_tn(a, b):
    return lax.dot_general(a.astype(BF16), b.astype(BF16), (((0,), (0,)), ((), ())),
                           preferred_element_type=F32)


def _split(x):
    hi = x.astype(BF16)
    lo = (x - hi.astype(F32)).astype(BF16)
    return hi, lo


def _dot3(a, b):
    ah, al = _split(a)
    bh, bl = _split(b)
    return _dot(ah, bh) + (_dot(ah, bl) + _dot(al, bh))


def _dot2(a, b_exact):
    ah, al = _split(a)
    return _dot(ah, b_exact) + _dot(al, b_exact)


def _dot2_left(a_exact, b):
    bh, bl = _split(b)
    return _dot(a_exact, bh) + _dot(a_exact, bl)


def _sigmoid(x):
    return 1.0 / (1.0 + jnp.exp(-x))


def _layer_norm_rows(x):
    mu = jnp.mean(x, axis=-1, keepdims=True)
    xc = x - mu
    var = jnp.mean(xc * xc, axis=-1, keepdims=True)
    return xc * lax.rsqrt(var + LN_EPS)


def _cond_row(i, tm, is_grid):
    return 1 + (i * tm) // LAT_SEQ if is_grid else 0


def _mod_spec(l, which, grp):
    return pl.BlockSpec((None, 1, 1, 1, D_MODEL), lambda i, *_: (l, grp(i), which, 0, 0))


def _mod_kernel(c_ref, w_ref, b_ref, o_ref):
    c = c_ref[...]
    s = c * _sigmoid(c)
    o_ref[0] = _dot(s, w_ref[0]) + b_ref[0]


def _mod_call(cond, w_ada, b_ada):
    depth = w_ada.shape[0]
    n_out = w_ada.shape[2]
    return pl.pallas_call(
        _mod_kernel,
        out_shape=jax.ShapeDtypeStruct((depth, N_COND, n_out), F32),
        grid=(depth, n_out // TN_MOD),
        in_specs=[
            pl.BlockSpec((N_COND, D_MODEL), lambda l, j: (0, 0)),
            pl.BlockSpec((1, D_MODEL, TN_MOD), lambda l, j: (l, 0, j)),
            pl.BlockSpec((1, 1, TN_MOD), lambda l, j: (l, 0, j)),
        ],
        out_specs=pl.BlockSpec((1, N_COND, TN_MOD), lambda l, j: (l, 0, j)),
        compiler_params=_cparams(("parallel", "parallel")),
        name="mod",
    )(cond, w_ada, b_ada.reshape(depth, 1, n_out))


def _inproj_kernel(x_ref, sh_ref, sc_ref, w_ref, wlo_ref, cw_ref, o_ref, h_ref, wbf_ref, acc_a, acc_b, *,
                   is_grid):
    j = pl.program_id(1)
    tn = o_ref.shape[1]
    mid = slice(CONV_PAD, CONV_PAD + TM)

    n_blocks = TM // GRID_W
    n_parts = 4
    lane_w = 128
    n_chunk = 128

    def round_weights():
        for nc in range(tn // n_chunk):
            rows = slice(nc * n_chunk, (nc + 1) * n_chunk)
            wbf_ref[rows, :] = w_ref[rows, :].astype(BF16)

    def matmul_into(dst_ref, part=None):
        if part is None:
            dst_ref[mid, :] = _dot_nt(h_ref[...], wbf_ref[...])
        else:
            r0, r1 = part * (TM // n_parts), (part + 1) * (TM // n_parts)
            dst_ref[CONV_PAD + r0:CONV_PAD + r1, :] = _dot_nt(h_ref[r0:r1, :], wbf_ref[...])

    def conv_from(src_ref, g0=0, g1=n_blocks):
        row = lax.broadcasted_iota(jnp.int32, (GRID_W, lane_w), 0)
        first = row == 0
        last = row == GRID_W - 1
        per_seq = CTX_SEQ // GRID_W

        def shifted(s, lanes, mask_first, mask_last):
            start = CONV_PAD + s * GRID_W
            uc = src_ref[start:start + GRID_W, lanes]
            ul = src_ref[start - 1:start - 1 + GRID_W, lanes]
            ur = src_ref[start + 1:start + 1 + GRID_W, lanes]
            if mask_first:
                ul = jnp.where(first, 0.0, ul)
            if mask_last:
                ur = jnp.where(last, 0.0, ur)
            return ul, uc, ur

        for c in range(tn // lane_w):
            lanes = slice(c * lane_w, (c + 1) * lane_w)
            cw = cw_ref[:, lanes]
            if not is_grid:
                for s in range(g0, g1):
                    ul, uc, ur = shifted(s, lanes, s % per_seq == 0, s % per_seq == per_seq - 1)
                    o_ref[s * GRID_W:(s + 1) * GRID_W, lanes] = ul * cw[3:4, :] + uc * cw[4:5, :] + ur * cw[5:6, :]
                continue
            partial = {}
            for s in range(max(g0 - 1, 0), min(g1 + 1, n_blocks)):
                ul, uc, ur = shifted(s, lanes, True, True)
                for a in range(3):
                    gi = s - (a - 1)
                    if g0 <= gi < g1:
                        t = (ul * cw[3 * a:3 * a + 1, :] + uc * cw[3 * a + 1:3 * a + 2, :]
                             + ur * cw[3 * a + 2:3 * a + 3, :])
                        partial[gi] = t if gi not in partial else partial[gi] + t
                if s - 1 in partial:
                    o_ref[(s - 1) * GRID_W:s * GRID_W, lanes] = partial.pop(s - 1)
            for gi in sorted(partial):
                o_ref[gi * GRID_W:(gi + 1) * GRID_W, lanes] = partial.pop(gi)

    @pl.when(j == 0)
    def _():
        sc = 1.0 + sc_ref[0, 0]
        sh = sh_ref[0, 0]

        for acc in (acc_a, acc_b):
            acc[0:CONV_PAD, :] = jnp.zeros((CONV_PAD, tn), F32)
            acc[CONV_PAD + TM:, :] = jnp.zeros((CONV_PAD, tn), F32)
        part_rows = TM // n_parts
        for part in range(n_parts):
            for rb in range(part_rows // 128):
                rows = slice(part * part_rows + rb * 128, part * part_rows + (rb + 1) * 128)
                h = _layer_norm_rows(x_ref[rows, :]) * sc + sh
                h_ref[rows, :] = h.astype(BF16)
            rows = slice(part * part_rows, (part + 1) * part_rows)
            o_ref[rows, :] = jnp.dot(h_ref[rows, :], wlo_ref[...], preferred_element_type=F32)

    @pl.when(j == 1)
    def _():
        round_weights()
        matmul_into(acc_a)

    main = jnp.logical_and(j >= 2, j <= NJ_CONV)
    odd = jnp.bitwise_and(j, 1) == 1

    def overlapped(src_ref, dst_ref):
        round_weights()
        per_part = n_blocks // n_parts
        for part in range(n_parts):
            matmul_into(dst_ref, part)
            conv_from(src_ref, part * per_part, (part + 1) * per_part)

    @pl.when(jnp.logical_and(main, jnp.logical_not(odd)))
    def _():
        overlapped(acc_a, acc_b)

    @pl.when(jnp.logical_and(main, odd))
    def _():
        overlapped(acc_b, acc_a)

    @pl.when(j == NJ_CONV + 1)
    def _():
        conv_from(acc_b if NJ_CONV % 2 == 0 else acc_a)


def _inproj_call(x, mod_all, w_in, w_lo, conv_w, l, is_grid):
    n_units = x.shape[0] // TM
    kern = functools.partial(_inproj_kernel, is_grid=is_grid)
    grp = lambda i: _cond_row(i, TM, is_grid)
    tile = lambda t: jnp.clip(t, 0, NJ_CONV - 1)
    return pl.pallas_call(
        kern,
        out_shape=jax.ShapeDtypeStruct((x.shape[0], D_IN_PAD), F32),
        grid=(n_units, NJ_CONV + 2),
        in_specs=[
            pl.BlockSpec((TM, D_MODEL), lambda i, j: (i, 0)),
            _mod_spec(l, 0, grp),
            _mod_spec(l, 1, grp),
            pl.BlockSpec((None, TN_IN, D_MODEL), lambda i, j: (l, tile(j - 1), 0)),
            pl.BlockSpec((None, D_MODEL, TN_IN), lambda i, j: (l, 0, 0)),
            pl.BlockSpec((None, 9, TN_IN), lambda i, j: (l, 0, tile(j - 2))),
        ],
        out_specs=pl.BlockSpec((TM, TN_IN), lambda i, j: (i, jnp.where(j == 0, LO_BLOCK, tile(j - 2)))),
        scratch_shapes=[
            pltpu.VMEM((TM, D_MODEL), BF16),
            pltpu.VMEM((TN_IN, D_MODEL), BF16),
            pltpu.VMEM((TM + 2 * CONV_PAD, TN_IN), F32),
            pltpu.VMEM((TM + 2 * CONV_PAD, TN_IN), F32),
        ],
        compiler_params=_cparams(("parallel", "arbitrary")),
        name="inproj",
    )(x, mod_all, mod_all, w_in, w_lo, conv_w)


def _stack_heads(z, head0):
    return jnp.concatenate([jnp.where(head0, z, 0.0), jnp.where(head0, 0.0, z)], axis=0)


def _unstack_heads(z):
    c = z.shape[0] // 2
    return z[:c] + z[c:]


def _wkv_kernel(*refs, has_init, has_state_out):
    (r_ref, k_ref, v_ref, lo_ref, wup_ref, w0_ref, aup_ref, a0_ref, kk_ref, ka_ref) = refs[:10]
    pos = 10
    s0_ref = None
    if has_init:
        s0_ref = refs[pos]
        pos += 1
    o_ref = refs[pos]
    pos += 1
    sout_ref = None
    if has_state_out:
        sout_ref = refs[pos]
        pos += 1
    (s_ref, lw_ref, cum_ref, kkn_ref, b_ref, kd_ref, msk_ref,
     w1_ref, w2_ref, yhat_ref, kv_ref, mv_ref, dec_ref) = refs[pos:]

    d = pl.program_id(0)
    g = pl.program_id(2)
    C = CHUNK
    pairs = range(N_PAIRS)
    cols = [slice(p * PAIR_W, (p + 1) * PAIR_W) for p in pairs]

    @pl.when(g == 0)
    def _():
        if has_init:
            zero = jnp.zeros((HEAD_DIM, HEAD_DIM), F32)
            for p in pairs:
                top = jnp.concatenate([s0_ref[2 * p], zero], axis=1)
                bottom = jnp.concatenate([zero, s0_ref[2 * p + 1]], axis=1)
                s_ref[p] = jnp.concatenate([top, bottom], axis=0)
        else:
            s_ref[...] = jnp.zeros(s_ref.shape, F32)

    sgn = 1 - 2 * d
    si = lax.broadcasted_iota(jnp.int32, (2 * C, 2 * C), 0)
    sj = lax.broadcasted_iota(jnp.int32, (2 * C, 2 * C), 1)
    same_head = (si >= C) == (sj >= C)
    dlt = (jnp.bitwise_and(si, C - 1) - jnp.bitwise_and(sj, C - 1)) * sgn
    msk_ref[0] = jnp.where(jnp.logical_and(same_head, dlt < 0), 1.0, 0.0)
    msk_ref[1] = jnp.where(jnp.logical_and(same_head, dlt >= 0), 1.0, 0.0)
    msk_ref[2] = jnp.where(si == sj, 1.0, 0.0)
    blockdiag = (si >= HEAD_DIM) == (sj >= HEAD_DIM)
    msk_ref[3] = jnp.where(blockdiag, 1.0, 0.0)
    head_sum = jnp.where(blockdiag, 1.0, 0.0).astype(BF16)

    lo = lo_ref[...]
    xw = lo[:, 0:LORA_W]
    xa = lo[:, LORA_W:LORA_W + LORA_A]
    logw = -DECAY_SCALE * _sigmoid(w0_ref[...] + _dot(jnp.tanh(xw), wup_ref[...]))
    lw_ref[...] = logw
    gi = lax.broadcasted_iota(jnp.int32, (WKV_ROWS, WKV_ROWS), 0)
    gj = lax.broadcasted_iota(jnp.int32, (WKV_ROWS, WKV_ROWS), 1)
    same_chunk = jnp.bitwise_and(gi, -C) == jnp.bitwise_and(gj, -C)
    cum_mask = jnp.where(jnp.logical_and(same_chunk, (gi - gj) * sgn >= 0), 1.0, 0.0).astype(BF16)
    cum_ref[...] = _dot2_left(cum_mask, logw)
    iclr = _sigmoid(a0_ref[...] + _dot(xa, aup_ref[...]))
    k = k_ref[...]
    kd_ref[...] = k * (1.0 + (iclr - 1.0) * ka_ref[...])
    kkr = k * kk_ref[...]
    for p in pairs:
        slab = kkr[:, cols[p]]
        kkn = slab * lax.rsqrt(_dot(slab * slab, head_sum) + 1e-12)
        kkn_ref[:, cols[p]] = kkn
        b_ref[:, cols[p]] = kkn * iclr[:, cols[p]]

    def chunk_rows(i):
        cc = jnp.where(d == 0, i, WKV_GROUP - 1 - i)
        return pl.ds(pl.multiple_of(cc * C, C), C)

    def phase_a(it, carry):
        chunk_ids = [it * WKV_PAR + ci for ci in range(WKV_PAR)]
        units = [(ci, c) for ci in chunk_ids for c in cols]
        chunk_of = [ci for ci in chunk_ids for _ in cols]
        pair_of = [p for _ in chunk_ids for p in pairs]
        head0 = lax.broadcasted_iota(jnp.int32, (C, PAIR_W), 1) < HEAD_DIM
        strict_t = msk_ref[0]
        incl = msk_ref[1]
        eye = msk_ref[2]
        bd_f = msk_ref[3]

        lw = [lw_ref[chunk_rows(ci), c] for ci, c in units]
        cum = [cum_ref[chunk_rows(ci), c] for ci, c in units]
        kkn = [kkn_ref[chunk_rows(ci), c] for ci, c in units]
        bb = [b_ref[chunk_rows(ci), c] for ci, c in units]
        kd = [kd_ref[chunk_rows(ci), c] for ci, c in units]
        rr = [r_ref[chunk_rows(ci), c] for ci, c in units]
        vv = [v_ref[chunk_rows(ci), c] for ci, c in units]
        cum_prev = [cm - x for cm, x in zip(cum, lw)]
        mid = [cm[C // 2:C // 2 + 1, :] for cm in cum]
        total = [jnp.where(d == 0, cm[C - 1:C, :], cm[0:1, :]) for cm in cum]
        e_in = [jnp.exp(cm - m) for cm, m in zip(cum, mid)]
        e_in_prev = [jnp.exp(cm - m) for cm, m in zip(cum_prev, mid)]
        e_out = [jnp.exp(m - cm) for cm, m in zip(cum, mid)]
        e_tail = [jnp.exp(t - cm) for cm, t in zip(cum, total)]
        a1s = [_stack_heads(x * e, head0) for x, e in zip(kkn, e_in_prev)]
        r1s = [_stack_heads(x * e, head0) for x, e in zip(rr, e_in)]
        b1 = [x * e for x, e in zip(bb, e_out)]
        k1 = [x * e for x, e in zip(kd, e_out)]
        a0s = [_stack_heads(x * jnp.exp(cm), head0) for x, cm in zip(kkn, cum_prev)]
        vs = [_stack_heads(x, head0) for x in vv]
        g_t = [_dot_nt(jnp.concatenate([b, k], axis=0), jnp.concatenate([a, r_], axis=0))
               for a, r_, b, k in zip(a1s, r1s, b1, k1)]
        strict_s = strict_t[:C] + strict_t[C:]
        eye_s = eye[:C] + eye[C:]
        incl_s = strict_s + eye_s
        stack = lambda z: _stack_heads(z, head0)
        l_ab_s = [g_[:C, :PAIR_W] * strict_s for g_ in g_t]
        l_ak_t = [stack(g_[C:, :PAIR_W] * strict_s) for g_ in g_t]
        m_rb_t = [stack(g_[:C, PAIR_W:] * incl_s) for g_ in g_t]
        m_rk = [_dot_nt(a, jnp.concatenate([k, k], axis=0)) * incl for a, k in zip(r1s, k1)]
        lakv_t = [_dot_tn(a, b) for a, b in zip(vs, l_ak_t)]
        mv = [_unstack_heads(_dot(a, b)) for a, b in zip(m_rk, vs)]
        kv = [_dot_tn(a, b * e) * bd_f for a, b, e in zip(vv, kd, e_tail)]
        x = [eye_s - l_ for l_ in l_ab_s]
        q = [_dot(l_, stack(l_)) for l_ in l_ab_s]
        step = 2
        while 2 * step < C:
            prod = [_dot(jnp.concatenate([xx, qq], axis=0), stack(qq)) for xx, qq in zip(x, q)]
            x = [xx + pr[:C] for xx, pr in zip(x, prod)]
            q = [pr[C:] for pr in prod]
            step *= 2
        x = [stack(xx + _dot(xx, stack(qq))) for xx, qq in zip(x, q)]
        a_hat_t = [_dot_tn(a, xx) for xx, a in zip(x, a0s)]
        y_hat_t = [_dot(a, xx) for xx, a in zip(x, lakv_t)]
        zero_rows = jnp.zeros((C, PAIR_W), F32)
        for u, (i, p) in enumerate(zip(chunk_of, pair_of)):
            r0_t = jnp.concatenate([rr[u] * jnp.exp(cum[u]), zero_rows], axis=0).T
            w1_ref[i, p] = jnp.concatenate([a_hat_t[u], r0_t], axis=1).astype(BF16)
            w2_ref[i, p] = jnp.concatenate([stack(bb[u] * e_tail[u]), m_rb_t[u]], axis=1).astype(BF16)
            yhat_ref[i, p] = y_hat_t[u]
            kv_ref[i, p] = kv[u]
            mv_ref[i, p] = mv[u]
            dec_ref[i, p] = jnp.broadcast_to(jnp.exp(total[u]), (8, PAIR_W))
        return carry

    def phase_b(i, s_old):
        rows = chunk_rows(i)
        s_bf = [s.astype(BF16) for s in s_old]
        sw = [_dot(s_bf[p], w1_ref[i, p]) for p in pairs]
        yt = [(sw[p][:, :PAIR_W] + yhat_ref[i, p]).astype(BF16) for p in pairs]
        yw = [_dot(yt[p], w2_ref[i, p]) for p in pairs]
        s_new = [dec_ref[i, p, 0:1, :] * s_old[p] + kv_ref[i, p] - yw[p][:, :PAIR_W] for p in pairs]
        o = [sw[p][:, PAIR_W:].T[:C] + mv_ref[i, p] - _unstack_heads(yw[p][:, PAIR_W:].T) for p in pairs]
        for p in pairs:
            o_ref[0, rows, cols[p]] = o[p]
        return s_new

    lax.fori_loop(0, WKV_GROUP // WKV_PAR, phase_a, 0)
    state = [s_ref[p] for p in pairs]
    for i in range(WKV_GROUP):
        state = phase_b(i, state)
    for p in pairs:
        s_ref[p] = state[p]

    if has_state_out:
        @pl.when(g == pl.num_programs(2) - 1)
        def _():
            for p in pairs:
                s = s_ref[p]
                sout_ref[2 * p] = s[:HEAD_DIM, :HEAD_DIM]
                sout_ref[2 * p + 1] = s[HEAD_DIM:, HEAD_DIM:]


def _wkv_call(proj, lp, l, s0, n_seq, seq_len, has_state_out):
    n_groups = seq_len // WKV_ROWS
    has_init = s0 is not None

    def group(d, s, g):
        return s * n_groups + g + d * (n_groups - 1 - 2 * g)

    def col(cb):
        return lambda d, s, g: (group(d, s, g), cb)

    per_dir_mat = pl.BlockSpec((None, None, LORA_W, D_RWKV), lambda d, s, g: (l, d, 0, 0))
    per_dir_vec = pl.BlockSpec((None, None, 1, D_RWKV), lambda d, s, g: (l, d, 0, 0))
    shared = pl.BlockSpec((None, 1, D_RWKV), lambda d, s, g: (l, 0, 0))
    in_specs = [
        pl.BlockSpec((WKV_ROWS, D_RWKV), col(0)),
        pl.BlockSpec((WKV_ROWS, D_RWKV), col(1)),
        pl.BlockSpec((WKV_ROWS, D_RWKV), col(2)),
        pl.BlockSpec((WKV_ROWS, TN_IN), col(LO_BLOCK)),
        per_dir_mat, per_dir_vec, per_dir_mat, per_dir_vec, shared, shared,
    ]
    args = [proj, proj, proj, proj, lp["w_up"], lp["w0"], lp["a_up"], lp["a0"], lp["k_k"], lp["k_a"]]
    if has_init:
        in_specs.append(pl.BlockSpec((None, None, None, N_HEADS, HEAD_DIM, HEAD_DIM),
                                     lambda d, s, g: (s, l, d, 0, 0, 0)))
        args.append(s0)
    out_shape = [jax.ShapeDtypeStruct((2, n_seq * seq_len, D_RWKV), F32)]
    out_specs = [pl.BlockSpec((1, WKV_ROWS, D_RWKV), lambda d, s, g: (d, group(d, s, g), 0))]
    if has_state_out:
        out_shape.append(jax.ShapeDtypeStruct((n_seq, 2, N_HEADS, HEAD_DIM, HEAD_DIM), F32))
        out_specs.append(pl.BlockSpec((None, None, N_HEADS, HEAD_DIM, HEAD_DIM), lambda d, s, g: (s, d, 0, 0, 0)))
    kern = functools.partial(_wkv_kernel, has_init=has_init, has_state_out=has_state_out)
    per_chunk = (WKV_GROUP, N_PAIRS)
    return pl.pallas_call(
        kern,
        out_shape=out_shape,
        grid=(2, n_seq, n_groups),
        in_specs=in_specs,
        out_specs=out_specs,
        scratch_shapes=[
            pltpu.VMEM((N_PAIRS, PAIR_W, PAIR_W), F32),
            pltpu.VMEM((WKV_ROWS, D_RWKV), F32),
            pltpu.VMEM((WKV_ROWS, D_RWKV), F32),
            pltpu.VMEM((WKV_ROWS, D_RWKV), F32),
            pltpu.VMEM((WKV_ROWS, D_RWKV), F32),
            pltpu.VMEM((WKV_ROWS, D_RWKV), F32),
            pltpu.VMEM((4, PAIR_W, PAIR_W), F32),
            pltpu.VMEM(per_chunk + (PAIR_W, 2 * PAIR_W), BF16),
            pltpu.VMEM(per_chunk + (PAIR_W, 2 * PAIR_W), BF16),
            pltpu.VMEM(per_chunk + (PAIR_W, PAIR_W), F32),
            pltpu.VMEM(per_chunk + (PAIR_W, PAIR_W), F32),
            pltpu.VMEM(per_chunk + (CHUNK, PAIR_W), F32),
            pltpu.VMEM(per_chunk + (8, PAIR_W), F32),
        ],
        compiler_params=_cparams(("parallel", "parallel", "arbitrary")),
        name="wkv",
    )(*args)


def _rwkvpost_kernel(o_ref, r_ref, k_ref, v_ref, lo_ref, aup_ref, a0_ref, ka_ref, rk_ref, lng_ref, lnb_ref,
                     gup_ref, out_ref):
    lo = lo_ref[...]
    xa = lo[:, LORA_W:LORA_W + LORA_A]
    sg = _sigmoid(lo[:, LORA_W + LORA_A:LORA_W + LORA_A + LORA_G_PAD])
    gi = lax.broadcasted_iota(jnp.int32, (PAIR_W, PAIR_W), 0)
    gj = lax.broadcasted_iota(jnp.int32, (PAIR_W, PAIR_W), 1)
    head_sum = jnp.where((gi >= HEAD_DIM) == (gj >= HEAD_DIM), 1.0, 0.0).astype(BF16)
    cols = [slice(p * PAIR_W, (p + 1) * PAIR_W) for p in range(N_PAIRS)]
    gate = _dot(sg, gup_ref[...])
    ic = _sigmoid(a0_ref[0] + _dot(xa, aup_ref[0])) + _sigmoid(a0_ref[1] + _dot(xa, aup_ref[1]))
    kd_sum = k_ref[...] * (2.0 + (ic - 2.0) * ka_ref[...])
    bterm = r_ref[...] * kd_sum * rk_ref[...]
    o = [o_ref[0, :, c] + o_ref[1, :, c] for c in cols]
    mu = [_dot2(x, head_sum) * (1.0 / HEAD_DIM) for x in o]
    oc = [x - m for x, m in zip(o, mu)]
    var = [_dot(x * x, head_sum) * (1.0 / HEAD_DIM) for x in oc]
    bonus = [_dot(bterm[:, c], head_sum) * v_ref[:, c] for c in cols]
    for c, x, vr, bn in zip(cols, oc, var, bonus):
        y = x * lax.rsqrt(vr + GN_EPS) * lng_ref[:, c] + lnb_ref[:, c]
        out_ref[:, c] = ((y + bn) * gate[:, c]).astype(BF16)


def _rwkvpost_call(o, proj, lp, l):
    n_rows = o.shape[1]

    def col(cb):
        return lambda i: (i, cb)

    shared = pl.BlockSpec((None, 1, D_RWKV), lambda i: (l, 0, 0))
    return pl.pallas_call(
        _rwkvpost_kernel,
        out_shape=jax.ShapeDtypeStruct((n_rows, D_RWKV), BF16),
        grid=(n_rows // TM_POST,),
        in_specs=[
            pl.BlockSpec((2, TM_POST, D_RWKV), lambda i: (0, i, 0)),
            pl.BlockSpec((TM_POST, D_RWKV), col(0)),
            pl.BlockSpec((TM_POST, D_RWKV), col(1)),
            pl.BlockSpec((TM_POST, D_RWKV), col(2)),
            pl.BlockSpec((TM_POST, TN_IN), col(LO_BLOCK)),
            pl.BlockSpec((None, 2, LORA_A, D_RWKV), lambda i: (l, 0, 0, 0)),
            pl.BlockSpec((None, 2, 1, D_RWKV), lambda i: (l, 0, 0, 0)),
            shared, shared, shared, shared,
            pl.BlockSpec((None, LORA_G_PAD, D_RWKV), lambda i: (l, 0, 0)),
        ],
        out_specs=pl.BlockSpec((TM_POST, D_RWKV), lambda i: (i, 0)),
        compiler_params=_cparams(("parallel",)),
        name="rwkvpost",
    )(o, proj, proj, proj, proj, lp["a_up"], lp["a0"], lp["k_a"], lp["r_k"], lp["ln_g"], lp["ln_b"], lp["g_up"])


def _dft_mats(L):
    idx = (np.arange(L)[:, None] * np.arange(L)[None, :]) % (2 * L)
    ang = np.pi * idx.astype(np.float64) / L
    alt = np.cos(np.pi * np.arange(L))
    fc = np.cos(ang)
    fs = -np.sin(ang)
    fs[0, :] = alt
    fwd = np.concatenate([fc, fs], axis=0)
    ic = 2.0 * np.cos(ang.T)
    ic[:, 0] = 1.0
    isn = -2.0 * np.sin(ang.T)
    isn[:, 0] = alt
    inv = np.concatenate([ic, isn], axis=1) / (2 * L)
    return fwd.astype(np.float32), inv.astype(np.float32)


def _filter_features(L):
    t = np.linspace(0.0, 1.0, L, dtype=np.float32)[:, None]
    w = (2.0 * math.pi / L) * np.arange(L, dtype=np.float32)[:, None]
    f = np.linspace(1e-4, FILT_BANDS - 1, FILT_BANDS, dtype=np.float32)[None, :]
    z = np.concatenate([t, np.cos(f * w), -np.sin(f * w)], axis=-1)
    zp = np.zeros((L, FILT_EMB_PAD), np.float32)
    zp[:, :FILT_EMB] = z
    return zp, t


def _filter_deltas():
    max_decay = math.log(FILT_TARGET) / FAST_DECAY_PCT
    min_decay = math.log(FILT_TARGET) / SLOW_DECAY_PCT
    return np.abs(np.linspace(min_decay, max_decay, D_HYENA, dtype=np.float32))[None, :]


def _hyfilt_kernel(z_ref, t_ref, dl_ref, fwd_ref, w1_ref, b1_ref, w2_ref, b2_ref,
                   w3f0_ref, w3b0_ref, w3f1_ref, w3b1_ref, fr_ref, o_ref):
    L = z_ref.shape[0]
    tc = o_ref.shape[3]
    h = jnp.sin(fr_ref[0:1, :] * (_dot3(z_ref[...], w1_ref[...]) + b1_ref[...]))
    h = jnp.sin(fr_ref[1:2, :] * (_dot3(h, w2_ref[...]) + b2_ref[...]))
    decay = jnp.exp(-t_ref[...] * dl_ref[...])
    first = lax.broadcasted_iota(jnp.int32, (L, tc), 0) == 0
    fwd = fwd_ref[...]
    for order, (wf_ref, wb_ref) in enumerate(((w3f0_ref, w3b0_ref), (w3f1_ref, w3b1_ref))):
        hf = _dot3(h, wf_ref[...]) * decay
        hb = _dot3(h, wb_ref[...]) * decay
        norm = jnp.sum(jnp.abs(hf) + jnp.abs(hb), axis=0, keepdims=True)
        hf = hf / norm
        hb = jnp.where(first, 0.0, hb / norm)
        ks = _dot(fwd, hf + hb)
        kd = _dot(fwd, hf - hb)
        kr = ks[:L]
        o_ref[order, 0] = kr
        o_ref[order, 1] = jnp.where(first, 0.0, kd[L:])
        o_ref[order, 2] = jnp.where(first, ks[L:L + 1], kr)


def _hyfilt_call(L, fp, l):
    z, t = _filter_features(L)
    fwd, _ = _dft_mats(L)
    nct = D_HYENA // TC_FILT
    full = lambda shape: pl.BlockSpec(shape, lambda j: tuple(0 for _ in shape))
    layer = lambda shape: pl.BlockSpec((None,) + shape, lambda j: (l,) + tuple(0 for _ in shape))
    w3_spec = lambda grp: pl.BlockSpec((None, FILT_HIDDEN, TC_FILT), lambda j: (l, 0, grp * nct + j))
    w3 = fp["w3"]
    return pl.pallas_call(
        _hyfilt_kernel,
        out_shape=jax.ShapeDtypeStruct((2, 3, L, D_HYENA), F32),
        grid=(nct,),
        in_specs=[
            full((L, FILT_EMB_PAD)), full((L, 1)), pl.BlockSpec((1, TC_FILT), lambda j: (0, j)), full((2 * L, L)),
            layer((FILT_EMB_PAD, FILT_HIDDEN)), layer((1, FILT_HIDDEN)),
            layer((FILT_HIDDEN, FILT_HIDDEN)), layer((1, FILT_HIDDEN)),
            w3_spec(0), w3_spec(1), w3_spec(2), w3_spec(3),
            layer((2, FILT_HIDDEN)),
        ],
        out_specs=pl.BlockSpec((2, 3, L, TC_FILT), lambda j: (0, 0, 0, j)),
        compiler_params=_cparams(("parallel",)),
        name="hyfilt",
    )(jnp.asarray(z), jnp.asarray(t), jnp.asarray(_filter_deltas()), jnp.asarray(fwd).astype(BF16),
      fp["w1"], fp["b1"], fp["w2"], fp["b2"], w3, w3, w3, w3, fp["freq"])


def _hyena_kernel(u_ref, x1_ref, x2_ref, fwd_ref, inv_ref, spec_ref, bias_ref, out_ref):
    L = fwd_ref.shape[1]
    n_sub = u_ref.shape[0] // L

    def side_by_side(ref):
        return jnp.concatenate([ref[s * L:(s + 1) * L, :] for s in range(n_sub)], axis=1)

    def tiled(x):
        return jnp.concatenate([x] * n_sub, axis=1)

    def long_conv(u, order):
        spec = _dot(fwd_ref[...], u)
        ur, ui = spec[:L], spec[L:]
        kr, ki, kr2 = tiled(spec_ref[order, 0]), tiled(spec_ref[order, 1]), tiled(spec_ref[order, 2])
        yr = ur * kr - ui * ki
        yi = ur * ki + ui * kr2
        y = _dot(inv_ref[...], jnp.concatenate([yr, yi], axis=0))
        return y + u * tiled(bias_ref[order:order + 1, :])

    z = side_by_side(x1_ref) * long_conv(side_by_side(u_ref), 0)
    out = (side_by_side(x2_ref) * long_conv(z, 1)).astype(BF16)
    tc = out_ref.shape[1]
    for s in range(n_sub):
        out_ref[s * L:(s + 1) * L, :] = out[:, s * tc:(s + 1) * tc]


def _hyena_call(proj, spec, bias, l, n_seq, L, tc, n_sub):
    fwd, inv = _dft_mats(L)
    nct = D_HYENA // tc
    rows = n_sub * L
    cb0 = 3 * D_RWKV // tc

    def col(which):
        return lambda s, j: (s, cb0 + which * nct + j)

    in_specs = [
        pl.BlockSpec((rows, tc), col(0)),
        pl.BlockSpec((rows, tc), col(1)),
        pl.BlockSpec((rows, tc), col(2)),
        pl.BlockSpec((2 * L, L), lambda s, j: (0, 0), pipeline_mode=pl.Buffered(1)),
        pl.BlockSpec((L, 2 * L), lambda s, j: (0, 0), pipeline_mode=pl.Buffered(1)),
        pl.BlockSpec((2, 3, L, tc), lambda s, j: (0, 0, 0, j)),
        pl.BlockSpec((None, 2, tc), lambda s, j: (l, 0, j)),
    ]
    return pl.pallas_call(
        _hyena_kernel,
        out_shape=jax.ShapeDtypeStruct((n_seq * L, D_HYENA), BF16),
        grid=(n_seq // n_sub, nct),
        in_specs=in_specs,
        out_specs=pl.BlockSpec((rows, tc), lambda s, j: (s, j)),
        compiler_params=_cparams(("parallel", "parallel")),
        name="hyena",
    )(proj, proj, proj, jnp.asarray(fwd).astype(BF16), jnp.asarray(inv).astype(BF16), spec, bias)


def _outproj_kernel(a_ref, b_ref, x_ref, g_ref, w_ref, lg_ref, lb_ref, o_ref):
    mix = (jnp.dot(a_ref[...], w_ref[0:D_RWKV, :], preferred_element_type=F32)
           + jnp.dot(b_ref[...], w_ref[D_RWKV:, :], preferred_element_type=F32))
    y = ALPHA * x_ref[...] + g_ref[0, 0] * mix
    o_ref[...] = _layer_norm_rows(y) * lg_ref[...] + lb_ref[...]


def _outproj_call(a_out, b_out, x, mod_all, w_out, ln_g, ln_b, l, is_grid):
    grp = lambda i: _cond_row(i, TM_OUT, is_grid)
    vec = pl.BlockSpec((None, 1, D_MODEL), lambda i: (l, 0, 0))
    return pl.pallas_call(
        _outproj_kernel,
        out_shape=jax.ShapeDtypeStruct(x.shape, F32),
        grid=(x.shape[0] // TM_OUT,),
        in_specs=[
            pl.BlockSpec((TM_OUT, D_RWKV), lambda i: (i, 0)),
            pl.BlockSpec((TM_OUT, D_HYENA), lambda i: (i, 0)),
            pl.BlockSpec((TM_OUT, D_MODEL), lambda i: (i, 0)),
            _mod_spec(l, 2, grp),
            pl.BlockSpec((None, D_MODEL, D_MODEL), lambda i: (l, 0, 0)),
            vec, vec,
        ],
        out_specs=pl.BlockSpec((TM_OUT, D_MODEL), lambda i: (i, 0)),
        compiler_params=_cparams(("parallel",)),
        name="outproj",
    )(a_out, b_out, x, mod_all, w_out, ln_g, ln_b)


def _mlp_kernel(x_ref, sh_ref, sc_ref, g_ref, w1_ref, w2_ref, lg_ref, lb_ref, o_ref, h_ref):
    f = pl.program_id(1)
    last = pl.num_programs(1) - 1
    tm = o_ref.shape[0]
    n_parts = 4
    part_rows = tm // n_parts

    def partial_sum(rows, w1b, w2b):
        hid = jnp.maximum(jnp.dot(h_ref[rows, :], w1b, preferred_element_type=F32), 0.0)
        return jnp.dot((hid * hid).astype(BF16), w2b, preferred_element_type=F32)

    @pl.when(f == 0)
    def _():
        sc = 1.0 + sc_ref[0, 0]
        sh = sh_ref[0, 0]
        w1b = w1_ref[...].astype(BF16)
        w2b = w2_ref[...].astype(BF16)
        for part in range(n_parts):
            for rb in range(part_rows // 128):
                rows = slice(part * part_rows + rb * 128, part * part_rows + (rb + 1) * 128)
                h_ref[rows, :] = (_layer_norm_rows(x_ref[rows, :]) * sc + sh).astype(BF16)
            rows = slice(part * part_rows, (part + 1) * part_rows)
            o_ref[rows, :] = partial_sum(rows, w1b, w2b)

    @pl.when(jnp.logical_and(f > 0, f < last))
    def _():
        o_ref[...] += partial_sum(slice(0, tm), w1_ref[...].astype(BF16), w2_ref[...].astype(BF16))

    @pl.when(f == last)
    def _():
        gate = g_ref[0, 0]
        lg = lg_ref[...]
        lb = lb_ref[...]
        w1b = w1_ref[...].astype(BF16)
        w2b = w2_ref[...].astype(BF16)
        for part in range(n_parts):
            rows = slice(part * part_rows, (part + 1) * part_rows)
            o_ref[rows, :] += partial_sum(rows, w1b, w2b)
            for rb in range(part_rows // 128):
                rows = slice(part * part_rows + rb * 128, part * part_rows + (rb + 1) * 128)
                y = ALPHA * x_ref[rows, :] + gate * o_ref[rows, :]
                o_ref[rows, :] = _layer_norm_rows(y) * lg + lb


def _mlp_call(x, mod_all, w1, w2, ln_g, ln_b, l, is_grid):
    grp = lambda i: _cond_row(i, TM_MLP, is_grid)
    vec = pl.BlockSpec((None, 1, D_MODEL), lambda i, f: (l, 0, 0))
    return pl.pallas_call(
        _mlp_kernel,
        out_shape=jax.ShapeDtypeStruct(x.shape, F32),
        grid=(x.shape[0] // TM_MLP, D_FF // TF_MLP),
        in_specs=[
            pl.BlockSpec((TM_MLP, D_MODEL), lambda i, f: (i, 0), pipeline_mode=pl.Buffered(1)),
            _mod_spec(l, 3, grp),
            _mod_spec(l, 4, grp),
            _mod_spec(l, 5, grp),
            pl.BlockSpec((None, D_MODEL, TF_MLP), lambda i, f: (l, 0, f)),
            pl.BlockSpec((None, TF_MLP, D_MODEL), lambda i, f: (l, f, 0)),
            vec, vec,
        ],
        out_specs=pl.BlockSpec((TM_MLP, D_MODEL), lambda i, f: (i, 0)),
        scratch_shapes=[pltpu.VMEM((TM_MLP, D_MODEL), BF16)],
        compiler_params=_cparams(("parallel", "arbitrary")),
        name="mlp",
    )(x, mod_all, mod_all, mod_all, w1, w2, ln_g, ln_b)


def kernel(x_prompt, x_sample, c, state_rwkv, c_ctx, w_ada, b_ada, w_in, conv_w, lora_w_up, lora_w0, lora_a_up, lora_a0, lora_g_up, rwkv_k_k, rwkv_k_a, rwkv_r_k, rwkv_ln_g, rwkv_ln_b, filt_w1, filt_b1, filt_w2, filt_b2, filt_w3, filt_freq, hyena_bias, w_out, ln1_g, ln1_b, ln2_g, ln2_b, mlp_w1, mlp_w2):
    n_ctx, ctx_seq, _ = x_prompt.shape
    n_lat, lat_seq, _ = x_sample.shape
    depth = w_ada.shape[0]
    assert ctx_seq == CTX_SEQ and lat_seq == LAT_SEQ and (n_ctx * ctx_seq) % TM == 0
    assert 1 + n_lat <= N_COND
    ctx_rows = n_ctx * ctx_seq
    lat_rows = n_lat * lat_seq

    xc = x_prompt.reshape(ctx_rows, D_MODEL)
    xl = x_sample.reshape(lat_rows, D_MODEL)
    cond = jnp.concatenate([c_ctx[None, :], c, jnp.zeros((N_COND - 1 - n_lat, D_MODEL), F32)], axis=0)
    mod_all = _mod_call(cond, w_ada, b_ada).reshape(depth, N_COND, 6, 1, D_MODEL)

    w_in_t = jnp.swapaxes(w_in, 1, 2)
    w_lo_bf = jnp.pad(w_in[:, :, N_CONV:].astype(BF16), ((0, 0), (0, 0), (0, TN_IN - (D_IN - N_CONV))))
    conv_w9 = conv_w.reshape(depth, 9, N_CONV)
    w_out_bf = w_out.astype(BF16)
    row = lambda a: a.reshape(depth, 1, a.shape[-1])
    lp = {
        "w_up": lora_w_up, "w0": lora_w0[:, :, None, :], "a_up": lora_a_up, "a0": lora_a0[:, :, None, :],
        "k_k": row(rwkv_k_k), "k_a": row(rwkv_k_a), "r_k": rwkv_r_k.reshape(depth, 1, D_RWKV),
        "ln_g": row(rwkv_ln_g), "ln_b": row(rwkv_ln_b),
        "g_up": jnp.pad(lora_g_up, ((0, 0), (0, LORA_G_PAD - LORA_G), (0, 0))),
    }
    fp = {
        "w1": jnp.pad(filt_w1, ((0, 0), (0, FILT_EMB_PAD - FILT_EMB), (0, 0))), "b1": row(filt_b1),
        "w2": filt_w2, "b2": row(filt_b2), "w3": filt_w3, "freq": filt_freq,
    }
    ln1_g, ln1_b, ln2_g, ln2_b = row(ln1_g), row(ln1_b), row(ln2_g), row(ln2_b)
    state_in = state_rwkv.astype(F32)

    def trunk_layer(x, l, is_grid, n_seq, seq_len, s0, hyena_tc, hyena_sub):
        proj = _inproj_call(x, mod_all, w_in_t, w_lo_bf, conv_w9, l, is_grid)
        scan = _wkv_call(proj, lp, l, s0, n_seq, seq_len, s0 is None)
        a_out = _rwkvpost_call(scan[0], proj, lp, l)
        spec = _hyfilt_call(seq_len, fp, l)
        b_out = _hyena_call(proj, spec, hyena_bias, l, n_seq, seq_len, hyena_tc, hyena_sub)
        x = _outproj_call(a_out, b_out, x, mod_all, w_out_bf, ln1_g, ln1_b, l, is_grid)
        x = _mlp_call(x, mod_all, mlp_w1, mlp_w2, ln2_g, ln2_b, l, is_grid)
        return x, (scan[1] if s0 is None else None)

    ctx_states = []
    lat_sub = 2 if n_lat % 2 == 0 else 1
    for l in range(depth):
        xc, s_ctx = trunk_layer(xc, l, False, n_ctx, ctx_seq, None, D_HYENA, 1)
        ctx_states.append(s_ctx)
        xl, _ = trunk_layer(xl, l, True, n_lat, lat_seq, state_in, 256, lat_sub)

    y_prompt = xc.reshape(n_ctx, ctx_seq, D_MODEL)
    y_sample = xl.reshape(n_lat, lat_seq, D_MODEL)
    new_state = jnp.stack(ctx_states, axis=1).astype(x_prompt.dtype)
    return (y_prompt, y_sample, new_state)
```

```python
import functools
import math

import jax
import jax.numpy as jnp
import numpy as np
from jax import lax
from jax.experimental import pallas as pl
from jax.experimental.pallas import tpu as pltpu

F32 = jnp.float32
BF16 = jnp.bfloat16

D_MODEL = 2048
D_RWKV = 1024
D_HYENA = 1024
HEAD_DIM = 64
N_HEADS = D_RWKV // HEAD_DIM
N_PAIRS = N_HEADS // 2
PAIR_W = 2 * HEAD_DIM
LORA_W = 64
LORA_A = 64
LORA_G = 160
LORA_G_PAD = 256
N_CONV = 3 * D_RWKV + 3 * D_HYENA
D_IN = N_CONV + LORA_W + LORA_A + LORA_G
D_FF = 4 * D_MODEL
GRID_W = 64
CTX_SEQ = 256
LAT_SEQ = 1024
FILT_BANDS = 16
FILT_EMB = 1 + 2 * FILT_BANDS
FILT_EMB_PAD = 128
FILT_HIDDEN = 64
DEPTH = 2
ALPHA = (2 * DEPTH) ** 0.25
LN_EPS = 1e-5
GN_EPS = 64e-5
FILT_TARGET = 1e-2
FAST_DECAY_PCT = 0.3
SLOW_DECAY_PCT = 1.5
DECAY_SCALE = math.exp(-0.5)

TM = 1024
TN_IN = 512
NJ_CONV = N_CONV // TN_IN
D_IN_PAD = (NJ_CONV + 1) * TN_IN
LO_BLOCK = N_CONV // TN_IN
CONV_PAD = 72
CHUNK = 64
WKV_GROUP = 4
WKV_ROWS = WKV_GROUP * CHUNK
WKV_PAR = 2
N_COND = 8
TN_MOD = 1024
TM_OUT = 512
TM_MLP = 1024
TF_MLP = 512
TM_POST = 256
TC_FILT = 512
VMEM_LIMIT = 56 * 1024 * 1024


def _cparams(sem):
    return pltpu.CompilerParams(dimension_semantics=sem, vmem_limit_bytes=VMEM_LIMIT)


def _dot(a, b):
    return jnp.dot(a.astype(BF16), b.astype(BF16), preferred_element_type=F32)


def _dot_nt(a, b):
    return lax.dot_general(a.astype(BF16), b.astype(BF16), (((1,), (1,)), ((), ())),
                           preferred_element_type=F32)


def _dot_tn(a, b):
    return lax.dot_general(a.astype(BF16), b.astype(BF16), (((0,), (0,)), ((), ())),
                           preferred_element_type=F32)


def _split(x):
    hi = x.astype(BF16)
    lo = (x - hi.astype(F32)).astype(BF16)
    return hi, lo


def _dot3(a, b):
    ah, al = _split(a)
    bh, bl = _split(b)
    return _dot(ah, bh) + (_dot(ah, bl) + _dot(al, bh))


def _dot2(a, b_exact):
    ah, al = _split(a)
    return _dot(ah, b_exact) + _dot(al, b_exact)


def _dot2_left(a_exact, b):
    bh, bl = _split(b)
    return _dot(a_exact, bh) + _dot(a_exact, bl)


def _sigmoid(x):
    return 1.0 / (1.0 + jnp.exp(-x))


def _layer_norm_rows(x):
    mu = jnp.mean(x, axis=-1, keepdims=True)
    xc = x - mu
    var = jnp.mean(xc * xc, axis=-1, keepdims=True)
    return xc * lax.rsqrt(var + LN_EPS)


def _cond_row(i, tm, is_grid):
    return 1 + (i * tm) // LAT_SEQ if is_grid else 0


def _mod_spec(l, which, grp):
    return pl.BlockSpec((None, 1, 1, 1, D_MODEL), lambda i, *_: (l, grp(i), which, 0, 0))


def _mod_kernel(c_ref, w_ref, b_ref, o_ref):
    c = c_ref[...]
    s = c * _sigmoid(c)
    o_ref[0] = _dot(s, w_ref[0]) + b_ref[0]


def _mod_call(cond, w_ada, b_ada):
    depth = w_ada.shape[0]
    n_out = w_ada.shape[2]
    return pl.pallas_call(
        _mod_kernel,
        out_shape=jax.ShapeDtypeStruct((depth, N_COND, n_out), F32),
        grid=(depth, n_out // TN_MOD),
        in_specs=[
            pl.BlockSpec((N_COND, D_MODEL), lambda l, j: (0, 0)),
            pl.BlockSpec((1, D_MODEL, TN_MOD), lambda l, j: (l, 0, j)),
            pl.BlockSpec((1, 1, TN_MOD), lambda l, j: (l, 0, j)),
        ],
        out_specs=pl.BlockSpec((1, N_COND, TN_MOD), lambda l, j: (l, 0, j)),
        compiler_params=_cparams(("parallel", "parallel")),
        name="mod",
    )(cond, w_ada, b_ada.reshape(depth, 1, n_out))


def _inproj_kernel(x_ref, sh_ref, sc_ref, w_ref, wlo_ref, cw_ref, o_ref, h_ref, wbf_ref, acc_a, acc_b, *,
                   is_grid):
    j = pl.program_id(1)
    tn = o_ref.shape[1]
    mid = slice(CONV_PAD, CONV_PAD + TM)

    n_blocks = TM // GRID_W
    n_parts = 4
    lane_w = 128
    n_chunk = 128

    def round_weights():
        for nc in range(tn // n_chunk):
            rows = slice(nc * n_chunk, (nc + 1) * n_chunk)
            wbf_ref[rows, :] = w_ref[rows, :].astype(BF16)

    def matmul_into(dst_ref, part=None):
        if part is None:
            dst_ref[mid, :] = _dot_nt(h_ref[...], wbf_ref[...])
        else:
            r0, r1 = part * (TM // n_parts), (part + 1) * (TM // n_parts)
            dst_ref[CONV_PAD + r0:CONV_PAD + r1, :] = _dot_nt(h_ref[r0:r1, :], wbf_ref[...])

    def conv_from(src_ref, g0=0, g1=n_blocks):
        row = lax.broadcasted_iota(jnp.int32, (GRID_W, lane_w), 0)
        first = row == 0
        last = row == GRID_W - 1
        per_seq = CTX_SEQ // GRID_W

        def shifted(s, lanes, mask_first, mask_last):
            start = CONV_PAD + s * GRID_W
            uc = src_ref[start:start + GRID_W, lanes]
            ul = src_ref[start - 1:start - 1 + GRID_W, lanes]
            ur = src_ref[start + 1:start + 1 + GRID_W, lanes]
            if mask_first:
                ul = jnp.where(first, 0.0, ul)
            if mask_last:
                ur = jnp.where(last, 0.0, ur)
            return ul, uc, ur

        for c in range(tn // lane_w):
            lanes = slice(c * lane_w, (c + 1) * lane_w)
            cw = cw_ref[:, lanes]
            if not is_grid:
                for s in range(g0, g1):
                    ul, uc, ur = shifted(s, lanes, s % per_seq == 0, s % per_seq == per_seq - 1)
                    y = ul * cw[3:4, :] + uc * cw[4:5, :] + ur * cw[5:6, :]
                    o_ref[s * GRID_W:(s + 1) * GRID_W, lanes] = y.astype(o_ref.dtype)
                continue
            partial = {}
            for s in range(max(g0 - 1, 0), min(g1 + 1, n_blocks)):
                ul, uc, ur = shifted(s, lanes, True, True)
                for a in range(3):
                    gi = s - (a - 1)
                    if g0 <= gi < g1:
                        t = (ul * cw[3 * a:3 * a + 1, :] + uc * cw[3 * a + 1:3 * a + 2, :]
                             + ur * cw[3 * a + 2:3 * a + 3, :])
                        partial[gi] = t if gi not in partial else partial[gi] + t
                if s - 1 in partial:
                    o_ref[(s - 1) * GRID_W:s * GRID_W, lanes] = partial.pop(s - 1).astype(o_ref.dtype)
            for gi in sorted(partial):
                o_ref[gi * GRID_W:(gi + 1) * GRID_W, lanes] = partial.pop(gi).astype(o_ref.dtype)

    @pl.when(j == 0)
    def _():
        sc = 1.0 + sc_ref[0, 0]
        sh = sh_ref[0, 0]

        for acc in (acc_a, acc_b):
            acc[0:CONV_PAD, :] = jnp.zeros((CONV_PAD, tn), F32)
            acc[CONV_PAD + TM:, :] = jnp.zeros((CONV_PAD, tn), F32)
        part_rows = TM // n_parts
        for part in range(n_parts):
            for rb in range(part_rows // 128):
                rows = slice(part * part_rows + rb * 128, part * part_rows + (rb + 1) * 128)
                h = _layer_norm_rows(x_ref[rows, :]) * sc + sh
                h_ref[rows, :] = h.astype(BF16)
            rows = slice(part * part_rows, (part + 1) * part_rows)
            o_ref[rows, :] = jnp.dot(h_ref[rows, :], wlo_ref[...], preferred_element_type=F32).astype(o_ref.dtype)

    @pl.when(j == 1)
    def _():
        round_weights()
        matmul_into(acc_a)

    main = jnp.logical_and(j >= 2, j <= NJ_CONV)
    odd = jnp.bitwise_and(j, 1) == 1

    def overlapped(src_ref, dst_ref):
        round_weights()
        per_part = n_blocks // n_parts
        for part in range(n_parts):
            matmul_into(dst_ref, part)
            conv_from(src_ref, part * per_part, (part + 1) * per_part)

    @pl.when(jnp.logical_and(main, jnp.logical_not(odd)))
    def _():
        overlapped(acc_a, acc_b)

    @pl.when(jnp.logical_and(main, odd))
    def _():
        overlapped(acc_b, acc_a)

    @pl.when(j == NJ_CONV + 1)
    def _():
        conv_from(acc_b if NJ_CONV % 2 == 0 else acc_a)


def _inproj_call(x, mod_all, w_in, w_lo, conv_w, l, is_grid):
    n_units = x.shape[0] // TM
    kern = functools.partial(_inproj_kernel, is_grid=is_grid)
    grp = lambda i: _cond_row(i, TM, is_grid)
    tile = lambda t: jnp.clip(t, 0, NJ_CONV - 1)
    return pl.pallas_call(
        kern,
        out_shape=jax.ShapeDtypeStruct((x.shape[0], D_IN_PAD), BF16),
        grid=(n_units, NJ_CONV + 2),
        in_specs=[
            pl.BlockSpec((TM, D_MODEL), lambda i, j: (i, 0)),
            _mod_spec(l, 0, grp),
            _mod_spec(l, 1, grp),
            pl.BlockSpec((None, TN_IN, D_MODEL), lambda i, j: (l, tile(j - 1), 0)),
            pl.BlockSpec((None, D_MODEL, TN_IN), lambda i, j: (l, 0, 0)),
            pl.BlockSpec((None, 9, TN_IN), lambda i, j: (l, 0, tile(j - 2))),
        ],
        out_specs=pl.BlockSpec((TM, TN_IN), lambda i, j: (i, jnp.where(j == 0, LO_BLOCK, tile(j - 2)))),
        scratch_shapes=[
            pltpu.VMEM((TM, D_MODEL), BF16),
            pltpu.VMEM((TN_IN, D_MODEL), BF16),
            pltpu.VMEM((TM + 2 * CONV_PAD, TN_IN), F32),
            pltpu.VMEM((TM + 2 * CONV_PAD, TN_IN), F32),
        ],
        compiler_params=_cparams(("parallel", "arbitrary")),
        name="inproj",
    )(x, mod_all, mod_all, w_in, w_lo, conv_w)


def _stack_heads(z, head0):
    return jnp.concatenate([jnp.where(head0, z, 0.0), jnp.where(head0, 0.0, z)], axis=0)


def _unstack_heads(z):
    c = z.shape[0] // 2
    return z[:c] + z[c:]


def _wkv_kernel(*refs, has_init, has_state_out):
    (r_ref, k_ref, v_ref, lo_ref, wup_ref, w0_ref, aup_ref, a0_ref, kk_ref, ka_ref) = refs[:10]
    pos = 10
    s0_ref = None
    if has_init:
        s0_ref = refs[pos]
        pos += 1
    o_ref = refs[pos]
    pos += 1
    sout_ref = None
    if has_state_out:
        sout_ref = refs[pos]
        pos += 1
    (s_ref, lw_ref, cum_ref, kkn_ref, b_ref, kd_ref, msk_ref,
     w1_ref, w2_ref, yhat_ref, kv_ref, mv_ref, dec_ref) = refs[pos:]

    d = pl.program_id(0)
    g = pl.program_id(2)
    C = CHUNK
    pairs = range(N_PAIRS)
    cols = [slice(p * PAIR_W, (p + 1) * PAIR_W) for p in pairs]

    @pl.when(g == 0)
    def _():
        if has_init:
            zero = jnp.zeros((HEAD_DIM, HEAD_DIM), F32)
            for p in pairs:
                top = jnp.concatenate([s0_ref[2 * p], zero], axis=1)
                bottom = jnp.concatenate([zero, s0_ref[2 * p + 1]], axis=1)
                s_ref[p] = jnp.concatenate([top, bottom], axis=0)
        else:
            s_ref[...] = jnp.zeros(s_ref.shape, F32)

    sgn = 1 - 2 * d
    si = lax.broadcasted_iota(jnp.int32, (2 * C, 2 * C), 0)
    sj = lax.broadcasted_iota(jnp.int32, (2 * C, 2 * C), 1)
    same_head = (si >= C) == (sj >= C)
    dlt = (jnp.bitwise_and(si, C - 1) - jnp.bitwise_and(sj, C - 1)) * sgn
    msk_ref[0] = jnp.where(jnp.logical_and(same_head, dlt < 0), 1.0, 0.0)
    msk_ref[1] = jnp.where(jnp.logical_and(same_head, dlt >= 0), 1.0, 0.0)
    msk_ref[2] = jnp.where(si == sj, 1.0, 0.0)
    blockdiag = (si >= HEAD_DIM) == (sj >= HEAD_DIM)
    msk_ref[3] = jnp.where(blockdiag, 1.0, 0.0)
    head_sum = jnp.where(blockdiag, 1.0, 0.0).astype(BF16)

    lo = lo_ref[...].astype(F32)
    xw = lo[:, 0:LORA_W]
    xa = lo[:, LORA_W:LORA_W + LORA_A]
    logw = -DECAY_SCALE * _sigmoid(w0_ref[...] + _dot(jnp.tanh(xw), wup_ref[...]))
    lw_ref[...] = logw
    gi = lax.broadcasted_iota(jnp.int32, (WKV_ROWS, WKV_ROWS), 0)
    gj = lax.broadcasted_iota(jnp.int32, (WKV_ROWS, WKV_ROWS), 1)
    same_chunk = jnp.bitwise_and(gi, -C) == jnp.bitwise_and(gj, -C)
    cum_mask = jnp.where(jnp.logical_and(same_chunk, (gi - gj) * sgn >= 0), 1.0, 0.0).astype(BF16)
    cum_ref[...] = _dot2_left(cum_mask, logw)
    iclr = _sigmoid(a0_ref[...] + _dot(xa, aup_ref[...]))
    k = k_ref[...].astype(F32)
    kd_ref[...] = k * (1.0 + (iclr - 1.0) * ka_ref[...])
    kkr = k * kk_ref[...]
    for p in pairs:
        slab = kkr[:, cols[p]]
        kkn = slab * lax.rsqrt(_dot(slab * slab, head_sum) + 1e-12)
        kkn_ref[:, cols[p]] = kkn
        b_ref[:, cols[p]] = kkn * iclr[:, cols[p]]

    def chunk_rows(i):
        cc = jnp.where(d == 0, i, WKV_GROUP - 1 - i)
        return pl.ds(pl.multiple_of(cc * C, C), C)

    def phase_a(it, carry):
        chunk_ids = [it * WKV_PAR + ci for ci in range(WKV_PAR)]
        units = [(ci, c) for ci in chunk_ids for c in cols]
        chunk_of = [ci for ci in chunk_ids for _ in cols]
        pair_of = [p for _ in chunk_ids for p in pairs]
        head0 = lax.broadcasted_iota(jnp.int32, (C, PAIR_W), 1) < HEAD_DIM
        strict_t = msk_ref[0]
        incl = msk_ref[1]
        eye = msk_ref[2]
        bd_f = msk_ref[3]

        lw = [lw_ref[chunk_rows(ci), c] for ci, c in units]
        cum = [cum_ref[chunk_rows(ci), c] for ci, c in units]
        kkn = [kkn_ref[chunk_rows(ci), c] for ci, c in units]
        bb = [b_ref[chunk_rows(ci), c] for ci, c in units]
        kd = [kd_ref[chunk_rows(ci), c] for ci, c in units]
        rr = [r_ref[chunk_rows(ci), c].astype(F32) for ci, c in units]
        vv = [v_ref[chunk_rows(ci), c].astype(F32) for ci, c in units]
        cum_prev = [cm - x for cm, x in zip(cum, lw)]
        mid = [cm[C // 2:C // 2 + 1, :] for cm in cum]
        total = [jnp.where(d == 0, cm[C - 1:C, :], cm[0:1, :]) for cm in cum]
        e_in = [jnp.exp(cm - m) for cm, m in zip(cum, mid)]
        e_in_prev = [jnp.exp(cm - m) for cm, m in zip(cum_prev, mid)]
        e_out = [jnp.exp(m - cm) for cm, m in zip(cum, mid)]
        e_tail = [jnp.exp(t - cm) for cm, t in zip(cum, total)]
        a1s = [_stack_heads(x * e, head0) for x, e in zip(kkn, e_in_prev)]
        r1s = [_stack_heads(x * e, head0) for x, e in zip(rr, e_in)]
        b1 = [x * e for x, e in zip(bb, e_out)]
        k1 = [x * e for x, e in zip(kd, e_out)]
        a0s = [_stack_heads(x * jnp.exp(cm), head0) for x, cm in zip(kkn, cum_prev)]
        vs = [_stack_heads(x, head0) for x in vv]
        g_t = [_dot_nt(jnp.concatenate([b, k], axis=0), jnp.concatenate([a, r_], axis=0))
               for a, r_, b, k in zip(a1s, r1s, b1, k1)]
        strict_s = strict_t[:C] + strict_t[C:]
        eye_s = eye[:C] + eye[C:]
        incl_s = strict_s + eye_s
        stack = lambda z: _stack_heads(z, head0)
        l_ab_s = [g_[:C, :PAIR_W] * strict_s for g_ in g_t]
        l_ak_t = [stack(g_[C:, :PAIR_W] * strict_s) for g_ in g_t]
        m_rb_t = [stack(g_[:C, PAIR_W:] * incl_s) for g_ in g_t]
        m_rk = [_dot_nt(a, jnp.concatenate([k, k], axis=0)) * incl for a, k in zip(r1s, k1)]
        lakv_t = [_dot_tn(a, b) for a, b in zip(vs, l_ak_t)]
        mv = [_unstack_heads(_dot(a, b)) for a, b in zip(m_rk, vs)]
        kv = [_dot_tn(a, b * e) * bd_f for a, b, e in zip(vv, kd, e_tail)]
        x = [eye_s - l_ for l_ in l_ab_s]
        q = [_dot(l_, stack(l_)) for l_ in l_ab_s]
        step = 2
        while 2 * step < C:
            prod = [_dot(jnp.concatenate([xx, qq], axis=0), stack(qq)) for xx, qq in zip(x, q)]
            x = [xx + pr[:C] for xx, pr in zip(x, prod)]
            q = [pr[C:] for pr in prod]
            step *= 2
        x = [stack(xx + _dot(xx, stack(qq))) for xx, qq in zip(x, q)]
        a_hat_t = [_dot_tn(a, xx) for xx, a in zip(x, a0s)]
        y_hat_t = [_dot(a, xx) for xx, a in zip(x, lakv_t)]
        zero_rows = jnp.zeros((C, PAIR_W), F32)
        for u, (i, p) in enumerate(zip(chunk_of, pair_of)):
            r0_t = jnp.concatenate([rr[u] * jnp.exp(cum[u]), zero_rows], axis=0).T
            w1_ref[i, p] = jnp.concatenate([a_hat_t[u], r0_t], axis=1).astype(BF16)
            w2_ref[i, p] = jnp.concatenate([stack(bb[u] * e_tail[u]), m_rb_t[u]], axis=1).astype(BF16)
            yhat_ref[i, p] = y_hat_t[u]
            kv_ref[i, p] = kv[u]
            mv_ref[i, p] = mv[u]
            dec_ref[i, p] = jnp.broadcast_to(jnp.exp(total[u]), (8, PAIR_W))
        return carry

    def phase_b(i, s_old):
        rows = chunk_rows(i)
        s_bf = [s.astype(BF16) for s in s_old]
        sw = [_dot(s_bf[p], w1_ref[i, p]) for p in pairs]
        yt = [(sw[p][:, :PAIR_W] + yhat_ref[i, p]).astype(BF16) for p in pairs]
        yw = [_dot(yt[p], w2_ref[i, p]) for p in pairs]
        s_new = [dec_ref[i, p, 0:1, :] * s_old[p] + kv_ref[i, p] - yw[p][:, :PAIR_W] for p in pairs]
        o = [sw[p][:, PAIR_W:].T[:C] + mv_ref[i, p] - _unstack_heads(yw[p][:, PAIR_W:].T) for p in pairs]
        for p in pairs:
            o_ref[0, rows, cols[p]] = o[p]
        return s_new

    lax.fori_loop(0, WKV_GROUP // WKV_PAR, phase_a, 0)
    state = [s_ref[p] for p in pairs]
    for i in range(WKV_GROUP):
        state = phase_b(i, state)
    for p in pairs:
        s_ref[p] = state[p]

    if has_state_out:
        @pl.when(g == pl.num_programs(2) - 1)
        def _():
            for p in pairs:
                s = s_ref[p]
                sout_ref[2 * p] = s[:HEAD_DIM, :HEAD_DIM]
                sout_ref[2 * p + 1] = s[HEAD_DIM:, HEAD_DIM:]


def _wkv_call(proj, lp, l, s0, n_seq, seq_len, has_state_out):
    n_groups = seq_len // WKV_ROWS
    has_init = s0 is not None

    def group(d, s, g):
        return s * n_groups + g + d * (n_groups - 1 - 2 * g)

    def col(cb):
        return lambda d, s, g: (group(d, s, g), cb)

    per_dir_mat = pl.BlockSpec((None, None, LORA_W, D_RWKV), lambda d, s, g: (l, d, 0, 0))
    per_dir_vec = pl.BlockSpec((None, None, 1, D_RWKV), lambda d, s, g: (l, d, 0, 0))
    shared = pl.BlockSpec((None, 1, D_RWKV), lambda d, s, g: (l, 0, 0))
    in_specs = [
        pl.BlockSpec((WKV_ROWS, D_RWKV), col(0)),
        pl.BlockSpec((WKV_ROWS, D_RWKV), col(1)),
        pl.BlockSpec((WKV_ROWS, D_RWKV), col(2)),
        pl.BlockSpec((WKV_ROWS, TN_IN), col(LO_BLOCK)),
        per_dir_mat, per_dir_vec, per_dir_mat, per_dir_vec, shared, shared,
    ]
    args = [proj, proj, proj, proj, lp["w_up"], lp["w0"], lp["a_up"], lp["a0"], lp["k_k"], lp["k_a"]]
    if has_init:
        in_specs.append(pl.BlockSpec((None, None, None, N_HEADS, HEAD_DIM, HEAD_DIM),
                                     lambda d, s, g: (s, l, d, 0, 0, 0)))
        args.append(s0)
    out_shape = [jax.ShapeDtypeStruct((2, n_seq * seq_len, D_RWKV), F32)]
    out_specs = [pl.BlockSpec((1, WKV_ROWS, D_RWKV), lambda d, s, g: (d, group(d, s, g), 0))]
    if has_state_out:
        out_shape.append(jax.ShapeDtypeStruct((n_seq, 2, N_HEADS, HEAD_DIM, HEAD_DIM), F32))
        out_specs.append(pl.BlockSpec((None, None, N_HEADS, HEAD_DIM, HEAD_DIM), lambda d, s, g: (s, d, 0, 0, 0)))
    kern = functools.partial(_wkv_kernel, has_init=has_init, has_state_out=has_state_out)
    per_chunk = (WKV_GROUP, N_PAIRS)
    return pl.pallas_call(
        kern,
        out_shape=out_shape,
        grid=(2, n_seq, n_groups),
        in_specs=in_specs,
        out_specs=out_specs,
        scratch_shapes=[
            pltpu.VMEM((N_PAIRS, PAIR_W, PAIR_W), F32),
            pltpu.VMEM((WKV_ROWS, D_RWKV), F32),
            pltpu.VMEM((WKV_ROWS, D_RWKV), F32),
            pltpu.VMEM((WKV_ROWS, D_RWKV), F32),
            pltpu.VMEM((WKV_ROWS, D_RWKV), F32),
            pltpu.VMEM((WKV_ROWS, D_RWKV), F32),
            pltpu.VMEM((4, PAIR_W, PAIR_W), F32),
            pltpu.VMEM(per_chunk + (PAIR_W, 2 * PAIR_W), BF16),
            pltpu.VMEM(per_chunk + (PAIR_W, 2 * PAIR_W), BF16),
            pltpu.VMEM(per_chunk + (PAIR_W, PAIR_W), F32),
            pltpu.VMEM(per_chunk + (PAIR_W, PAIR_W), F32),
            pltpu.VMEM(per_chunk + (CHUNK, PAIR_W), F32),
            pltpu.VMEM(per_chunk + (8, PAIR_W), F32),
        ],
        compiler_params=_cparams(("parallel", "parallel", "arbitrary")),
        name="wkv",
    )(*args)


def _rwkvpost_kernel(o_ref, r_ref, k_ref, v_ref, lo_ref, aup_ref, a0_ref, ka_ref, rk_ref, lng_ref, lnb_ref,
                     gup_ref, out_ref):
    lo = lo_ref[...].astype(F32)
    xa = lo[:, LORA_W:LORA_W + LORA_A]
    sg = _sigmoid(lo[:, LORA_W + LORA_A:LORA_W + LORA_A + LORA_G_PAD])
    gi = lax.broadcasted_iota(jnp.int32, (PAIR_W, PAIR_W), 0)
    gj = lax.broadcasted_iota(jnp.int32, (PAIR_W, PAIR_W), 1)
    head_sum = jnp.where((gi >= HEAD_DIM) == (gj >= HEAD_DIM), 1.0, 0.0).astype(BF16)
    cols = [slice(p * PAIR_W, (p + 1) * PAIR_W) for p in range(N_PAIRS)]
    gate = _dot(sg, gup_ref[...])
    ic = _sigmoid(a0_ref[0] + _dot(xa, aup_ref[0])) + _sigmoid(a0_ref[1] + _dot(xa, aup_ref[1]))
    kd_sum = k_ref[...].astype(F32) * (2.0 + (ic - 2.0) * ka_ref[...])
    bterm = r_ref[...].astype(F32) * kd_sum * rk_ref[...]
    o = [o_ref[0, :, c] + o_ref[1, :, c] for c in cols]
    mu = [_dot2(x, head_sum) * (1.0 / HEAD_DIM) for x in o]
    oc = [x - m for x, m in zip(o, mu)]
    var = [_dot(x * x, head_sum) * (1.0 / HEAD_DIM) for x in oc]
    bonus = [_dot(bterm[:, c], head_sum) * v_ref[:, c].astype(F32) for c in cols]
    for c, x, vr, bn in zip(cols, oc, var, bonus):
        y = x * lax.rsqrt(vr + GN_EPS) * lng_ref[:, c] + lnb_ref[:, c]
        out_ref[:, c] = ((y + bn) * gate[:, c]).astype(BF16)


def _rwkvpost_call(o, proj, lp, l):
    n_rows = o.shape[1]

    def col(cb):
        return lambda i: (i, cb)

    shared = pl.BlockSpec((None, 1, D_RWKV), lambda i: (l, 0, 0))
    return pl.pallas_call(
        _rwkvpost_kernel,
        out_shape=jax.ShapeDtypeStruct((n_rows, D_RWKV), BF16),
        grid=(n_rows // TM_POST,),
        in_specs=[
            pl.BlockSpec((2, TM_POST, D_RWKV), lambda i: (0, i, 0)),
            pl.BlockSpec((TM_POST, D_RWKV), col(0)),
            pl.BlockSpec((TM_POST, D_RWKV), col(1)),
            pl.BlockSpec((TM_POST, D_RWKV), col(2)),
            pl.BlockSpec((TM_POST, TN_IN), col(LO_BLOCK)),
            pl.BlockSpec((None, 2, LORA_A, D_RWKV), lambda i: (l, 0, 0, 0)),
            pl.BlockSpec((None, 2, 1, D_RWKV), lambda i: (l, 0, 0, 0)),
            shared, shared, shared, shared,
            pl.BlockSpec((None, LORA_G_PAD, D_RWKV), lambda i: (l, 0, 0)),
        ],
        out_specs=pl.BlockSpec((TM_POST, D_RWKV), lambda i: (i, 0)),
        compiler_params=_cparams(("parallel",)),
        name="rwkvpost",
    )(o, proj, proj, proj, proj, lp["a_up"], lp["a0"], lp["k_a"], lp["r_k"], lp["ln_g"], lp["ln_b"], lp["g_up"])


def _dft_mats(L):
    idx = (np.arange(L)[:, None] * np.arange(L)[None, :]) % (2 * L)
    ang = np.pi * idx.astype(np.float64) / L
    alt = np.cos(np.pi * np.arange(L))
    fc = np.cos(ang)
    fs = -np.sin(ang)
    fs[0, :] = alt
    fwd = np.concatenate([fc, fs], axis=0)
    ic = 2.0 * np.cos(ang.T)
    ic[:, 0] = 1.0
    isn = -2.0 * np.sin(ang.T)
    isn[:, 0] = alt
    inv = np.concatenate([ic, isn], axis=1) / (2 * L)
    return fwd.astype(np.float32), inv.astype(np.float32)


def _filter_features(L):
    t = np.linspace(0.0, 1.0, L, dtype=np.float32)[:, None]
    w = (2.0 * math.pi / L) * np.arange(L, dtype=np.float32)[:, None]
    f = np.linspace(1e-4, FILT_BANDS - 1, FILT_BANDS, dtype=np.float32)[None, :]
    z = np.concatenate([t, np.cos(f * w), -np.sin(f * w)], axis=-1)
    zp = np.zeros((L, FILT_EMB_PAD), np.float32)
    zp[:, :FILT_EMB] = z
    return zp, t


def _filter_deltas():
    max_decay = math.log(FILT_TARGET) / FAST_DECAY_PCT
    min_decay = math.log(FILT_TARGET) / SLOW_DECAY_PCT
    return np.abs(np.linspace(min_decay, max_decay, D_HYENA, dtype=np.float32))[None, :]


def _hyfilt_kernel(z_ref, t_ref, dl_ref, fwd_ref, w1_ref, b1_ref, w2_ref, b2_ref,
                   w3f0_ref, w3b0_ref, w3f1_ref, w3b1_ref, fr_ref, o_ref):
    L = z_ref.shape[0]
    tc = o_ref.shape[3]
    h = jnp.sin(fr_ref[0:1, :] * (_dot3(z_ref[...], w1_ref[...]) + b1_ref[...]))
    h = jnp.sin(fr_ref[1:2, :] * (_dot3(h, w2_ref[...]) + b2_ref[...]))
    decay = jnp.exp(-t_ref[...] * dl_ref[...])
    first = lax.broadcasted_iota(jnp.int32, (L, tc), 0) == 0
    fwd = fwd_ref[...]
    for order, (wf_ref, wb_ref) in enumerate(((w3f0_ref, w3b0_ref), (w3f1_ref, w3b1_ref))):
        hf = _dot3(h, wf_ref[...]) * decay
        hb = _dot3(h, wb_ref[...]) * decay
        norm = jnp.sum(jnp.abs(hf) + jnp.abs(hb), axis=0, keepdims=True)
        hf = hf / norm
        hb = jnp.where(first, 0.0, hb / norm)
        ks = _dot(fwd, hf + hb)
        kd = _dot(fwd, hf - hb)
        kr = ks[:L]
        o_ref[order, 0] = kr
        o_ref[order, 1] = jnp.where(first, 0.0, kd[L:])
        o_ref[order, 2] = jnp.where(first, ks[L:L + 1], kr)


def _hyfilt_call(L, fp, l):
    z, t = _filter_features(L)
    fwd, _ = _dft_mats(L)
    nct = D_HYENA // TC_FILT
    full = lambda shape: pl.BlockSpec(shape, lambda j: tuple(0 for _ in shape))
    layer = lambda shape: pl.BlockSpec((None,) + shape, lambda j: (l,) + tuple(0 for _ in shape))
    w3_spec = lambda grp: pl.BlockSpec((None, FILT_HIDDEN, TC_FILT), lambda j: (l, 0, grp * nct + j))
    w3 = fp["w3"]
    return pl.pallas_call(
        _hyfilt_kernel,
        out_shape=jax.ShapeDtypeStruct((2, 3, L, D_HYENA), F32),
        grid=(nct,),
        in_specs=[
            full((L, FILT_EMB_PAD)), full((L, 1)), pl.BlockSpec((1, TC_FILT), lambda j: (0, j)), full((2 * L, L)),
            layer((FILT_EMB_PAD, FILT_HIDDEN)), layer((1, FILT_HIDDEN)),
            layer((FILT_HIDDEN, FILT_HIDDEN)), layer((1, FILT_HIDDEN)),
            w3_spec(0), w3_spec(1), w3_spec(2), w3_spec(3),
            layer((2, FILT_HIDDEN)),
        ],
        out_specs=pl.BlockSpec((2, 3, L, TC_FILT), lambda j: (0, 0, 0, j)),
        compiler_params=_cparams(("parallel",)),
        name="hyfilt",
    )(jnp.asarray(z), jnp.asarray(t), jnp.asarray(_filter_deltas()), jnp.asarray(fwd).astype(BF16),
      fp["w1"], fp["b1"], fp["w2"], fp["b2"], w3, w3, w3, w3, fp["freq"])


def _hyena_kernel(u_ref, x1_ref, x2_ref, fwd_ref, inv_ref, spec_ref, bias_ref, out_ref):
    L = fwd_ref.shape[1]
    n_sub = u_ref.shape[0] // L

    def side_by_side(ref):
        return jnp.concatenate([ref[s * L:(s + 1) * L, :].astype(F32) for s in range(n_sub)], axis=1)

    def tiled(x):
        return jnp.concatenate([x] * n_sub, axis=1)

    def long_conv(u, order):
        spec = _dot(fwd_ref[...], u)
        ur, ui = spec[:L], spec[L:]
        kr, ki, kr2 = tiled(spec_ref[order, 0]), tiled(spec_ref[order, 1]), tiled(spec_ref[order, 2])
        yr = ur * kr - ui * ki
        yi = ur * ki + ui * kr2
        y = _dot(inv_ref[...], jnp.concatenate([yr, yi], axis=0))
        return y + u * tiled(bias_ref[order:order + 1, :])

    z = side_by_side(x1_ref) * long_conv(side_by_side(u_ref), 0)
    out = (side_by_side(x2_ref) * long_conv(z, 1)).astype(BF16)
    tc = out_ref.shape[1]
    for s in range(n_sub):
        out_ref[s * L:(s + 1) * L, :] = out[:, s * tc:(s + 1) * tc]


def _hyena_call(proj, spec, bias, l, n_seq, L, tc, n_sub):
    fwd, inv = _dft_mats(L)
    nct = D_HYENA // tc
    rows = n_sub * L
    cb0 = 3 * D_RWKV // tc

    def col(which):
        return lambda s, j: (s, cb0 + which * nct + j)

    in_specs = [
        pl.BlockSpec((rows, tc), col(0)),
        pl.BlockSpec((rows, tc), col(1)),
        pl.BlockSpec((rows, tc), col(2)),
        pl.BlockSpec((2 * L, L), lambda s, j: (0, 0), pipeline_mode=pl.Buffered(1)),
        pl.BlockSpec((L, 2 * L), lambda s, j: (0, 0), pipeline_mode=pl.Buffered(1)),
        pl.BlockSpec((2, 3, L, tc), lambda s, j: (0, 0, 0, j)),
        pl.BlockSpec((None, 2, tc), lambda s, j: (l, 0, j)),
    ]
    return pl.pallas_call(
        _hyena_kernel,
        out_shape=jax.ShapeDtypeStruct((n_seq * L, D_HYENA), BF16),
        grid=(n_seq // n_sub, nct),
        in_specs=in_specs,
        out_specs=pl.BlockSpec((rows, tc), lambda s, j: (s, j)),
        compiler_params=_cparams(("parallel", "parallel")),
        name="hyena",
    )(proj, proj, proj, jnp.asarray(fwd).astype(BF16), jnp.asarray(inv).astype(BF16), spec, bias)


def _outproj_kernel(a_ref, b_ref, x_ref, g_ref, w_ref, lg_ref, lb_ref, o_ref):
    mix = (jnp.dot(a_ref[...], w_ref[0:D_RWKV, :], preferred_element_type=F32)
           + jnp.dot(b_ref[...], w_ref[D_RWKV:, :], preferred_element_type=F32))
    y = ALPHA * x_ref[...] + g_ref[0, 0] * mix
    o_ref[...] = _layer_norm_rows(y) * lg_ref[...] + lb_ref[...]


def _outproj_call(a_out, b_out, x, mod_all, w_out, ln_g, ln_b, l, is_grid):
    grp = lambda i: _cond_row(i, TM_OUT, is_grid)
    vec = pl.BlockSpec((None, 1, D_MODEL), lambda i: (l, 0, 0))
    return pl.pallas_call(
        _outproj_kernel,
        out_shape=jax.ShapeDtypeStruct(x.shape, F32),
        grid=(x.shape[0] // TM_OUT,),
        in_specs=[
            pl.BlockSpec((TM_OUT, D_RWKV), lambda i: (i, 0)),
            pl.BlockSpec((TM_OUT, D_HYENA), lambda i: (i, 0)),
            pl.BlockSpec((TM_OUT, D_MODEL), lambda i: (i, 0)),
            _mod_spec(l, 2, grp),
            pl.BlockSpec((None, D_MODEL, D_MODEL), lambda i: (l, 0, 0)),
            vec, vec,
        ],
        out_specs=pl.BlockSpec((TM_OUT, D_MODEL), lambda i: (i, 0)),
        compiler_params=_cparams(("parallel",)),
        name="outproj",
    )(a_out, b_out, x, mod_all, w_out, ln_g, ln_b)


def _mlp_kernel(x_ref, sh_ref, sc_ref, g_ref, w1_ref, w2_ref, lg_ref, lb_ref, o_ref, h_ref):
    f = pl.program_id(1)
    last = pl.num_programs(1) - 1
    tm = o_ref.shape[0]
    n_parts = 4
    part_rows = tm // n_parts

    def partial_sum(rows, w1b, w2b):
        hid = jnp.maximum(jnp.dot(h_ref[rows, :], w1b, preferred_element_type=F32), 0.0)
        return jnp.dot((hid * hid).astype(BF16), w2b, preferred_element_type=F32)

    @pl.when(f == 0)
    def _():
        sc = 1.0 + sc_ref[0, 0]
        sh = sh_ref[0, 0]
        w1b = w1_ref[...].astype(BF16)
        w2b = w2_ref[...].astype(BF16)
        for part in range(n_parts):
            for rb in range(part_rows // 128):
                rows = slice(part * part_rows + rb * 128, part * part_rows + (rb + 1) * 128)
                h_ref[rows, :] = (_layer_norm_rows(x_ref[rows, :]) * sc + sh).astype(BF16)
            rows = slice(part * part_rows, (part + 1) * part_rows)
            o_ref[rows, :] = partial_sum(rows, w1b, w2b)

    @pl.when(jnp.logical_and(f > 0, f < last))
    def _():
        o_ref[...] += partial_sum(slice(0, tm), w1_ref[...].astype(BF16), w2_ref[...].astype(BF16))

    @pl.when(f == last)
    def _():
        gate = g_ref[0, 0]
        lg = lg_ref[...]
        lb = lb_ref[...]
        w1b = w1_ref[...].astype(BF16)
        w2b = w2_ref[...].astype(BF16)
        for part in range(n_parts):
            rows = slice(part * part_rows, (part + 1) * part_rows)
            o_ref[rows, :] += partial_sum(rows, w1b, w2b)
            for rb in range(part_rows // 128):
                rows = slice(part * part_rows + rb * 128, part * part_rows + (rb + 1) * 128)
                y = ALPHA * x_ref[rows, :] + gate * o_ref[rows, :]
                o_ref[rows, :] = _layer_norm_rows(y) * lg + lb


def _mlp_call(x, mod_all, w1, w2, ln_g, ln_b, l, is_grid):
    grp = lambda i: _cond_row(i, TM_MLP, is_grid)
    vec = pl.BlockSpec((None, 1, D_MODEL), lambda i, f: (l, 0, 0))
    return pl.pallas_call(
        _mlp_kernel,
        out_shape=jax.ShapeDtypeStruct(x.shape, F32),
        grid=(x.shape[0] // TM_MLP, D_FF // TF_MLP),
        in_specs=[
            pl.BlockSpec((TM_MLP, D_MODEL), lambda i, f: (i, 0), pipeline_mode=pl.Buffered(1)),
            _mod_spec(l, 3, grp),
            _mod_spec(l, 4, grp),
            _mod_spec(l, 5, grp),
            pl.BlockSpec((None, D_MODEL, TF_MLP), lambda i, f: (l, 0, f)),
            pl.BlockSpec((None, TF_MLP, D_MODEL), lambda i, f: (l, f, 0)),
            vec, vec,
        ],
        out_specs=pl.BlockSpec((TM_MLP, D_MODEL), lambda i, f: (i, 0)),
        scratch_shapes=[pltpu.VMEM((TM_MLP, D_MODEL), BF16)],
        compiler_params=_cparams(("parallel", "arbitrary")),
        name="mlp",
    )(x, mod_all, mod_all, mod_all, w1, w2, ln_g, ln_b)


def kernel(x_prompt, x_sample, c, state_rwkv, c_ctx, w_ada, b_ada, w_in, conv_w, lora_w_up, lora_w0, lora_a_up, lora_a0, lora_g_up, rwkv_k_k, rwkv_k_a, rwkv_r_k, rwkv_ln_g, rwkv_ln_b, filt_w1, filt_b1, filt_w2, filt_b2, filt_w3, filt_freq, hyena_bias, w_out, ln1_g, ln1_b, ln2_g, ln2_b, mlp_w1, mlp_w2):
    n_ctx, ctx_seq, _ = x_prompt.shape
    n_lat, lat_seq, _ = x_sample.shape
    depth = w_ada.shape[0]
    assert ctx_seq == CTX_SEQ and lat_seq == LAT_SEQ and (n_ctx * ctx_seq) % TM == 0
    assert 1 + n_lat <= N_COND
    ctx_rows = n_ctx * ctx_seq
    lat_rows = n_lat * lat_seq

    xc = x_prompt.reshape(ctx_rows, D_MODEL)
    xl = x_sample.reshape(lat_rows, D_MODEL)
    cond = jnp.concatenate([c_ctx[None, :], c, jnp.zeros((N_COND - 1 - n_lat, D_MODEL), F32)], axis=0)
    mod_all = _mod_call(cond, w_ada, b_ada).reshape(depth, N_COND, 6, 1, D_MODEL)

    w_in_t = jnp.swapaxes(w_in, 1, 2)
    w_lo_bf = jnp.pad(w_in[:, :, N_CONV:].astype(BF16), ((0, 0), (0, 0), (0, TN_IN - (D_IN - N_CONV))))
    conv_w9 = conv_w.reshape(depth, 9, N_CONV)
    w_out_bf = w_out.astype(BF16)
    row = lambda a: a.reshape(depth, 1, a.shape[-1])
    lp = {
        "w_up": lora_w_up, "w0": lora_w0[:, :, None, :], "a_up": lora_a_up, "a0": lora_a0[:, :, None, :],
        "k_k": row(rwkv_k_k), "k_a": row(rwkv_k_a), "r_k": rwkv_r_k.reshape(depth, 1, D_RWKV),
        "ln_g": row(rwkv_ln_g), "ln_b": row(rwkv_ln_b),
        "g_up": jnp.pad(lora_g_up, ((0, 0), (0, LORA_G_PAD - LORA_G), (0, 0))),
    }
    fp = {
        "w1": jnp.pad(filt_w1, ((0, 0), (0, FILT_EMB_PAD - FILT_EMB), (0, 0))), "b1": row(filt_b1),
        "w2": filt_w2, "b2": row(filt_b2), "w3": filt_w3, "freq": filt_freq,
    }
    ln1_g, ln1_b, ln2_g, ln2_b = row(ln1_g), row(ln1_b), row(ln2_g), row(ln2_b)
    state_in = state_rwkv.astype(F32)

    def trunk_layer(x, l, is_grid, n_seq, seq_len, s0, hyena_tc, hyena_sub):
        proj = _inproj_call(x, mod_all, w_in_t, w_lo_bf, conv_w9, l, is_grid)
        scan = _wkv_call(proj, lp, l, s0, n_seq, seq_len, s0 is None)
        a_out = _rwkvpost_call(scan[0], proj, lp, l)
        spec = _hyfilt_call(seq_len, fp, l)
        b_out = _hyena_call(proj, spec, hyena_bias, l, n_seq, seq_len, hyena_tc, hyena_sub)
        x = _outproj_call(a_out, b_out, x, mod_all, w_out_bf, ln1_g, ln1_b, l, is_grid)
        x = _mlp_call(x, mod_all, mlp_w1, mlp_w2, ln2_g, ln2_b, l, is_grid)
        return x, (scan[1] if s0 is None else None)

    ctx_states = []
    lat_sub = 2 if n_lat % 2 == 0 else 1
    for l in range(depth):
        xc, s_ctx = trunk_layer(xc, l, False, n_ctx, ctx_seq, None, D_HYENA, 1)
        ctx_states.append(s_ctx)
        xl, _ = trunk_layer(xl, l, True, n_lat, lat_seq, state_in, 256, lat_sub)

    y_prompt = xc.reshape(n_ctx, ctx_seq, D_MODEL)
    y_sample = xl.reshape(n_lat, lat_seq, D_MODEL)
    new_state = jnp.stack(ctx_states, axis=1).astype(x_prompt.dtype)
    return (y_prompt, y_sample, new_state)
```
